```python
import jax
import jax.numpy as jnp
from jax import lax
import numpy as np

D_MODEL = 1024
BATCH = 16
SEQ = 4096
DEPTH = 4
DEC_BATCH = 16
DEC_SEQ = 16
PAST_LEN = 4096

CHUNK = 64
SUB_BLOCK = 16
QBLOCK = 128
N_A = DEPTH // 2
N_B = DEPTH - N_A
MIX_WIDTH = D_MODEL
MEM_WIDTH = D_MODEL // 4
MAIN_WIDTH = MIX_WIDTH - MEM_WIDTH
HG_DK = 128
HG_DV = 128
HG_HEADS = MAIN_WIDTH // HG_DK
FOX_HD = 64
FOX_HEADS = MAIN_WIDTH // FOX_HD
MEM_HEADS = 4
MEM_HD = MEM_WIDTH // MEM_HEADS
N_MEM = 256
D_FF = -(-(8 * D_MODEL) // (3 * 256)) * 256
A_IN = 4 * MAIN_WIDTH + MEM_WIDTH
B_IN = MAIN_WIDTH + MEM_WIDTH
KV_OUT = 2 * MAIN_WIDTH + FOX_HEADS
EPS = 1e-6
K_MAX = 0.999999
FOX_F_BIAS = 3.0
NEG = -1e30

kernel_name = "yoco_hgrn2_fox_stream_step"


def _rms_norm(x, g):
    xf = x.astype(jnp.float32)
    y = xf * lax.rsqrt(jnp.mean(xf * xf, axis=-1, keepdims=True) + EPS)
    return (y * g.astype(jnp.float32)).astype(x.dtype)


def _hgrn_chunk_step(s, inp):
    q, k, v, g = inp
    bsz, h, c, dk = q.shape
    ns = c // SUB_BLOCK
    b = jnp.cumsum(g, axis=2)
    o_inter = jnp.einsum("bhck,bhkv->bhcv", q * jnp.exp(b), s)
    qs = q.reshape(bsz, h, ns, SUB_BLOCK, dk)
    ks = k.reshape(bsz, h, ns, SUB_BLOCK, dk)
    bs = b.reshape(bsz, h, ns, SUB_BLOCK, dk)
    ref = jnp.concatenate([jnp.zeros_like(bs[:, :, :1, 0]), bs[:, :, :-1, -1]], axis=2)
    q_hat = qs * jnp.exp(bs - ref[:, :, :, None, :])
    e_off = jnp.minimum(ref[:, :, :, None, None, :] - bs[:, :, None, :, :, :], 0.0)
    k_hat = ks[:, :, None] * jnp.exp(e_off)
    a_off = jnp.einsum("bhitk,bhijsk->bhijts", q_hat, k_hat)
    e_diag = jnp.minimum(bs[:, :, :, :, None, :] - bs[:, :, :, None, :, :], 0.0)
    a_diag = jnp.einsum("bhitk,bhisk,bhitsk->bhits", qs, ks, jnp.exp(e_diag))
    eye = np.eye(ns, dtype=bool)[:, :, None, None]
    lower = np.tril(np.ones((ns, ns), dtype=bool), -1)[:, :, None, None]
    tri = np.tril(np.ones((SUB_BLOCK, SUB_BLOCK), dtype=bool))
    a = jnp.where(eye, jnp.where(tri, a_diag[:, :, :, None], 0.0), jnp.where(lower, a_off, 0.0))
    a = a.transpose(0, 1, 2, 4, 3, 5).reshape(bsz, h, c, c)
    o = o_inter + jnp.einsum("bhts,bhsv->bhtv", a, v)
    b_last = b[:, :, -1]
    s_new = jnp.exp(b_last)[..., None] * s + jnp.einsum(
        "bhck,bhcv->bhkv", k * jnp.exp(b_last[:, :, None] - b), v)
    return s_new, o


def _hgrn_recurrence(q, k, v, g, s0):
    bsz, t, h, _ = q.shape
    pad = (-t) % CHUNK
    n = (t + pad) // CHUNK

    def to_chunks(x):
        x = jnp.pad(x, ((0, 0), (0, pad), (0, 0), (0, 0)))
        return x.reshape(bsz, n, CHUNK, h, x.shape[-1]).transpose(1, 0, 3, 2, 4)

    s_fin, o = lax.scan(_hgrn_chunk_step, s0, (to_chunks(q), to_chunks(k), to_chunks(v), to_chunks(g)))
    o = o.transpose(1, 0, 3, 2, 4).reshape(bsz, n * CHUNK, h, -1)[:, :t]
    return o, s_fin


def _hgrn2_mixer(pq, pf, pi, pg, lb, gnorm, s0):
    bsz, t, _ = pq.shape
    shp = (bsz, t, HG_HEADS, HG_DK)
    q = jax.nn.silu(pq.astype(jnp.float32)).reshape(shp)
    z = pf.astype(jnp.float32).reshape(shp)
    lbh = lb.astype(jnp.float32).reshape(HG_HEADS, HG_DK)
    k = jnp.minimum((1.0 - lbh) * jax.nn.sigmoid(-z), K_MAX)
    log_f = jnp.log1p(-k)
    v = pi.astype(jnp.float32).reshape(bsz, t, HG_HEADS, HG_DV)
    o, s_fin = _hgrn_recurrence(q, k, v, log_f, s0.astype(jnp.float32))
    gate = jax.nn.silu(pg.astype(jnp.float32)).reshape(bsz, t, HG_HEADS, HG_DV)
    o = _rms_norm(o, gnorm) * gate
    return o.reshape(bsz, t, MAIN_WIDTH).astype(pq.dtype), s_fin


def _fox_block(q, k, v, c_q, c_k, t0):
    s = jnp.einsum("bqhd,bkhd->bhqk", q, k).astype(jnp.float32) * (FOX_HD ** -0.5)
    s = s + (jnp.swapaxes(c_q, 1, 2)[..., :, None] - jnp.swapaxes(c_k, 1, 2)[..., None, :])
    q_pos = t0 + jnp.arange(q.shape[1])
    k_pos = jnp.arange(k.shape[1])
    s = jnp.where(k_pos[None, :] <= q_pos[:, None], s, NEG)
    p = jax.nn.softmax(s, axis=-1)
    return jnp.einsum("bhqk,bkhd->bqhd", p.astype(v.dtype), v)


def _fox_attention(q, k_all, v_all, c_all, t0):
    bsz, t = q.shape[:2]
    if t % QBLOCK == 0:
        nb = t // QBLOCK
        qb = q.reshape(bsz, nb, QBLOCK, FOX_HEADS, FOX_HD).swapaxes(0, 1)
        cb = c_all[:, t0:].reshape(bsz, nb, QBLOCK, FOX_HEADS).swapaxes(0, 1)
        starts = t0 + QBLOCK * jnp.arange(nb)
        o = lax.map(lambda blk: _fox_block(blk[0], k_all, v_all, blk[1], c_all, blk[2]), (qb, cb, starts))
        o = o.swapaxes(0, 1)
    else:
        o = _fox_block(q, k_all, v_all, c_all[:, t0:], c_all, t0)
    return o.reshape(bsz, t, MAIN_WIDTH)


def _shared_kv(h, g_norm, w_kv, b_f, g_k):
    bsz, t, _ = h.shape
    p = _rms_norm(h, g_norm) @ w_kv
    k = _rms_norm(p[..., :MAIN_WIDTH].reshape(bsz, t, FOX_HEADS, FOX_HD), g_k)
    v = p[..., MAIN_WIDTH:2 * MAIN_WIDTH].reshape(bsz, t, FOX_HEADS, FOX_HD)
    logf = jax.nn.log_sigmoid((p[..., 2 * MAIN_WIDTH:] + b_f).astype(jnp.float32))
    return k, v, logf


def _memory_kv(mem, g_norm, w_kv, g_k):
    bsz, m, _ = mem.shape
    k, v = jnp.split(_rms_norm(mem, g_norm) @ w_kv, 2, axis=-1)
    k = _rms_norm(k.reshape(bsz, m, MEM_HEADS, MEM_HD), g_k)
    return k, v.reshape(bsz, m, MEM_HEADS, MEM_HD)


def _memory_attend(pm, g_q, mem_k, mem_v):
    bsz, t, _ = pm.shape
    q = _rms_norm(pm.reshape(bsz, t, MEM_HEADS, MEM_HD), g_q)
    s = jnp.einsum("bthd,bmhd->bhtm", q, mem_k).astype(jnp.float32) * (MEM_HD ** -0.5)
    p = jax.nn.softmax(s, axis=-1)
    o = jnp.einsum("bhtm,bmhd->bthd", p.astype(mem_v.dtype), mem_v)
    return o.reshape(bsz, t, MEM_WIDTH)


def _swiglu(h, g, w_up, w_down):
    gate, up = jnp.split(_rms_norm(h, g) @ w_up, 2, axis=-1)
    return (jax.nn.silu(gate) * up) @ w_down


def _trunk(x, mem_k, mem_v, hg_states, past, lower_bounds, prm):
    bsz, t, _ = x.shape
    h = x
    new_states = []
    shared = None
    new_kv = None
    for l in range(DEPTH):
        a = _rms_norm(h, prm["norm_mix"][l])
        if l < N_A:
            proj = a @ prm["w_in_a"][l]
            pq, pf, pi, pg, pm = jnp.split(
                proj, [MAIN_WIDTH, 2 * MAIN_WIDTH, 3 * MAIN_WIDTH, 4 * MAIN_WIDTH], axis=-1)
            o_main, s_new = _hgrn2_mixer(pq, pf, pi, pg, lower_bounds[l], prm["hg_gnorm"][l], hg_states[l])
            new_states.append(s_new)
        else:
            j = l - N_A
            proj = a @ prm["w_in_b"][j]
            pq, pm = jnp.split(proj, [MAIN_WIDTH], axis=-1)
            q = _rms_norm(pq.reshape(bsz, t, FOX_HEADS, FOX_HD), prm["fox_gq"][j])
            o_main = _fox_attention(q, shared[0], shared[1], shared[2], shared[3])
        o_mem = _memory_attend(pm, prm["mem_gq"][l], mem_k[l], mem_v[l])
        h = h + jnp.concatenate([o_main, o_mem], axis=-1) @ prm["w_out"][l]
        h = h + _swiglu(h, prm["norm_ffn"][l], prm["w_ffn_up"][l], prm["w_ffn_down"][l])
        if l == N_A - 1:
            k_new, v_new, logf_new = _shared_kv(h, prm["norm_kv"], prm["w_kv"], prm["b_f"], prm["fox_gk"])
            new_kv = (k_new, v_new, logf_new)
            if past is None:
                k_all, v_all, logf_all = k_new, v_new, logf_new
            else:
                k_all = jnp.concatenate([past[0], k_new.astype(past[0].dtype)], axis=1)
                v_all = jnp.concatenate([past[1], v_new.astype(past[1].dtype)], axis=1)
                logf_all = jnp.concatenate([past[2].astype(jnp.float32), logf_new], axis=1)
            c_all = jnp.cumsum(logf_all, axis=1)
            shared = (k_all, v_all, c_all, k_all.shape[1] - t)
    return h, new_states, new_kv


def setup_inputs(seed: int = 0) -> dict:
    key = jax.random.key(seed)
    ks = jax.random.split(key, 28)
    f32 = jnp.float32
    d = D_MODEL

    def nrm(k, shape, scale=1.0):
        return jax.random.normal(k, shape, f32) * scale

    def gain(k, shape):
        return 1.0 + 0.02 * jax.random.normal(k, shape, f32)

    return {
        "x_prompt": nrm(ks[0], (BATCH, SEQ, d)),
        "x_sample": nrm(ks[1], (DEC_BATCH, DEC_SEQ, d)),
        "mem_prompt": nrm(ks[2], (BATCH, N_MEM, d)),
        "state_hgrn_0": nrm(ks[3], (DEC_BATCH, HG_HEADS, HG_DK, HG_DV), 0.5),
        "state_hgrn_1": nrm(ks[4], (DEC_BATCH, HG_HEADS, HG_DK, HG_DV), 0.5),
        "cache_fox_k": nrm(ks[5], (DEC_BATCH, PAST_LEN, FOX_HEADS, FOX_HD)),
        "cache_fox_v": nrm(ks[6], (DEC_BATCH, PAST_LEN, FOX_HEADS, FOX_HD)),
        "cache_fox_logf": jax.nn.log_sigmoid(FOX_F_BIAS + nrm(ks[7], (DEC_BATCH, PAST_LEN, FOX_HEADS))),
        "cache_mem_k": nrm(ks[8], (DEPTH, DEC_BATCH, N_MEM, MEM_HEADS, MEM_HD)),
        "cache_mem_v": nrm(ks[9], (DEPTH, DEC_BATCH, N_MEM, MEM_HEADS, MEM_HD)),
        "norm_mix": gain(ks[10], (DEPTH, d)),
        "w_in_a": nrm(ks[11], (N_A, d, A_IN), d ** -0.5),
        "lb_logits": nrm(ks[12], (N_A, MAIN_WIDTH), 0.1),
        "hg_gnorm": gain(ks[13], (N_A, HG_DV)),
        "w_in_b": nrm(ks[14], (N_B, d, B_IN), d ** -0.5),
        "fox_gq": gain(ks[15], (N_B, FOX_HD)),
        "norm_kv": gain(ks[16], (d,)),
        "w_kv": nrm(ks[17], (d, KV_OUT), d ** -0.5),
        "b_f": FOX_F_BIAS + nrm(ks[18], (FOX_HEADS,), 0.5),
        "fox_gk": gain(ks[19], (FOX_HD,)),
        "norm_mem": gain(ks[20], (DEPTH, d)),
        "w_mem_kv": nrm(ks[21], (DEPTH, d, 2 * MEM_WIDTH), d ** -0.5),
        "mem_gq": gain(ks[22], (DEPTH, MEM_HD)),
        "mem_gk": gain(ks[23], (DEPTH, MEM_HD)),
        "w_out": nrm(ks[24], (DEPTH, MIX_WIDTH, d), (2 * DEPTH * MIX_WIDTH) ** -0.5),
        "norm_ffn": gain(ks[25], (DEPTH, d)),
        "w_ffn_up": nrm(ks[26], (DEPTH, d, 2 * D_FF), d ** -0.5),
        "w_ffn_down": nrm(ks[27], (DEPTH, D_FF, d), (2 * DEPTH * D_FF) ** -0.5),
    }


def reference(x_prompt, x_sample, mem_prompt, state_hgrn_0, state_hgrn_1, cache_fox_k, cache_fox_v,
              cache_fox_logf, cache_mem_k, cache_mem_v, norm_mix, w_in_a, lb_logits, hg_gnorm, w_in_b,
              fox_gq, norm_kv, w_kv, b_f, fox_gk, norm_mem, w_mem_kv, mem_gq, mem_gk, w_out, norm_ffn,
              w_ffn_up, w_ffn_down):
    prm = {
        "norm_mix": norm_mix, "w_in_a": w_in_a, "hg_gnorm": hg_gnorm, "w_in_b": w_in_b,
        "fox_gq": fox_gq, "norm_kv": norm_kv, "w_kv": w_kv, "b_f": b_f, "fox_gk": fox_gk,
        "mem_gq": mem_gq, "w_out": w_out, "norm_ffn": norm_ffn, "w_ffn_up": w_ffn_up,
        "w_ffn_down": w_ffn_down,
    }
    p_lb = jax.nn.softmax(lb_logits.astype(jnp.float32), axis=0)
    lower_bounds = jnp.cumsum(p_lb, axis=0) - p_lb[0]

    mk, mv = [], []
    for l in range(DEPTH):
        k_l, v_l = _memory_kv(mem_prompt, norm_mem[l], w_mem_kv[l], mem_gk[l])
        mk.append(k_l)
        mv.append(v_l)
    p_mem_k = jnp.stack(mk)
    p_mem_v = jnp.stack(mv)
    s_zero = jnp.zeros((x_prompt.shape[0], HG_HEADS, HG_DK, HG_DV), jnp.float32)
    y_prompt, p_states, p_kv = _trunk(x_prompt, p_mem_k, p_mem_v, [s_zero] * N_A, None, lower_bounds, prm)

    y_sample, s_states, s_kv = _trunk(x_sample, cache_mem_k, cache_mem_v, [state_hgrn_0, state_hgrn_1],
                                      (cache_fox_k, cache_fox_v, cache_fox_logf), lower_bounds, prm)
    return (y_prompt, y_sample, p_states[0], p_states[1], p_kv[0], p_kv[1], p_kv[2], p_mem_k, p_mem_v,
            s_states[0], s_states[1], s_kv[0], s_kv[1], s_kv[2])
```

```python
import functools

import numpy as np
import jax
import jax.numpy as jnp
from jax import lax
from jax.experimental import pallas as pl
from jax.experimental.pallas import tpu as pltpu

F32 = jnp.float32
BF16 = jnp.bfloat16

EPS = 1e-6
K_MAX = 0.999999
NEG = -1e30
LOG2E = 1.4426950408889634

LANES = 128
HEAD_DIM = 64
CHUNK = 64
HG_DK = 128
FF_TILE = 256
ROW_TILE = 512
FOX_TQ = 256
FOX_TK = 256
PREP_TILE = 256
N_SPLIT = 3
ONES_LANE = 36
VMEM_LIMIT_BYTES = 56 * 1024 * 1024


def _cparams(*sem):
    return pltpu.CompilerParams(dimension_semantics=sem, vmem_limit_bytes=VMEM_LIMIT_BYTES)


def _resident(shape):
    nd = len(shape)
    return pl.BlockSpec(shape, lambda *_: (0,) * nd, pipeline_mode=pl.Buffered(1))


def _dot(a, b):
    return jnp.dot(a, b, preferred_element_type=F32)


def _dot_nt(a, b):
    return lax.dot_general(a, b, (((1,), (1,)), ((), ())), preferred_element_type=F32)


def _dot_tn(a, b):
    return lax.dot_general(a, b, (((0,), (0,)), ((), ())), preferred_element_type=F32)


def _split_bf16(x, n):
    parts = []
    r = x
    for _ in range(n):
        p = r.astype(BF16)
        parts.append(p)
        r = r - p.astype(F32)
    return parts


def _rms(x, g):
    ms = jnp.mean(x * x, axis=-1, keepdims=True)
    return x * lax.rsqrt(ms + EPS) * g


def _silu(x):
    return x / (1.0 + jnp.exp(-x))


def _even_lanes(shape):
    return lax.broadcasted_iota(jnp.int32, shape, len(shape) - 1) < HEAD_DIM


def _pair_rms(x2, g2):
    even = _even_lanes(x2.shape)
    sq = x2 * x2
    se = jnp.sum(jnp.where(even, sq, 0.0), axis=-1, keepdims=True)
    so = jnp.sum(jnp.where(even, 0.0, sq), axis=-1, keepdims=True)
    ms = jnp.where(even, se, so) * (1.0 / HEAD_DIM)
    return x2 * lax.rsqrt(ms + EPS) * g2


def _proj_split_kernel(x_ref, g_ref, w_ref, *out_refs, widths):
    xn = _rms(x_ref[...], g_ref[...]).astype(BF16)
    off = 0
    for o_ref, wd in zip(out_refs, widths):
        o_ref[...] = _dot(xn, w_ref[:, off:off + wd]).astype(o_ref.dtype)
        off += wd


def _proj_split(x, g, w, widths):
    n, d = x.shape
    tm = min(ROW_TILE, n)
    return pl.pallas_call(
        functools.partial(_proj_split_kernel, widths=widths),
        grid=(n // tm,),
        in_specs=[pl.BlockSpec((tm, d), lambda i: (i, 0)), _resident((1, d)), _resident(w.shape)],
        out_specs=[pl.BlockSpec((tm, wd), lambda i: (i, 0)) for wd in widths],
        out_shape=[jax.ShapeDtypeStruct((n, wd), F32) for wd in widths],
        compiler_params=_cparams("parallel"), name="proj_a",
    )(x, g, w)


HG_LEVELS = (1, 2, 4, 8, 16, 32)


def _hgrn_consts():
    c = CHUNK
    nl = len(HG_LEVELS)
    m = np.zeros((nl + 2, c, c), np.float32)
    masks = np.zeros((nl + 1, c, c), np.float32)
    t = np.arange(c)
    for li, h in enumerate(HG_LEVELS):
        blk = t // (2 * h)
        second = (t // h) % 2 == 1
        boundary = blk * 2 * h + h
        for r in range(c):
            if second[r]:
                m[li, r, boundary[r]:r + 1] = 1.0
            else:
                m[li, r, r + 1:boundary[r]] = 1.0
        masks[li] = ((blk[:, None] == blk[None, :]) & second[:, None] & ~second[None, :])
    m[nl] = np.tril(np.ones((c, c)))
    m[nl + 1] = np.triu(np.ones((c, c)), 1)
    masks[nl] = np.eye(c)
    return m.reshape((nl + 2) * c, c), masks


def _hgrn_kernel(pq_ref, pf_ref, pi_ref, pg_ref, lbl_ref, gn_ref, s0_ref, m_ref, lm_ref,
                 o_ref, sfin_ref, st_ref, *, n_heads, valid_len, layer):
    c_idx = pl.program_id(1)
    n_chunks = pl.num_programs(1)
    nl = len(HG_LEVELS)

    @pl.when(c_idx == 0)
    def _():
        for h in range(n_heads):
            st_ref[h] = s0_ref[h].T

    rows = [lbl_ref[i:i + 1, :] for i in range(lbl_ref.shape[0])]
    mx = functools.reduce(jnp.maximum, rows)
    es = [jnp.exp(r - mx) for r in rows]
    tot = functools.reduce(lambda a, b: a + b, es)
    ps = [e / tot for e in es]
    cum = ps[0]
    for i in range(1, layer + 1):
        cum = cum + ps[i]
    lb = cum - ps[0]

    m_all = m_ref[...]
    gn = gn_ref[...]
    if valid_len < CHUNK:
        row_ok = lax.broadcasted_iota(jnp.int32, (CHUNK, HG_DK), 0) < valid_len

    for h in range(n_heads):
        sl = slice(h * HG_DK, (h + 1) * HG_DK)
        q = _silu(pq_ref[:, sl])
        z = pf_ref[:, sl]
        kk = jnp.minimum((1.0 - lb[:, sl]) * (1.0 / (1.0 + jnp.exp(z))), K_MAX)
        g = jnp.log1p(-kk)
        if valid_len < CHUNK:
            kk = jnp.where(row_ok, kk, 0.0)
            g = jnp.where(row_ok, g, 0.0)
        v_b = pi_ref[:, sl].astype(BF16)

        g_hi, g_lo = _split_bf16(g, 2)
        e_all = jnp.exp(_dot(m_all, g_hi) + _dot(m_all, g_lo))

        a = lm_ref[nl] * _dot_nt(q.astype(BF16), kk.astype(BF16))
        for li in range(nl):
            el = e_all[li * CHUNK:(li + 1) * CHUNK]
            a = a + lm_ref[li] * _dot_nt((q * el).astype(BF16), (kk * el).astype(BF16))

        e_b = e_all[nl * CHUNK:(nl + 1) * CHUNK]
        e_r = e_all[(nl + 1) * CHUNK:(nl + 2) * CHUNK]
        st = st_ref[h]
        o = _dot_nt((q * e_b).astype(BF16), st.astype(BF16)) + _dot(a.astype(BF16), v_b)
        st_ref[h] = e_b[CHUNK - 1:CHUNK, :] * st + _dot_tn(v_b, (kk * e_r).astype(BF16))

        gate = _silu(pg_ref[:, sl])
        o_ref[:, sl] = (_rms(o, gn) * gate).astype(o_ref.dtype)

    @pl.when(c_idx == n_chunks - 1)
    def _():
        for h in range(n_heads):
            sfin_ref[h] = st_ref[h].T


def _hgrn(pq, pf, pi, pg, lb_logits, gnorm, s0, layer, valid_len):
    b, t, w = pq.shape
    n_heads = w // HG_DK
    m_np, masks_np = _hgrn_consts()
    m_all = jnp.asarray(m_np, BF16)
    masks = jnp.asarray(masks_np, F32)
    tok = pl.BlockSpec((None, CHUNK, w), lambda i, c: (i, c, 0))
    st_spec = pl.BlockSpec((None, n_heads, HG_DK, HG_DK), lambda i, c: (i, 0, 0, 0))
    return pl.pallas_call(
        functools.partial(_hgrn_kernel, n_heads=n_heads, valid_len=valid_len, layer=layer),
        grid=(b, t // CHUNK),
        in_specs=[tok, tok, tok, tok, _resident(lb_logits.shape), _resident((1, HG_DK)), st_spec,
                  _resident(m_all.shape), _resident(masks.shape)],
        out_specs=[tok, st_spec],
        out_shape=[jax.ShapeDtypeStruct((b, t, w), BF16),
                   jax.ShapeDtypeStruct((b, n_heads, HG_DK, HG_DK), F32)],
        scratch_shapes=[pltpu.VMEM((n_heads, HG_DK, HG_DK), F32)],
        compiler_params=_cparams("parallel", "arbitrary"), name="hgrn",
    )(pq, pf, pi, pg, lb_logits, gnorm.reshape(1, HG_DK), s0, m_all, masks)


def _post_kernel(h_ref, om_ref, pm_ref, mk_ref, mv_ref, gq_ref, wo_ref, gf_ref, wu_ref, wd_ref,
                 out_ref, *, main_width, d_ff):
    x = h_ref[...]
    attn = _dot(om_ref[...], wo_ref[:main_width, :])
    gq2 = gq_ref[...]
    n_pairs = pm_ref.shape[-1] // LANES
    for p in range(n_pairs):
        sl = slice(p * LANES, (p + 1) * LANES)
        qn = _pair_rms(pm_ref[:, sl], gq2) * (HEAD_DIM ** -0.5)
        even = _even_lanes(qn.shape)
        k2 = mk_ref[:, sl]
        v2 = mv_ref[:, sl]
        halves = []
        for own in (even, jnp.logical_not(even)):
            s = _dot_nt(jnp.where(own, qn, 0.0).astype(BF16), k2)
            e = jnp.exp(s - jnp.max(s, axis=-1, keepdims=True))
            l = jnp.sum(e, axis=-1, keepdims=True)
            halves.append(_dot(e.astype(BF16), v2) / l)
        o2 = jnp.where(even, halves[0], halves[1])
        attn = attn + _dot(o2.astype(BF16), wo_ref[main_width + p * LANES:main_width + (p + 1) * LANES, :])
    h1 = x + attn

    xn = _rms(h1, gf_ref[...]).astype(BF16)
    acc = jnp.zeros_like(h1)
    for f in range(d_ff // FF_TILE):
        gate = _dot(xn, wu_ref[:, f * FF_TILE:(f + 1) * FF_TILE])
        up = _dot(xn, wu_ref[:, d_ff + f * FF_TILE:d_ff + (f + 1) * FF_TILE])
        act = (_silu(gate) * up).astype(BF16)
        acc = acc + _dot(act, wd_ref[f * FF_TILE:(f + 1) * FF_TILE, :])
    out_ref[...] = h1 + acc


def _post(h, o_main, pm, mem_k, mem_v, gq2, w_out, g_ffn, w_up, w_down):
    b, t, d = h.shape
    tm = min(ROW_TILE, t)
    main_width = o_main.shape[-1]
    mem_width = pm.shape[-1]
    n_mem = mem_k.shape[1]
    d_ff = w_down.shape[0]
    row = lambda wd: pl.BlockSpec((None, tm, wd), lambda i, r: (i, r, 0))
    mem = pl.BlockSpec((None, n_mem, mem_width), lambda i, r: (i, 0, 0))
    return pl.pallas_call(
        functools.partial(_post_kernel, main_width=main_width, d_ff=d_ff),
        grid=(b, t // tm),
        in_specs=[row(d), row(main_width), row(mem_width), mem, mem, _resident((1, LANES)),
                  _resident(w_out.shape), _resident((1, d)), _resident(w_up.shape),
                  _resident(w_down.shape)],
        out_specs=row(d),
        out_shape=jax.ShapeDtypeStruct((b, t, d), F32),
        compiler_params=_cparams("parallel", "parallel"), name="post",
    )(h, o_main, pm, mem_k, mem_v, gq2, w_out, g_ffn, w_up, w_down)


def _kv_proj_kernel(*refs, has_f, n_f):
    if has_f:
        (x_ref, g_ref, wk_ref, wv_ref, gk_ref, wf_ref, bf_ref,
         k_ref, v_ref, lf_ref, lfp_ref) = refs
    else:
        x_ref, g_ref, wk_ref, wv_ref, gk_ref, k_ref, v_ref, kb_ref, vb_ref = refs
    xn = _rms(x_ref[...], g_ref[...]).astype(BF16)
    gk2 = gk_ref[...]
    for p in range(k_ref.shape[-1] // LANES):
        sl = slice(p * LANES, (p + 1) * LANES)
        k2 = _pair_rms(_dot(xn, wk_ref[:, sl]), gk2)
        k_ref[:, sl] = k2
        if not has_f:
            kb_ref[:, sl] = k2.astype(BF16)
    v = _dot(xn, wv_ref[...])
    v_ref[...] = v
    if has_f:
        y = _dot(xn, wf_ref[...]) + bf_ref[...]
        lf = jnp.minimum(y, 0.0) - jnp.log1p(jnp.exp(-jnp.abs(y)))
        lane = lax.broadcasted_iota(jnp.int32, lf.shape, 1)
        lf = jnp.where(lane < n_f, lf, 0.0)
        lfp_ref[...] = lf
        lf_ref[...] = lf[:, :n_f]
    else:
        vb_ref[...] = v.astype(BF16)


def _kv_proj(x, g, wk, wv, gk2, wf=None, bf=None, n_f=0):
    n, d = x.shape
    tm = min(ROW_TILE, n)
    wk_w = wk.shape[1]
    has_f = wf is not None
    row = lambda wd: pl.BlockSpec((tm, wd), lambda i: (i, 0))
    in_specs = [row(d), _resident((1, d)), _resident(wk.shape), _resident(wv.shape),
                _resident((1, LANES))]
    args = [x, g, wk, wv, gk2]
    if has_f:
        in_specs += [_resident(wf.shape), _resident((1, LANES))]
        args += [wf, bf]
        out_specs = [row(wk_w), row(wk_w), row(n_f), row(LANES)]
        out_shape = [jax.ShapeDtypeStruct((n, wk_w), F32), jax.ShapeDtypeStruct((n, wk_w), F32),
                     jax.ShapeDtypeStruct((n, n_f), F32), jax.ShapeDtypeStruct((n, LANES), F32)]
    else:
        out_specs = [row(wk_w)] * 4
        out_shape = [jax.ShapeDtypeStruct((n, wk_w), F32), jax.ShapeDtypeStruct((n, wk_w), F32),
                     jax.ShapeDtypeStruct((n, wk_w), BF16), jax.ShapeDtypeStruct((n, wk_w), BF16)]
    return pl.pallas_call(
        functools.partial(_kv_proj_kernel, has_f=has_f, n_f=n_f),
        grid=(n // tm,),
        in_specs=in_specs, out_specs=out_specs, out_shape=out_shape,
        compiler_params=_cparams("parallel"), name="kv_proj",
    )(*args)


def _bias_lane_base(h):
    return h * LANES + (HEAD_DIM if h % 2 == 0 else 0)


def _pack_consts(n_heads):
    p = np.zeros((N_SPLIT, LANES, LANES), np.float32)
    for s in range(N_SPLIT):
        for h in range(n_heads):
            p[s, h, s * n_heads + h] = 1.0
    ones_row = np.zeros((1, LANES), np.float32)
    ones_row[0, ONES_LANE] = 1.0
    s_k = np.zeros((LANES, n_heads * LANES), np.float32)
    s_q = np.zeros((LANES, n_heads * LANES), np.float32)
    for h in range(n_heads):
        base = _bias_lane_base(h)
        for s in range(N_SPLIT):
            s_q[s * n_heads + h, base + s] = 1.0
            s_q[ONES_LANE, base + N_SPLIT + s] = 1.0
            s_k[ONES_LANE, base + s] = 1.0
            s_k[s * n_heads + h, base + N_SPLIT + s] = -1.0
    return p, ones_row, s_q, s_k


def _own_half(shape, h):
    even = _even_lanes(shape)
    return even if h % 2 == 0 else jnp.logical_not(even)


def _kv_prep_kernel(k_ref, v_ref, lf_ref, tril_ref, p_ref, ones_ref, sk_ref,
                    kaug_ref, vb_ref, c3_ref, carry_ref, *, n_heads):
    @pl.when(pl.program_id(1) == 0)
    def _():
        carry_ref[...] = jnp.zeros_like(carry_ref)

    tril = tril_ref[...]
    cs = carry_ref[...]
    for part in _split_bf16(lf_ref[...], N_SPLIT):
        cs = cs + _dot(tril, part)
    tl = cs.shape[0]
    carry_ref[...] = cs[tl - 1:tl, :]

    c3 = ones_ref[...]
    for s, part in enumerate(_split_bf16(cs * LOG2E, N_SPLIT)):
        c3 = c3 + _dot(part, p_ref[s])
    c3 = c3.astype(BF16)
    c3_ref[...] = c3
    kbias = _dot(c3, sk_ref[...])
    for h in range(n_heads):
        pair = slice((h // 2) * LANES, (h // 2 + 1) * LANES)
        slab = slice(h * LANES, (h + 1) * LANES)
        k2 = k_ref[:, pair]
        kaug_ref[:, slab] = jnp.where(_own_half(k2.shape, h), k2, kbias[:, slab]).astype(BF16)
    vb_ref[...] = v_ref[...].astype(BF16)


def _kv_prep(k, v, lf_pad, n_heads):
    b, l, w = k.shape
    tl = min(PREP_TILE, l)
    p_np, ones_np, _, sk_np = _pack_consts(n_heads)
    tril = jnp.asarray(np.tril(np.ones((tl, tl), np.float32)), BF16)
    row = lambda wd: pl.BlockSpec((None, tl, wd), lambda i, j: (i, j, 0))
    return pl.pallas_call(
        functools.partial(_kv_prep_kernel, n_heads=n_heads),
        grid=(b, l // tl),
        in_specs=[row(w), row(w), row(LANES), _resident((tl, tl)), _resident(p_np.shape),
                  _resident((1, LANES)), _resident(sk_np.shape)],
        out_specs=[row(n_heads * LANES), row(w), row(LANES)],
        out_shape=[jax.ShapeDtypeStruct((b, l, n_heads * LANES), BF16),
                   jax.ShapeDtypeStruct((b, l, w), BF16),
                   jax.ShapeDtypeStruct((b, l, LANES), BF16)],
        scratch_shapes=[pltpu.VMEM((1, LANES), F32)],
        compiler_params=_cparams("parallel", "arbitrary"), name="kv_prep",
    )(k, v, lf_pad, tril, jnp.asarray(p_np, BF16), jnp.asarray(ones_np, F32),
      jnp.asarray(sk_np, BF16))


def _proj_b_kernel(x_ref, g_ref, wq_ref, wm_ref, gq_ref, c3_ref, sq_ref, qaug_ref, pm_ref,
                   *, n_heads):
    xn = _rms(x_ref[...], g_ref[...]).astype(BF16)
    pm_ref[...] = _dot(xn, wm_ref[...])
    qbias = _dot(c3_ref[...], sq_ref[...])
    gq2 = gq_ref[...]
    for p in range(n_heads // 2):
        pair = slice(p * LANES, (p + 1) * LANES)
        qn = _pair_rms(_dot(xn, wq_ref[:, pair]), gq2) * (LOG2E * HEAD_DIM ** -0.5)
        for h in (2 * p, 2 * p + 1):
            slab = slice(h * LANES, (h + 1) * LANES)
            qaug_ref[:, slab] = jnp.where(_own_half(qn.shape, h), qn, qbias[:, slab]).astype(BF16)


def _proj_b(x, g, wq, wm, gq2, c3q, n_heads):
    n, d = x.shape
    tm = min(ROW_TILE, n)
    _, _, sq_np, _ = _pack_consts(n_heads)
    row = lambda wd: pl.BlockSpec((tm, wd), lambda i: (i, 0))
    return pl.pallas_call(
        functools.partial(_proj_b_kernel, n_heads=n_heads),
        grid=(n // tm,),
        in_specs=[row(d), _resident((1, d)), _resident(wq.shape), _resident(wm.shape),
                  _resident((1, LANES)), row(LANES), _resident(sq_np.shape)],
        out_specs=[row(n_heads * LANES), row(wm.shape[1])],
        out_shape=[jax.ShapeDtypeStruct((n, n_heads * LANES), BF16),
                   jax.ShapeDtypeStruct((n, wm.shape[1]), F32)],
        compiler_params=_cparams("parallel"), name="proj_b",
    )(x, g, wq, wm, gq2, c3q, jnp.asarray(sq_np, BF16))


def _fox_kernel(q_ref, k_ref, v_ref, o_ref, m_ref, l_ref, acc_ref, *, t0, tq, tk):
    i = pl.program_id(2)
    j = pl.program_id(3)
    n_k = pl.num_programs(3)

    @pl.when(j == 0)
    def _():
        m_ref[...] = jnp.full_like(m_ref, NEG)
        l_ref[...] = jnp.zeros_like(l_ref)
        acc_ref[...] = jnp.zeros_like(acc_ref)

    q_last = t0 + i * tq + tq - 1
    q_first = t0 + i * tq
    k_first = j * tk
    k_last = j * tk + tk - 1

    def step(masked):
        v2 = v_ref[...]
        for e in range(2):
            sl = slice(e * LANES, (e + 1) * LANES)
            s = _dot_nt(q_ref[:, sl], k_ref[:, sl])
            if masked:
                row = lax.broadcasted_iota(jnp.int32, s.shape, 0)
                col = lax.broadcasted_iota(jnp.int32, s.shape, 1)
                s = jnp.where(col - row <= q_first - k_first, s, NEG)
            m_old = m_ref[e]
            m_new = jnp.maximum(m_old, jnp.max(s, axis=-1, keepdims=True))
            alpha = jnp.exp2(m_old - m_new)
            p = jnp.exp2(s - m_new)
            l_ref[e] = alpha * l_ref[e] + jnp.sum(p, axis=-1, keepdims=True)
            acc_ref[e] = alpha * acc_ref[e] + _dot(p.astype(BF16), v2)
            m_ref[e] = m_new

    @pl.when(k_last <= q_first)
    def _():
        step(False)

    @pl.when(jnp.logical_and(k_last > q_first, k_first <= q_last))
    def _():
        step(True)

    @pl.when(j == n_k - 1)
    def _():
        even = _even_lanes(o_ref.shape)
        o_ref[...] = jnp.where(even, acc_ref[0] / l_ref[0], acc_ref[1] / l_ref[1]).astype(o_ref.dtype)


def _fox(q_aug, k_aug, v_b, t0):
    b, t, wq = q_aug.shape
    l = k_aug.shape[1]
    n_pairs = wq // (2 * LANES)
    tq = min(FOX_TQ, t)
    tk = min(FOX_TK, l)

    def kv_index(bi, p, i, j):
        return (bi, jnp.minimum(j, (t0 + i * tq + tq - 1) // tk), p)

    return pl.pallas_call(
        functools.partial(_fox_kernel, t0=t0, tq=tq, tk=tk),
        grid=(b, n_pairs, t // tq, l // tk),
        in_specs=[pl.BlockSpec((None, tq, 2 * LANES), lambda bi, p, i, j: (bi, i, p)),
                  pl.BlockSpec((None, tk, 2 * LANES), kv_index),
                  pl.BlockSpec((None, tk, LANES), kv_index)],
        out_specs=pl.BlockSpec((None, tq, LANES), lambda bi, p, i, j: (bi, i, p)),
        out_shape=jax.ShapeDtypeStruct((b, t, n_pairs * LANES), BF16),
        scratch_shapes=[pltpu.VMEM((2, tq, 1), F32), pltpu.VMEM((2, tq, 1), F32),
                        pltpu.VMEM((2, tq, LANES), F32)],
        compiler_params=_cparams("parallel", "parallel", "parallel", "arbitrary"), name="fox",
    )(q_aug, k_aug, v_b)


def _tile2(g):
    return jnp.concatenate([g, g]).reshape(1, LANES).astype(F32)


def _trunk(x, mem_k, mem_v, hg_states, past, prm):
    b, t, d = x.shape
    depth = prm["norm_mix"].shape[0]
    n_a = prm["w_in_a"].shape[0]
    mem_width = mem_k[0].shape[-1]
    main_width = prm["w_in_b"].shape[2] - mem_width
    n_fox = main_width // HEAD_DIM
    t_pad = -(-t // CHUNK) * CHUNK
    h = x
    new_states = []
    new_kv = None
    for l in range(depth):
        g_mix = prm["norm_mix"][l].reshape(1, d)
        if l < n_a:
            widths = (main_width,) * 4 + (mem_width,)
            pq, pf, pi, pg, pm = _proj_split(h.reshape(b * t, d), g_mix, prm["w_in_a"][l], widths)
            chunked = [jnp.pad(a.reshape(b, t, main_width), ((0, 0), (0, t_pad - t), (0, 0)))
                       for a in (pq, pf, pi, pg)]
            o_main, s_new = _hgrn(*chunked, prm["lb_logits"], prm["hg_gnorm"][l], hg_states[l],
                                  layer=l, valid_len=CHUNK if t_pad == t else t % CHUNK)
            o_main = o_main[:, :t]
            new_states.append(s_new)
        else:
            j = l - n_a
            q_aug, pm = _proj_b(h.reshape(b * t, d), g_mix, prm["w_q_b"][j], prm["w_m_b"][j],
                                _tile2(prm["fox_gq"][j]), c3_q.reshape(b * t, LANES), n_fox)
            o_main = _fox(q_aug.reshape(b, t, n_fox * LANES), k_aug, v_b, t0)
        h = _post(h, o_main, pm.reshape(b, t, -1), mem_k[l], mem_v[l], _tile2(prm["mem_gq"][l]),
                  prm["w_out"][l], prm["norm_ffn"][l].reshape(1, d), prm["w_ffn_up"][l],
                  prm["w_ffn_down"][l])
        if l == n_a - 1:
            k_new, v_new, lf_new, lf_pad = _kv_proj(
                h.reshape(b * t, d), prm["norm_kv"].reshape(1, d), prm["w_k"], prm["w_v"],
                _tile2(prm["fox_gk"]), prm["w_f"], prm["b_f"], n_fox)
            k_new = k_new.reshape(b, t, main_width)
            v_new = v_new.reshape(b, t, main_width)
            lf_pad = lf_pad.reshape(b, t, LANES)
            new_kv = (k_new.reshape(b, t, n_fox, HEAD_DIM), v_new.reshape(b, t, n_fox, HEAD_DIM),
                      lf_new.reshape(b, t, n_fox))
            if past is None:
                k_all, v_all, lf_all, t0 = k_new, v_new, lf_pad, 0
            else:
                t0 = past[0].shape[1]
                l_all = t0 + t
                tail = -(-l_all // FOX_TK) * FOX_TK - l_all
                lf_past = jnp.pad(past[2].astype(F32), ((0, 0), (0, 0), (0, LANES - n_fox)))
                k_all = jnp.pad(jnp.concatenate([past[0].reshape(b, t0, main_width), k_new], 1),
                                ((0, 0), (0, tail), (0, 0)))
                v_all = jnp.pad(jnp.concatenate([past[1].reshape(b, t0, main_width), v_new], 1),
                                ((0, 0), (0, tail), (0, 0)))
                lf_all = jnp.pad(jnp.concatenate([lf_past, lf_pad], 1), ((0, 0), (0, tail), (0, 0)))
            k_aug, v_b, c3 = _kv_prep(k_all, v_all, lf_all, n_fox)
            c3_q = c3[:, t0:t0 + t]
    return h, new_states, new_kv


def kernel(x_prompt, x_sample, mem_prompt, state_hgrn_0, state_hgrn_1, cache_fox_k, cache_fox_v, cache_fox_logf, cache_mem_k, cache_mem_v, norm_mix, w_in_a, lb_logits, hg_gnorm, w_in_b, fox_gq, norm_kv, w_kv, b_f, fox_gk, norm_mem, w_mem_kv, mem_gq, mem_gk, w_out, norm_ffn, w_ffn_up, w_ffn_down):
    depth, d = norm_mix.shape
    mem_width = cache_mem_k.shape[-1] * cache_mem_k.shape[-2]
    main_width = w_in_b.shape[2] - mem_width
    n_fox = b_f.shape[0]
    bsz, n_mem, _ = mem_prompt.shape

    w_f = jnp.pad(w_kv[:, 2 * main_width:], ((0, 0), (0, LANES - n_fox)))
    prm = {
        "norm_mix": norm_mix, "w_in_a": w_in_a.astype(BF16), "lb_logits": lb_logits.astype(F32),
        "hg_gnorm": hg_gnorm, "w_in_b": w_in_b,
        "w_q_b": w_in_b[:, :, :main_width].astype(BF16), "w_m_b": w_in_b[:, :, main_width:].astype(BF16),
        "fox_gq": fox_gq, "norm_kv": norm_kv,
        "w_k": w_kv[:, :main_width].astype(BF16), "w_v": w_kv[:, main_width:2 * main_width].astype(BF16),
        "w_f": w_f.astype(BF16), "b_f": jnp.pad(b_f, (0, LANES - n_fox)).reshape(1, LANES).astype(F32),
        "fox_gk": fox_gk, "mem_gq": mem_gq, "w_out": w_out.astype(BF16), "norm_ffn": norm_ffn,
        "w_ffn_up": w_ffn_up.astype(BF16), "w_ffn_down": w_ffn_down.astype(BF16),
    }

    mem_rows = mem_prompt.reshape(bsz * n_mem, d)
    mk, mv, mkb, mvb = [], [], [], []
    for l in range(depth):
        wkv = w_mem_kv[l].astype(BF16)
        k_l, v_l, kb_l, vb_l = _kv_proj(mem_rows, norm_mem[l].reshape(1, d), wkv[:, :mem_width],
                                        wkv[:, mem_width:], _tile2(mem_gk[l]))
        mk.append(k_l)
        mv.append(v_l)
        mkb.append(kb_l.reshape(bsz, n_mem, mem_width))
        mvb.append(vb_l.reshape(bsz, n_mem, mem_width))
    mem_shape = (depth, bsz, n_mem) + cache_mem_k.shape[-2:]
    p_mem_k = jnp.stack(mk).reshape(mem_shape)
    p_mem_v = jnp.stack(mv).reshape(mem_shape)
    s_zero = jnp.zeros((bsz,) + state_hgrn_0.shape[1:], F32)
    y_prompt, p_states, p_kv = _trunk(x_prompt, mkb, mvb, [s_zero] * w_in_a.shape[0], None, prm)

    dec_b = x_sample.shape[0]
    cmk = cache_mem_k.reshape(depth, dec_b, n_mem, mem_width).astype(BF16)
    cmv = cache_mem_v.reshape(depth, dec_b, n_mem, mem_width).astype(BF16)
    y_sample, s_states, s_kv = _trunk(x_sample, cmk, cmv, [state_hgrn_0, state_hgrn_1],
                                      (cache_fox_k, cache_fox_v, cache_fox_logf), prm)
    return (y_prompt, y_sample, p_states[0], p_states[1], p_kv[0], p_kv[1], p_kv[2], p_mem_k, p_mem_v,
            s_states[0], s_states[1], s_kv[0], s_kv[1], s_kv[2])
```

```python
import functools

import numpy as np
import jax
import jax.numpy as jnp
from jax import lax
from jax.experimental import pallas as pl
from jax.experimental.pallas import tpu as pltpu

F32 = jnp.float32
BF16 = jnp.bfloat16

EPS = 1e-6
K_MAX = 0.999999
NEG = -1e30
LOG2E = 1.4426950408889634

LANES = 128
HEAD_DIM = 64
CHUNK = 64
HG_DK = 128
FF_TILE = 256
ROW_TILE = 512
FOX_TQ = 1024
FOX_TK = 1024
N_SPLIT = 3
ONES_LANE = 36
VMEM_LIMIT_BYTES = 56 * 1024 * 1024


def _cparams(*sem):
    return pltpu.CompilerParams(dimension_semantics=sem, vmem_limit_bytes=VMEM_LIMIT_BYTES)


def _resident(shape):
    nd = len(shape)
    return pl.BlockSpec(shape, lambda *_: (0,) * nd, pipeline_mode=pl.Buffered(1))


def _dot(a, b):
    return jnp.dot(a, b, preferred_element_type=F32)


def _dot_nt(a, b):
    return lax.dot_general(a, b, (((1,), (1,)), ((), ())), preferred_element_type=F32)


def _dot_tn(a, b):
    return lax.dot_general(a, b, (((0,), (0,)), ((), ())), preferred_element_type=F32)


def _split_bf16(x, n):
    parts = []
    r = x
    for _ in range(n):
        p = r.astype(BF16)
        parts.append(p)
        r = r - p.astype(F32)
    return parts


def _rms(x, g):
    ms = jnp.mean(x * x, axis=-1, keepdims=True)
    return x * lax.rsqrt(ms + EPS) * g


def _sigmoid(x):
    return 0.5 + 0.5 * jnp.tanh(0.5 * x)


def _silu(x):
    hx = 0.5 * x
    return hx + hx * jnp.tanh(hx)


def _even_lanes(shape):
    return lax.broadcasted_iota(jnp.int32, shape, len(shape) - 1) < HEAD_DIM


def _pair_rms(x2, g2):
    even = _even_lanes(x2.shape)
    sq = x2 * x2
    se = jnp.sum(jnp.where(even, sq, 0.0), axis=-1, keepdims=True)
    so = jnp.sum(jnp.where(even, 0.0, sq), axis=-1, keepdims=True)
    ms = jnp.where(even, se, so) * (1.0 / HEAD_DIM)
    return x2 * lax.rsqrt(ms + EPS) * g2


def _proj_split_kernel(x_ref, g_ref, w_ref, *out_refs, widths):
    xn = _rms(x_ref[...], g_ref[...]).astype(BF16)
    off = 0
    for o_ref, wd in zip(out_refs, widths):
        o_ref[...] = _dot(xn, w_ref[:, off:off + wd]).astype(o_ref.dtype)
        off += wd


def _proj_split(x, g, w, widths):
    n, d = x.shape
    tm = min(ROW_TILE, n)
    return pl.pallas_call(
        functools.partial(_proj_split_kernel, widths=widths),
        grid=(n // tm,),
        in_specs=[pl.BlockSpec((tm, d), lambda i: (i, 0)), _resident((1, d)), _resident(w.shape)],
        out_specs=[pl.BlockSpec((tm, wd), lambda i: (i, 0)) for wd in widths],
        out_shape=[jax.ShapeDtypeStruct((n, wd), F32) for wd in widths],
        compiler_params=_cparams("parallel"), name="proj_a",
    )(x, g, w)


HG_MXU_LEVELS = (2, 4)
HG_ROW_LEVELS = (8, 16, 32)
HG_BLOCK = 256


def _hgrn_consts():
    c = CHUNK
    t = np.arange(c)
    m = []
    for h in HG_MXU_LEVELS:
        blk = t // (2 * h)
        second = (t // h) % 2 == 1
        boundary = blk * 2 * h + h
        mh = np.zeros((c, c), np.float32)
        for r in range(c):
            if second[r]:
                mh[r, boundary[r]:r + 1] = 1.0
            else:
                mh[r, r + 1:boundary[r]] = 1.0
        m.append(mh)
    m.append(np.tril(np.ones((c, c), np.float32)))
    masks = []
    for h in (1,) + HG_MXU_LEVELS + HG_ROW_LEVELS:
        blk = t // (2 * h)
        second = (t // h) % 2 == 1
        masks.append((blk[:, None] == blk[None, :]) & second[:, None] & ~second[None, :])
    masks.append(np.eye(c, dtype=bool))
    return np.concatenate(m, 0), np.stack(masks).astype(np.float32)


def _row_level_exponent(b, h):
    pieces = []
    for r0 in range(0, CHUNK, 2 * h):
        rho = b[r0 + h - 1:r0 + h, :]
        pieces.append(rho - b[r0:r0 + h, :])
        pieces.append(b[r0 + h:r0 + 2 * h, :] - rho)
    return jnp.concatenate(pieces, axis=0)


def _hgrn_kernel(pq_ref, pf_ref, pi_ref, pg_ref, lbl_ref, gn_ref, s0_ref, m_ref, lm_ref,
                 o_ref, sfin_ref, st_ref, *, n_heads, t_valid, layer):
    blk_idx = pl.program_id(1)
    tb = pq_ref.shape[0]
    n_mxu = len(HG_MXU_LEVELS)

    @pl.when(blk_idx == 0)
    def _():
        for h in range(n_heads):
            st_ref[h] = s0_ref[h].T

    rows = [lbl_ref[i:i + 1, :] for i in range(lbl_ref.shape[0])]
    mx = functools.reduce(jnp.maximum, rows)
    es = [jnp.exp(r - mx) for r in rows]
    tot = functools.reduce(lambda a, b: a + b, es)
    ps = [e / tot for e in es]
    cum = ps[0]
    for i in range(1, layer + 1):
        cum = cum + ps[i]
    lb = cum - ps[0]

    m_all = m_ref[...]
    gn = gn_ref[...]
    heads = [slice(h * HG_DK, (h + 1) * HG_DK) for h in range(n_heads)]
    odd_row = lax.broadcasted_iota(jnp.int32, (CHUNK, pq_ref.shape[1]), 0) % 2 == 1
    n_lv = lm_ref.shape[0] - 1
    level_masks = [lm_ref[i] > 0.5 for i in range(n_lv)]
    on_diag = lm_ref[n_lv] > 0.5

    per_chunk = []
    for c in range(tb // CHUNK):
        rs = slice(c * CHUNK, (c + 1) * CHUNK)
        q = _silu(pq_ref[rs, :])
        kk = jnp.minimum((1.0 - lb) * _sigmoid(-pf_ref[rs, :]), K_MAX)
        if t_valid is not None:
            row = blk_idx * tb + c * CHUNK + lax.broadcasted_iota(jnp.int32, kk.shape, 0)
            kk = jnp.where(row < t_valid, kk, 0.0)
        f = 1.0 - kk
        g = jnp.log2(f)
        v_b = pi_ref[rs, :].astype(BF16)

        g_hi, g_lo = _split_bf16(g, 2)
        d_all = _dot(m_all, g_hi) + _dot(m_all, g_lo)
        b = d_all[n_mxu * CHUNK:]
        e_levels = [jnp.where(odd_row, f, 1.0)]
        e_levels += [jnp.exp2(d_all[i * CHUNK:(i + 1) * CHUNK]) for i in range(n_mxu)]
        e_levels += [jnp.exp2(_row_level_exponent(b, h)) for h in HG_ROW_LEVELS]

        q_b = q.astype(BF16)
        k_b = kk.astype(BF16)
        a = [jnp.where(on_diag, _dot_nt(q_b[:, sl], k_b[:, sl]), 0.0) for sl in heads]
        for in_level, el in zip(level_masks, e_levels):
            q_l = (q * el).astype(BF16)
            k_l = (kk * el).astype(BF16)
            a = [jnp.where(in_level, _dot_nt(q_l[:, sl], k_l[:, sl]), a_h) for a_h, sl in zip(a, heads)]

        b_last = b[CHUNK - 1:CHUNK, :]
        per_chunk.append(dict(
            a=[a_h.astype(BF16) for a_h in a], v=v_b, q_e=(q * jnp.exp2(b)).astype(BF16),
            k_e=(kk * jnp.exp2(b_last - b)).astype(BF16), decay=jnp.exp2(b_last)))

    for c, pc in enumerate(per_chunk):
        rs = slice(c * CHUNK, (c + 1) * CHUNK)
        for h, sl in enumerate(heads):
            st = st_ref[h]
            o = _dot_nt(pc["q_e"][:, sl], st.astype(BF16)) + _dot(pc["a"][h], pc["v"][:, sl])
            st_ref[h] = pc["decay"][:, sl] * st + _dot_tn(pc["v"][:, sl], pc["k_e"][:, sl])
            o_ref[rs, sl] = (_rms(o, gn) * _silu(pg_ref[rs, sl])).astype(o_ref.dtype)

    @pl.when(blk_idx == pl.num_programs(1) - 1)
    def _():
        for h in range(n_heads):
            sfin_ref[h] = st_ref[h].T


def _hgrn(pq, pf, pi, pg, lb_logits, gnorm, s0, layer, t_valid):
    b, t, w = pq.shape
    n_heads = w // HG_DK
    tb = HG_BLOCK if t % HG_BLOCK == 0 else CHUNK
    m_np, masks_np = _hgrn_consts()
    m_all = jnp.asarray(m_np, BF16)
    masks = jnp.asarray(masks_np, F32)
    tok = pl.BlockSpec((None, tb, w), lambda i, c: (i, c, 0))
    st_spec = pl.BlockSpec((None, n_heads, HG_DK, HG_DK), lambda i, c: (i, 0, 0, 0))
    return pl.pallas_call(
        functools.partial(_hgrn_kernel, n_heads=n_heads, layer=layer,
                          t_valid=None if t_valid == t else t_valid),
        grid=(b, t // tb),
        in_specs=[tok, tok, tok, tok, _resident(lb_logits.shape), _resident((1, HG_DK)), st_spec,
                  _resident(m_all.shape), _resident(masks.shape)],
        out_specs=[tok, st_spec],
        out_shape=[jax.ShapeDtypeStruct((b, t, w), BF16),
                   jax.ShapeDtypeStruct((b, n_heads, HG_DK, HG_DK), F32)],
        scratch_shapes=[pltpu.VMEM((n_heads, HG_DK, HG_DK), F32)],
        compiler_params=_cparams("parallel", "arbitrary"), name="hgrn",
    )(pq, pf, pi, pg, lb_logits, gnorm.reshape(1, HG_DK), s0, m_all, masks)


def _post_kernel(h_ref, om_ref, pm_ref, mk_ref, mv_ref, gq_ref, wo_ref, gf_ref, wu_ref, wd_ref,
                 out_ref, *, main_width, d_ff):
    x = h_ref[...]
    attn = _dot(om_ref[...], wo_ref[:main_width, :])
    gq2 = gq_ref[...]
    n_pairs = pm_ref.shape[-1] // LANES
    for p in range(n_pairs):
        sl = slice(p * LANES, (p + 1) * LANES)
        qn = _pair_rms(pm_ref[:, sl], gq2) * (HEAD_DIM ** -0.5)
        even = _even_lanes(qn.shape)
        k2 = mk_ref[:, sl]
        v2 = mv_ref[:, sl]
        halves = []
        for own in (even, jnp.logical_not(even)):
            s = _dot_nt(jnp.where(own, qn, 0.0).astype(BF16), k2)
            e = jnp.exp(s - jnp.max(s, axis=-1, keepdims=True))
            l = jnp.sum(e, axis=-1, keepdims=True)
            halves.append(_dot(e.astype(BF16), v2) * (1.0 / l))
        o2 = jnp.where(even, halves[0], halves[1])
        attn = attn + _dot(o2.astype(BF16), wo_ref[main_width + p * LANES:main_width + (p + 1) * LANES, :])
    h1 = x + attn

    xn = _rms(h1, gf_ref[...]).astype(BF16)
    acc = jnp.zeros_like(h1)
    for f in range(d_ff // FF_TILE):
        gate = _dot(xn, wu_ref[:, f * FF_TILE:(f + 1) * FF_TILE])
        up = _dot(xn, wu_ref[:, d_ff + f * FF_TILE:d_ff + (f + 1) * FF_TILE])
        act = (_silu(gate) * up).astype(BF16)
        acc = acc + _dot(act, wd_ref[f * FF_TILE:(f + 1) * FF_TILE, :])
    out_ref[...] = h1 + acc


def _post(h, o_main, pm, mem_k, mem_v, gq2, w_out, g_ffn, w_up, w_down):
    b, t, d = h.shape
    tm = min(ROW_TILE, t)
    main_width = o_main.shape[-1]
    mem_width = pm.shape[-1]
    n_mem = mem_k.shape[1]
    d_ff = w_down.shape[0]
    row = lambda wd: pl.BlockSpec((None, tm, wd), lambda i, r: (i, r, 0))
    mem = pl.BlockSpec((None, n_mem, mem_width), lambda i, r: (i, 0, 0))
    return pl.pallas_call(
        functools.partial(_post_kernel, main_width=main_width, d_ff=d_ff),
        grid=(b, t // tm),
        in_specs=[row(d), row(main_width), row(mem_width), mem, mem, _resident((1, LANES)),
                  _resident(w_out.shape), _resident((1, d)), _resident(w_up.shape),
                  _resident(w_down.shape)],
        out_specs=row(d),
        out_shape=jax.ShapeDtypeStruct((b, t, d), F32),
        compiler_params=_cparams("parallel", "parallel"), name="post",
    )(h, o_main, pm, mem_k, mem_v, gq2, w_out, g_ffn, w_up, w_down)


def _kv_proj_kernel(*refs, has_f, n_f):
    if has_f:
        (x_ref, g_ref, wk_ref, wv_ref, gk_ref, wf_ref, bf_ref,
         k_ref, v_ref, lf_ref, lfp_ref) = refs
    else:
        x_ref, g_ref, wk_ref, wv_ref, gk_ref, k_ref, v_ref, kb_ref, vb_ref = refs
    xn = _rms(x_ref[...], g_ref[...]).astype(BF16)
    gk2 = gk_ref[...]
    for p in range(k_ref.shape[-1] // LANES):
        sl = slice(p * LANES, (p + 1) * LANES)
        k2 = _pair_rms(_dot(xn, wk_ref[:, sl]), gk2)
        k_ref[:, sl] = k2
        if not has_f:
            kb_ref[:, sl] = k2.astype(BF16)
    v = _dot(xn, wv_ref[...])
    v_ref[...] = v
    if has_f:
        y = _dot(xn, wf_ref[...]) + bf_ref[...]
        lf = jnp.minimum(y, 0.0) - jnp.log1p(jnp.exp(-jnp.abs(y)))
        lane = lax.broadcasted_iota(jnp.int32, lf.shape, 1)
        lf = jnp.where(lane < n_f, lf, 0.0)
        lfp_ref[...] = lf
        lf_ref[...] = lf[:, :n_f]
    else:
        vb_ref[...] = v.astype(BF16)


def _kv_proj(x, g, wk, wv, gk2, wf=None, bf=None, n_f=0):
    n, d = x.shape
    tm = min(ROW_TILE, n)
    wk_w = wk.shape[1]
    has_f = wf is not None
    row = lambda wd: pl.BlockSpec((tm, wd), lambda i: (i, 0))
    in_specs = [row(d), _resident((1, d)), _resident(wk.shape), _resident(wv.shape),
                _resident((1, LANES))]
    args = [x, g, wk, wv, gk2]
    if has_f:
        in_specs += [_resident(wf.shape), _resident((1, LANES))]
        args += [wf, bf]
        out_specs = [row(wk_w), row(wk_w), row(n_f), row(LANES)]
        out_shape = [jax.ShapeDtypeStruct((n, wk_w), F32), jax.ShapeDtypeStruct((n, wk_w), F32),
                     jax.ShapeDtypeStruct((n, n_f), F32), jax.ShapeDtypeStruct((n, LANES), F32)]
    else:
        out_specs = [row(wk_w)] * 4
        out_shape = [jax.ShapeDtypeStruct((n, wk_w), F32), jax.ShapeDtypeStruct((n, wk_w), F32),
                     jax.ShapeDtypeStruct((n, wk_w), BF16), jax.ShapeDtypeStruct((n, wk_w), BF16)]
    return pl.pallas_call(
        functools.partial(_kv_proj_kernel, has_f=has_f, n_f=n_f),
        grid=(n // tm,),
        in_specs=in_specs, out_specs=out_specs, out_shape=out_shape,
        compiler_params=_cparams("parallel"), name="kv_proj",
    )(*args)


def _bias_lane_base(h):
    return h * LANES + (HEAD_DIM if h % 2 == 0 else 0)


def _pack_consts(n_heads):
    p = np.zeros((N_SPLIT, LANES, LANES), np.float32)
    for s in range(N_SPLIT):
        for h in range(n_heads):
            p[s, h, s * n_heads + h] = 1.0
    ones_row = np.zeros((1, LANES), np.float32)
    ones_row[0, ONES_LANE] = 1.0
    s_k = np.zeros((LANES, n_heads * LANES), np.float32)
    s_q = np.zeros((LANES, n_heads * LANES), np.float32)
    for h in range(n_heads):
        base = _bias_lane_base(h)
        for s in range(N_SPLIT):
            s_q[s * n_heads + h, base + s] = 1.0
            s_q[ONES_LANE, base + N_SPLIT + s] = 1.0
            s_k[ONES_LANE, base + s] = 1.0
            s_k[s * n_heads + h, base + N_SPLIT + s] = -1.0
    return p, ones_row, s_q, s_k


def _own_half(shape, h):
    even = _even_lanes(shape)
    return even if h % 2 == 0 else jnp.logical_not(even)


def _kv_prep_kernel(k_ref, v_ref, lf_ref, tril_ref, p_ref, ones_ref, sk_ref,
                    kaug_ref, vt_ref, c3_ref, carry_ref, *, n_heads):
    @pl.when(pl.program_id(1) == 0)
    def _():
        carry_ref[...] = jnp.zeros_like(carry_ref)

    tril = tril_ref[...]
    cs = carry_ref[...]
    for part in _split_bf16(lf_ref[...], N_SPLIT):
        cs = cs + _dot(tril, part)
    tl = cs.shape[0]
    carry_ref[...] = cs[tl - 1:tl, :]

    c3 = ones_ref[...]
    for s, part in enumerate(_split_bf16(cs * LOG2E, N_SPLIT)):
        c3 = c3 + _dot(part, p_ref[s])
    c3 = c3.astype(BF16)
    c3_ref[...] = c3
    kbias = _dot(c3, sk_ref[...])
    ones_block = (lax.broadcasted_iota(jnp.int32, (LANES - HEAD_DIM, tl), 0) == 0).astype(F32)
    for p in range(n_heads // 2):
        pair = slice(p * LANES, (p + 1) * LANES)
        k2 = k_ref[:, pair]
        vt2 = v_ref[:, pair].T
        for e in range(2):
            h = 2 * p + e
            slab = slice(h * LANES, (h + 1) * LANES)
            kaug_ref[:, slab] = jnp.where(_own_half(k2.shape, h), k2, kbias[:, slab]).astype(BF16)
            vt_ref[h] = jnp.concatenate([vt2[e * HEAD_DIM:(e + 1) * HEAD_DIM], ones_block], 0).astype(BF16)


def _kv_prep(k, v, lf_pad, n_heads):
    b, l, w = k.shape
    tl = min(FOX_TK, l)
    p_np, ones_np, _, sk_np = _pack_consts(n_heads)
    tril = jnp.asarray(np.tril(np.ones((tl, tl), np.float32)), BF16)
    row = lambda wd: pl.BlockSpec((None, tl, wd), lambda i, j: (i, j, 0))
    return pl.pallas_call(
        functools.partial(_kv_prep_kernel, n_heads=n_heads),
        grid=(b, l // tl),
        in_specs=[row(w), row(w), row(LANES), _resident((tl, tl)), _resident(p_np.shape),
                  _resident((1, LANES)), _resident(sk_np.shape)],
        out_specs=[row(n_heads * LANES),
                   pl.BlockSpec((None, n_heads, None, LANES, tl), lambda i, j: (i, 0, j, 0, 0)),
                   row(LANES)],
        out_shape=[jax.ShapeDtypeStruct((b, l, n_heads * LANES), BF16),
                   jax.ShapeDtypeStruct((b, n_heads, l // tl, LANES, tl), BF16),
                   jax.ShapeDtypeStruct((b, l, LANES), BF16)],
        scratch_shapes=[pltpu.VMEM((1, LANES), F32)],
        compiler_params=_cparams("parallel", "arbitrary"), name="kv_prep",
    )(k, v, lf_pad, tril, jnp.asarray(p_np, BF16), jnp.asarray(ones_np, F32),
      jnp.asarray(sk_np, BF16))


def _proj_b_kernel(x_ref, g_ref, wq_ref, wm_ref, gq_ref, c3_ref, sq_ref, qaug_ref, pm_ref,
                   *, n_heads):
    xn = _rms(x_ref[...], g_ref[...]).astype(BF16)
    pm_ref[...] = _dot(xn, wm_ref[...])
    qbias = _dot(c3_ref[...], sq_ref[...])
    gq2 = gq_ref[...]
    for p in range(n_heads // 2):
        pair = slice(p * LANES, (p + 1) * LANES)
        qn = _pair_rms(_dot(xn, wq_ref[:, pair]), gq2) * (LOG2E * HEAD_DIM ** -0.5)
        for h in (2 * p, 2 * p + 1):
            slab = slice(h * LANES, (h + 1) * LANES)
            qaug_ref[:, slab] = jnp.where(_own_half(qn.shape, h), qn, qbias[:, slab]).astype(BF16)


def _proj_b(x, g, wq, wm, gq2, c3q, n_heads):
    n, d = x.shape
    tm = min(ROW_TILE, n)
    _, _, sq_np, _ = _pack_consts(n_heads)
    row = lambda wd: pl.BlockSpec((tm, wd), lambda i: (i, 0))
    return pl.pallas_call(
        functools.partial(_proj_b_kernel, n_heads=n_heads),
        grid=(n // tm,),
        in_specs=[row(d), _resident((1, d)), _resident(wq.shape), _resident(wm.shape),
                  _resident((1, LANES)), row(LANES), _resident(sq_np.shape)],
        out_specs=[row(n_heads * LANES), row(wm.shape[1])],
        out_shape=[jax.ShapeDtypeStruct((n, n_heads * LANES), BF16),
                   jax.ShapeDtypeStruct((n, wm.shape[1]), F32)],
        compiler_params=_cparams("parallel"), name="proj_b",
    )(x, g, wq, wm, gq2, c3q, jnp.asarray(sq_np, BF16))


def _fox_kernel(q_ref, k_ref, vt_ref, o_ref, acc_ref, *, t0, tq, tk):
    i = pl.program_id(2)
    q_first = t0 + i * tq
    n_full = (q_first + 1) // tk
    n_need = (q_first + tq - 1) // tk + 1
    qs = [q_ref[:, e * LANES:(e + 1) * LANES] for e in range(2)]
    acc_ref[...] = jnp.zeros_like(acc_ref)

    def tile(j, ms, masked):
        row0 = pl.multiple_of(j * tk, tk)
        new_ms = []
        for e in range(2):
            s = _dot_nt(k_ref[pl.ds(row0, tk), e * LANES:(e + 1) * LANES], qs[e])
            if masked:
                key = lax.broadcasted_iota(jnp.int32, s.shape, 0)
                qry = lax.broadcasted_iota(jnp.int32, s.shape, 1)
                s = jnp.where(key - qry <= q_first - j * tk, s, NEG)
            m_new = jnp.maximum(ms[e], jnp.max(s, axis=0, keepdims=True))
            p = jnp.exp2(s - m_new).astype(BF16)
            acc_ref[e] = jnp.exp2(ms[e] - m_new) * acc_ref[e] + _dot(vt_ref[e, j], p)
            new_ms.append(m_new)
        return tuple(new_ms)

    m0 = jnp.full((1, tq), NEG, F32)
    ms = lax.fori_loop(0, n_full, lambda j, ms: tile(j, ms, False), (m0, m0))
    lax.fori_loop(n_full, n_need, lambda j, ms: tile(j, ms, True), ms)

    halves = [acc_ref[e, :HEAD_DIM, :] * (1.0 / acc_ref[e, HEAD_DIM:HEAD_DIM + 1, :]) for e in range(2)]
    o_ref[...] = jnp.concatenate(halves, axis=0).T.astype(o_ref.dtype)


def _fox(q_aug, k_aug, vt, t0):
    b, t, wq = q_aug.shape
    l = k_aug.shape[1]
    n_pairs = wq // (2 * LANES)
    tq = min(FOX_TQ, t)
    tk = vt.shape[-1]
    return pl.pallas_call(
        functools.partial(_fox_kernel, t0=t0, tq=tq, tk=tk),
        grid=(b, n_pairs, t // tq),
        in_specs=[pl.BlockSpec((None, tq, 2 * LANES), lambda bi, p, i: (bi, i, p)),
                  pl.BlockSpec((None, l, 2 * LANES), lambda bi, p, i: (bi, 0, p)),
                  pl.BlockSpec((None, 2, l // tk, LANES, tk), lambda bi, p, i: (bi, p, 0, 0, 0))],
        out_specs=pl.BlockSpec((None, tq, LANES), lambda bi, p, i: (bi, i, p)),
        out_shape=jax.ShapeDtypeStruct((b, t, n_pairs * LANES), BF16),
        scratch_shapes=[pltpu.VMEM((2, LANES, tq), F32)],
        compiler_params=_cparams("parallel", "parallel", "arbitrary"), name="fox",
    )(q_aug, k_aug, vt)


def _tile2(g):
    return jnp.concatenate([g, g]).reshape(1, LANES).astype(F32)


def _trunk(x, mem_k, mem_v, hg_states, past, prm):
    b, t, d = x.shape
    depth = prm["norm_mix"].shape[0]
    n_a = prm["w_in_a"].shape[0]
    mem_width = mem_k[0].shape[-1]
    main_width = prm["w_in_b"].shape[2] - mem_width
    n_fox = main_width // HEAD_DIM
    t_pad = -(-t // CHUNK) * CHUNK
    h = x
    new_states = []
    new_kv = None
    for l in range(depth):
        g_mix = prm["norm_mix"][l].reshape(1, d)
        if l < n_a:
            widths = (main_width,) * 4 + (mem_width,)
            pq, pf, pi, pg, pm = _proj_split(h.reshape(b * t, d), g_mix, prm["w_in_a"][l], widths)
            chunked = [jnp.pad(a.reshape(b, t, main_width), ((0, 0), (0, t_pad - t), (0, 0)))
                       for a in (pq, pf, pi, pg)]
            o_main, s_new = _hgrn(*chunked, prm["lb_logits"], prm["hg_gnorm"][l], hg_states[l],
                                  layer=l, t_valid=t)
            o_main = o_main[:, :t]
            new_states.append(s_new)
        else:
            j = l - n_a
            q_aug, pm = _proj_b(h.reshape(b * t, d), g_mix, prm["w_q_b"][j], prm["w_m_b"][j],
                                _tile2(prm["fox_gq"][j]), c3_q.reshape(b * t, LANES), n_fox)
            o_main = _fox(q_aug.reshape(b, t, n_fox * LANES), k_aug, v_t, t0)
        h = _post(h, o_main, pm.reshape(b, t, -1), mem_k[l], mem_v[l], _tile2(prm["mem_gq"][l]),
                  prm["w_out"][l], prm["norm_ffn"][l].reshape(1, d), prm["w_ffn_up"][l],
                  prm["w_ffn_down"][l])
        if l == n_a - 1:
            k_new, v_new, lf_new, lf_pad = _kv_proj(
                h.reshape(b * t, d), prm["norm_kv"].reshape(1, d), prm["w_k"], prm["w_v"],
                _tile2(prm["fox_gk"]), prm["w_f"], prm["b_f"], n_fox)
            k_new = k_new.reshape(b, t, main_width)
            v_new = v_new.reshape(b, t, main_width)
            lf_pad = lf_pad.reshape(b, t, LANES)
            new_kv = (k_new.reshape(b, t, n_fox, HEAD_DIM), v_new.reshape(b, t, n_fox, HEAD_DIM),
                      lf_new.reshape(b, t, n_fox))
            if past is None:
                k_all, v_all, lf_all, t0 = k_new, v_new, lf_pad, 0
            else:
                t0 = past[0].shape[1]
                l_all = t0 + t
                tail = -(-l_all // FOX_TK) * FOX_TK - l_all
                lf_past = jnp.pad(past[2].astype(F32), ((0, 0), (0, 0), (0, LANES - n_fox)))
                k_all = jnp.pad(jnp.concatenate([past[0].reshape(b, t0, main_width), k_new], 1),
                                ((0, 0), (0, tail), (0, 0)))
                v_all = jnp.pad(jnp.concatenate([past[1].reshape(b, t0, main_width), v_new], 1),
                                ((0, 0), (0, tail), (0, 0)))
                lf_all = jnp.pad(jnp.concatenate([lf_past, lf_pad], 1), ((0, 0), (0, tail), (0, 0)))
            k_aug, v_t, c3 = _kv_prep(k_all, v_all, lf_all, n_fox)
            c3_q = c3[:, t0:t0 + t]
    return h, new_states, new_kv


def kernel(x_prompt, x_sample, mem_prompt, state_hgrn_0, state_hgrn_1, cache_fox_k, cache_fox_v, cache_fox_logf, cache_mem_k, cache_mem_v, norm_mix, w_in_a, lb_logits, hg_gnorm, w_in_b, fox_gq, norm_kv, w_kv, b_f, fox_gk, norm_mem, w_mem_kv, mem_gq, mem_gk, w_out, norm_ffn, w_ffn_up, w_ffn_down):
    depth, d = norm_mix.shape
    mem_width = cache_mem_k.shape[-1] * cache_mem_k.shape[-2]
    main_width = w_in_b.shape[2] - mem_width
    n_fox = b_f.shape[0]
    bsz, n_mem, _ = mem_prompt.shape

    w_f = jnp.pad(w_kv[:, 2 * main_width:], ((0, 0), (0, LANES - n_fox)))
    prm = {
        "norm_mix": norm_mix, "w_in_a": w_in_a.astype(BF16), "lb_logits": lb_logits.astype(F32),
        "hg_gnorm": hg_gnorm, "w_in_b": w_in_b,
        "w_q_b": w_in_b[:, :, :main_width].astype(BF16), "w_m_b": w_in_b[:, :, main_width:].astype(BF16),
        "fox_gq": fox_gq, "norm_kv": norm_kv,
        "w_k": w_kv[:, :main_width].astype(BF16), "w_v": w_kv[:, main_width:2 * main_width].astype(BF16),
        "w_f": w_f.astype(BF16), "b_f": jnp.pad(b_f, (0, LANES - n_fox)).reshape(1, LANES).astype(F32),
        "fox_gk": fox_gk, "mem_gq": mem_gq, "w_out": w_out.astype(BF16), "norm_ffn": norm_ffn,
        "w_ffn_up": w_ffn_up.astype(BF16), "w_ffn_down": w_ffn_down.astype(BF16),
    }

    mem_rows = mem_prompt.reshape(bsz * n_mem, d)
    mk, mv, mkb, mvb = [], [], [], []
    for l in range(depth):
        wkv = w_mem_kv[l].astype(BF16)
        k_l, v_l, kb_l, vb_l = _kv_proj(mem_rows, norm_mem[l].reshape(1, d), wkv[:, :mem_width],
                                        wkv[:, mem_width:], _tile2(mem_gk[l]))
        mk.append(k_l)
        mv.append(v_l)
        mkb.append(kb_l.reshape(bsz, n_mem, mem_width))
        mvb.append(vb_l.reshape(bsz, n_mem, mem_width))
    mem_shape = (depth, bsz, n_mem) + cache_mem_k.shape[-2:]
    p_mem_k = jnp.stack(mk).reshape(mem_shape)
    p_mem_v = jnp.stack(mv).reshape(mem_shape)
    s_zero = jnp.zeros((bsz,) + state_hgrn_0.shape[1:], F32)
    y_prompt, p_states, p_kv = _trunk(x_prompt, mkb, mvb, [s_zero] * w_in_a.shape[0], None, prm)

    dec_b = x_sample.shape[0]
    cmk = cache_mem_k.reshape(depth, dec_b, n_mem, mem_width).astype(BF16)
    cmv = cache_mem_v.reshape(depth, dec_b, n_mem, mem_width).astype(BF16)
    y_sample, s_states, s_kv = _trunk(x_sample, cmk, cmv, [state_hgrn_0, state_hgrn_1],
                                      (cache_fox_k, cache_fox_v, cache_fox_logf), prm)
    return (y_prompt, y_sample, p_states[0], p_states[1], p_kv[0], p_kv[1], p_kv[2], p_mem_k, p_mem_v,
            s_states[0], s_states[1], s_kv[0], s_kv[1], s_kv[2])
```

```python
import functools

import numpy as np
import jax
import jax.numpy as jnp
from jax import lax
from jax.experimental import pallas as pl
from jax.experimental.pallas import tpu as pltpu

F32 = jnp.float32
BF16 = jnp.bfloat16

EPS = 1e-6
K_MAX = 0.999999
NEG = -1e30
LOG2E = 1.4426950408889634

LANES = 128
HEAD_DIM = 64
CHUNK = 64
HG_DK = 128
FF_TILE = 256
ROW_TILE = 512
FOX_TQ = 1024
FOX_TK = 1024
FOX_COLS = 256
FOX_KEYS = 256
FOX_VROWS = HEAD_DIM + 16
N_SPLIT = 3
ONES_LANE = 36
VMEM_LIMIT_BYTES = 56 * 1024 * 1024


def _cparams(*sem):
    return pltpu.CompilerParams(dimension_semantics=sem, vmem_limit_bytes=VMEM_LIMIT_BYTES)


def _resident(shape):
    nd = len(shape)
    return pl.BlockSpec(shape, lambda *_: (0,) * nd, pipeline_mode=pl.Buffered(1))


def _dot(a, b):
    return jnp.dot(a, b, preferred_element_type=F32)


def _dot_nt(a, b):
    return lax.dot_general(a, b, (((1,), (1,)), ((), ())), preferred_element_type=F32)


def _dot_tn(a, b):
    return lax.dot_general(a, b, (((0,), (0,)), ((), ())), preferred_element_type=F32)


def _split_bf16(x, n):
    parts = []
    r = x
    for _ in range(n):
        p = r.astype(BF16)
        parts.append(p)
        r = r - p.astype(F32)
    return parts


def _rms(x, g):
    ms = jnp.mean(x * x, axis=-1, keepdims=True)
    return x * lax.rsqrt(ms + EPS) * g


def _sigmoid(x):
    return 0.5 + 0.5 * jnp.tanh(0.5 * x)


def _silu(x):
    hx = 0.5 * x
    return hx + hx * jnp.tanh(hx)


def _even_lanes(shape):
    return lax.broadcasted_iota(jnp.int32, shape, len(shape) - 1) < HEAD_DIM


def _pair_rms(x2, g2):
    even = _even_lanes(x2.shape)
    sq = x2 * x2
    se = jnp.sum(jnp.where(even, sq, 0.0), axis=-1, keepdims=True)
    so = jnp.sum(jnp.where(even, 0.0, sq), axis=-1, keepdims=True)
    ms = jnp.where(even, se, so) * (1.0 / HEAD_DIM)
    return x2 * lax.rsqrt(ms + EPS) * g2


def _proj_split_kernel(x_ref, g_ref, w_ref, *out_refs, widths):
    xn = _rms(x_ref[...], g_ref[...]).astype(BF16)
    off = 0
    for o_ref, wd in zip(out_refs, widths):
        o_ref[...] = _dot(xn, w_ref[:, off:off + wd]).astype(o_ref.dtype)
        off += wd


def _proj_split(x, g, w, widths):
    n, d = x.shape
    tm = min(ROW_TILE, n)
    return pl.pallas_call(
        functools.partial(_proj_split_kernel, widths=widths),
        grid=(n // tm,),
        in_specs=[pl.BlockSpec((tm, d), lambda i: (i, 0)), _resident((1, d)), _resident(w.shape)],
        out_specs=[pl.BlockSpec((tm, wd), lambda i: (i, 0)) for wd in widths],
        out_shape=[jax.ShapeDtypeStruct((n, wd), F32) for wd in widths],
        compiler_params=_cparams("parallel"), name="proj_a",
    )(x, g, w)


HG_MXU_LEVELS = (2, 4)
HG_ROW_LEVELS = (8, 16, 32)
HG_BLOCK = 256


def _hgrn_consts():
    c = CHUNK
    t = np.arange(c)
    m = []
    for h in HG_MXU_LEVELS:
        blk = t // (2 * h)
        second = (t // h) % 2 == 1
        boundary = blk * 2 * h + h
        mh = np.zeros((c, c), np.float32)
        for r in range(c):
            if second[r]:
                mh[r, boundary[r]:r + 1] = 1.0
            else:
                mh[r, r + 1:boundary[r]] = 1.0
        m.append(mh)
    m.append(np.tril(np.ones((c, c), np.float32)))
    masks = []
    for h in (1,) + HG_MXU_LEVELS + HG_ROW_LEVELS:
        blk = t // (2 * h)
        second = (t // h) % 2 == 1
        masks.append((blk[:, None] == blk[None, :]) & second[:, None] & ~second[None, :])
    masks.append(np.eye(c, dtype=bool))
    return np.concatenate(m, 0), np.stack(masks).astype(np.float32)


def _row_level_exponent(b, h):
    pieces = []
    for r0 in range(0, CHUNK, 2 * h):
        rho = b[r0 + h - 1:r0 + h, :]
        pieces.append(rho - b[r0:r0 + h, :])
        pieces.append(b[r0 + h:r0 + 2 * h, :] - rho)
    return jnp.concatenate(pieces, axis=0)


def _hgrn_kernel(pq_ref, pf_ref, pi_ref, pg_ref, lbl_ref, gn_ref, s0_ref, m_ref, lm_ref,
                 o_ref, sfin_ref, st_ref, *, n_heads, t_valid, layer):
    blk_idx = pl.program_id(1)
    tb = pq_ref.shape[0]
    n_mxu = len(HG_MXU_LEVELS)

    @pl.when(blk_idx == 0)
    def _():
        for h in range(n_heads):
            st_ref[h] = s0_ref[h].T

    rows = [lbl_ref[i:i + 1, :] for i in range(lbl_ref.shape[0])]
    mx = functools.reduce(jnp.maximum, rows)
    es = [jnp.exp(r - mx) for r in rows]
    tot = functools.reduce(lambda a, b: a + b, es)
    ps = [e / tot for e in es]
    cum = ps[0]
    for i in range(1, layer + 1):
        cum = cum + ps[i]
    lb = cum - ps[0]

    m_all = m_ref[...]
    gn = gn_ref[...]
    heads = [slice(h * HG_DK, (h + 1) * HG_DK) for h in range(n_heads)]
    odd_row = lax.broadcasted_iota(jnp.int32, (CHUNK, pq_ref.shape[1]), 0) % 2 == 1
    n_lv = lm_ref.shape[0] - 1
    level_masks = [lm_ref[i] > 0.5 for i in range(n_lv)]
    on_diag = lm_ref[n_lv] > 0.5

    per_chunk = []
    for c in range(tb // CHUNK):
        rs = slice(c * CHUNK, (c + 1) * CHUNK)
        q = _silu(pq_ref[rs, :])
        kk = jnp.minimum((1.0 - lb) * _sigmoid(-pf_ref[rs, :]), K_MAX)
        if t_valid is not None:
            row = blk_idx * tb + c * CHUNK + lax.broadcasted_iota(jnp.int32, kk.shape, 0)
            kk = jnp.where(row < t_valid, kk, 0.0)
        f = 1.0 - kk
        g = jnp.log2(f)
        v_b = pi_ref[rs, :].astype(BF16)

        g_hi, g_lo = _split_bf16(g, 2)
        d_all = _dot(m_all, g_hi) + _dot(m_all, g_lo)
        b = d_all[n_mxu * CHUNK:]
        e_levels = [jnp.where(odd_row, f, 1.0)]
        e_levels += [jnp.exp2(d_all[i * CHUNK:(i + 1) * CHUNK]) for i in range(n_mxu)]
        e_levels += [jnp.exp2(_row_level_exponent(b, h)) for h in HG_ROW_LEVELS]

        q_b = q.astype(BF16)
        k_b = kk.astype(BF16)
        a = [jnp.where(on_diag, _dot_nt(q_b[:, sl], k_b[:, sl]), 0.0) for sl in heads]
        for in_level, el in zip(level_masks, e_levels):
            q_l = (q * el).astype(BF16)
            k_l = (kk * el).astype(BF16)
            a = [jnp.where(in_level, _dot_nt(q_l[:, sl], k_l[:, sl]), a_h) for a_h, sl in zip(a, heads)]

        b_last = b[CHUNK - 1:CHUNK, :]
        per_chunk.append(dict(
            a=[a_h.astype(BF16) for a_h in a], v=v_b, q_e=(q * jnp.exp2(b)).astype(BF16),
            k_e=(kk * jnp.exp2(b_last - b)).astype(BF16), decay=jnp.exp2(b_last)))

    for c, pc in enumerate(per_chunk):
        rs = slice(c * CHUNK, (c + 1) * CHUNK)
        for h, sl in enumerate(heads):
            st = st_ref[h]
            o = _dot_nt(pc["q_e"][:, sl], st.astype(BF16)) + _dot(pc["a"][h], pc["v"][:, sl])
            st_ref[h] = pc["decay"][:, sl] * st + _dot_tn(pc["v"][:, sl], pc["k_e"][:, sl])
            o_ref[rs, sl] = (_rms(o, gn) * _silu(pg_ref[rs, sl])).astype(o_ref.dtype)

    @pl.when(blk_idx == pl.num_programs(1) - 1)
    def _():
        for h in range(n_heads):
            sfin_ref[h] = st_ref[h].T


def _hgrn(pq, pf, pi, pg, lb_logits, gnorm, s0, layer, t_valid):
    b, t, w = pq.shape
    n_heads = w // HG_DK
    tb = HG_BLOCK if t % HG_BLOCK == 0 else CHUNK
    m_np, masks_np = _hgrn_consts()
    m_all = jnp.asarray(m_np, BF16)
    masks = jnp.asarray(masks_np, F32)
    tok = pl.BlockSpec((None, tb, w), lambda i, c: (i, c, 0))
    st_spec = pl.BlockSpec((None, n_heads, HG_DK, HG_DK), lambda i, c: (i, 0, 0, 0))
    return pl.pallas_call(
        functools.partial(_hgrn_kernel, n_heads=n_heads, layer=layer,
                          t_valid=None if t_valid == t else t_valid),
        grid=(b, t // tb),
        in_specs=[tok, tok, tok, tok, _resident(lb_logits.shape), _resident((1, HG_DK)), st_spec,
                  _resident(m_all.shape), _resident(masks.shape)],
        out_specs=[tok, st_spec],
        out_shape=[jax.ShapeDtypeStruct((b, t, w), BF16),
                   jax.ShapeDtypeStruct((b, n_heads, HG_DK, HG_DK), F32)],
        scratch_shapes=[pltpu.VMEM((n_heads, HG_DK, HG_DK), F32)],
        compiler_params=_cparams("parallel", "arbitrary"), name="hgrn",
    )(pq, pf, pi, pg, lb_logits, gnorm.reshape(1, HG_DK), s0, m_all, masks)


def _post_kernel(h_ref, om_ref, pm_ref, mk_ref, mv_ref, gq_ref, wo_ref, gf_ref, wu_ref, wd_ref,
                 out_ref, *, main_width, d_ff):
    x = h_ref[...]
    attn = _dot(om_ref[...], wo_ref[:main_width, :])
    gq2 = gq_ref[...]
    n_pairs = pm_ref.shape[-1] // LANES
    for p in range(n_pairs):
        sl = slice(p * LANES, (p + 1) * LANES)
        qn = _pair_rms(pm_ref[:, sl], gq2) * (HEAD_DIM ** -0.5)
        even = _even_lanes(qn.shape)
        k2 = mk_ref[:, sl]
        v2 = mv_ref[:, sl]
        halves = []
        for own in (even, jnp.logical_not(even)):
            s = _dot_nt(jnp.where(own, qn, 0.0).astype(BF16), k2)
            e = jnp.exp(s - jnp.max(s, axis=-1, keepdims=True))
            l = jnp.sum(e, axis=-1, keepdims=True)
            halves.append(_dot(e.astype(BF16), v2) * (1.0 / l))
        o2 = jnp.where(even, halves[0], halves[1])
        attn = attn + _dot(o2.astype(BF16), wo_ref[main_width + p * LANES:main_width + (p + 1) * LANES, :])
    h1 = x + attn

    xn = _rms(h1, gf_ref[...]).astype(BF16)
    acc = jnp.zeros_like(h1)
    for f in range(d_ff // FF_TILE):
        gate = _dot(xn, wu_ref[:, f * FF_TILE:(f + 1) * FF_TILE])
        up = _dot(xn, wu_ref[:, d_ff + f * FF_TILE:d_ff + (f + 1) * FF_TILE])
        act = (_silu(gate) * up).astype(BF16)
        acc = acc + _dot(act, wd_ref[f * FF_TILE:(f + 1) * FF_TILE, :])
    out_ref[...] = h1 + acc


def _post(h, o_main, pm, mem_k, mem_v, gq2, w_out, g_ffn, w_up, w_down):
    b, t, d = h.shape
    tm = min(ROW_TILE, t)
    main_width = o_main.shape[-1]
    mem_width = pm.shape[-1]
    n_mem = mem_k.shape[1]
    d_ff = w_down.shape[0]
    row = lambda wd: pl.BlockSpec((None, tm, wd), lambda i, r: (i, r, 0))
    mem = pl.BlockSpec((None, n_mem, mem_width), lambda i, r: (i, 0, 0))
    return pl.pallas_call(
        functools.partial(_post_kernel, main_width=main_width, d_ff=d_ff),
        grid=(b, t // tm),
        in_specs=[row(d), row(main_width), row(mem_width), mem, mem, _resident((1, LANES)),
                  _resident(w_out.shape), _resident((1, d)), _resident(w_up.shape),
                  _resident(w_down.shape)],
        out_specs=row(d),
        out_shape=jax.ShapeDtypeStruct((b, t, d), F32),
        compiler_params=_cparams("parallel", "parallel"), name="post",
    )(h, o_main, pm, mem_k, mem_v, gq2, w_out, g_ffn, w_up, w_down)


def _kv_proj_kernel(*refs, has_f, n_f):
    if has_f:
        (x_ref, g_ref, wk_ref, wv_ref, gk_ref, wf_ref, bf_ref,
         k_ref, v_ref, lf_ref, lfp_ref) = refs
    else:
        x_ref, g_ref, wk_ref, wv_ref, gk_ref, k_ref, v_ref, kb_ref, vb_ref = refs
    xn = _rms(x_ref[...], g_ref[...]).astype(BF16)
    gk2 = gk_ref[...]
    for p in range(k_ref.shape[-1] // LANES):
        sl = slice(p * LANES, (p + 1) * LANES)
        k2 = _pair_rms(_dot(xn, wk_ref[:, sl]), gk2)
        k_ref[:, sl] = k2
        if not has_f:
            kb_ref[:, sl] = k2.astype(BF16)
    v = _dot(xn, wv_ref[...])
    v_ref[...] = v
    if has_f:
        y = _dot(xn, wf_ref[...]) + bf_ref[...]
        lf = jnp.minimum(y, 0.0) - jnp.log1p(jnp.exp(-jnp.abs(y)))
        lane = lax.broadcasted_iota(jnp.int32, lf.shape, 1)
        lf = jnp.where(lane < n_f, lf, 0.0)
        lfp_ref[...] = lf
        lf_ref[...] = lf[:, :n_f]
    else:
        vb_ref[...] = v.astype(BF16)


def _kv_proj(x, g, wk, wv, gk2, wf=None, bf=None, n_f=0):
    n, d = x.shape
    tm = min(ROW_TILE, n)
    wk_w = wk.shape[1]
    has_f = wf is not None
    row = lambda wd: pl.BlockSpec((tm, wd), lambda i: (i, 0))
    in_specs = [row(d), _resident((1, d)), _resident(wk.shape), _resident(wv.shape),
                _resident((1, LANES))]
    args = [x, g, wk, wv, gk2]
    if has_f:
        in_specs += [_resident(wf.shape), _resident((1, LANES))]
        args += [wf, bf]
        out_specs = [row(wk_w), row(wk_w), row(n_f), row(LANES)]
        out_shape = [jax.ShapeDtypeStruct((n, wk_w), F32), jax.ShapeDtypeStruct((n, wk_w), F32),
                     jax.ShapeDtypeStruct((n, n_f), F32), jax.ShapeDtypeStruct((n, LANES), F32)]
    else:
        out_specs = [row(wk_w)] * 4
        out_shape = [jax.ShapeDtypeStruct((n, wk_w), F32), jax.ShapeDtypeStruct((n, wk_w), F32),
                     jax.ShapeDtypeStruct((n, wk_w), BF16), jax.ShapeDtypeStruct((n, wk_w), BF16)]
    return pl.pallas_call(
        functools.partial(_kv_proj_kernel, has_f=has_f, n_f=n_f),
        grid=(n // tm,),
        in_specs=in_specs, out_specs=out_specs, out_shape=out_shape,
        compiler_params=_cparams("parallel"), name="kv_proj",
    )(*args)


def _bias_lane_base(h):
    return h * LANES + (HEAD_DIM if h % 2 == 0 else 0)


def _pack_consts(n_heads):
    p = np.zeros((N_SPLIT, LANES, LANES), np.float32)
    for s in range(N_SPLIT):
        for h in range(n_heads):
            p[s, h, s * n_heads + h] = 1.0
    ones_row = np.zeros((1, LANES), np.float32)
    ones_row[0, ONES_LANE] = 1.0
    s_k = np.zeros((LANES, n_heads * LANES), np.float32)
    s_q = np.zeros((LANES, n_heads * LANES), np.float32)
    for h in range(n_heads):
        base = _bias_lane_base(h)
        for s in range(N_SPLIT):
            s_q[s * n_heads + h, base + s] = 1.0
            s_q[ONES_LANE, base + N_SPLIT + s] = 1.0
            s_k[ONES_LANE, base + s] = 1.0
            s_k[s * n_heads + h, base + N_SPLIT + s] = -1.0
    return p, ones_row, s_q, s_k


def _own_half(shape, h):
    even = _even_lanes(shape)
    return even if h % 2 == 0 else jnp.logical_not(even)


def _kv_prep_kernel(k_ref, v_ref, lf_ref, tril_ref, p_ref, ones_ref, sk_ref,
                    kaug_ref, vt_ref, c3_ref, carry_ref, *, n_heads):
    @pl.when(pl.program_id(1) == 0)
    def _():
        carry_ref[...] = jnp.zeros_like(carry_ref)

    tril = tril_ref[...]
    cs = carry_ref[...]
    for part in _split_bf16(lf_ref[...], N_SPLIT):
        cs = cs + _dot(tril, part)
    tl = cs.shape[0]
    carry_ref[...] = cs[tl - 1:tl, :]

    c3 = ones_ref[...]
    for s, part in enumerate(_split_bf16(cs * LOG2E, N_SPLIT)):
        c3 = c3 + _dot(part, p_ref[s])
    c3 = c3.astype(BF16)
    c3_ref[...] = c3
    kbias = _dot(c3, sk_ref[...])
    ones_block = (lax.broadcasted_iota(jnp.int32, (FOX_VROWS - HEAD_DIM, tl), 0) == 0).astype(F32)
    for p in range(n_heads // 2):
        pair = slice(p * LANES, (p + 1) * LANES)
        k2 = k_ref[:, pair]
        vt2 = v_ref[:, pair].T
        for e in range(2):
            h = 2 * p + e
            slab = slice(h * LANES, (h + 1) * LANES)
            kaug_ref[:, slab] = jnp.where(_own_half(k2.shape, h), k2, kbias[:, slab]).astype(BF16)
            vt_ref[h] = jnp.concatenate([vt2[e * HEAD_DIM:(e + 1) * HEAD_DIM], ones_block], 0).astype(BF16)


def _kv_prep(k, v, lf_pad, n_heads):
    b, l, w = k.shape
    tl = min(FOX_TK, l)
    p_np, ones_np, _, sk_np = _pack_consts(n_heads)
    tril = jnp.asarray(np.tril(np.ones((tl, tl), np.float32)), BF16)
    row = lambda wd: pl.BlockSpec((None, tl, wd), lambda i, j: (i, j, 0))
    return pl.pallas_call(
        functools.partial(_kv_prep_kernel, n_heads=n_heads),
        grid=(b, l // tl),
        in_specs=[row(w), row(w), row(LANES), _resident((tl, tl)), _resident(p_np.shape),
                  _resident((1, LANES)), _resident(sk_np.shape)],
        out_specs=[row(n_heads * LANES),
                   pl.BlockSpec((None, n_heads, None, FOX_VROWS, tl), lambda i, j: (i, 0, j, 0, 0)),
                   row(LANES)],
        out_shape=[jax.ShapeDtypeStruct((b, l, n_heads * LANES), BF16),
                   jax.ShapeDtypeStruct((b, n_heads, l // tl, FOX_VROWS, tl), BF16),
                   jax.ShapeDtypeStruct((b, l, LANES), BF16)],
        scratch_shapes=[pltpu.VMEM((1, LANES), F32)],
        compiler_params=_cparams("parallel", "arbitrary"), name="kv_prep",
    )(k, v, lf_pad, tril, jnp.asarray(p_np, BF16), jnp.asarray(ones_np, F32),
      jnp.asarray(sk_np, BF16))


def _proj_b_kernel(x_ref, g_ref, wq_ref, wm_ref, gq_ref, c3_ref, sq_ref, qaug_ref, pm_ref,
                   *, n_heads):
    xn = _rms(x_ref[...], g_ref[...]).astype(BF16)
    pm_ref[...] = _dot(xn, wm_ref[...])
    qbias = _dot(c3_ref[...], sq_ref[...])
    gq2 = gq_ref[...]
    for p in range(n_heads // 2):
        pair = slice(p * LANES, (p + 1) * LANES)
        qn = _pair_rms(_dot(xn, wq_ref[:, pair]), gq2) * (LOG2E * HEAD_DIM ** -0.5)
        for h in (2 * p, 2 * p + 1):
            slab = slice(h * LANES, (h + 1) * LANES)
            qaug_ref[:, slab] = jnp.where(_own_half(qn.shape, h), qn, qbias[:, slab]).astype(BF16)


def _proj_b(x, g, wq, wm, gq2, c3q, n_heads):
    n, d = x.shape
    tm = min(ROW_TILE, n)
    _, _, sq_np, _ = _pack_consts(n_heads)
    row = lambda wd: pl.BlockSpec((tm, wd), lambda i: (i, 0))
    return pl.pallas_call(
        functools.partial(_proj_b_kernel, n_heads=n_heads),
        grid=(n // tm,),
        in_specs=[row(d), _resident((1, d)), _resident(wq.shape), _resident(wm.shape),
                  _resident((1, LANES)), row(LANES), _resident(sq_np.shape)],
        out_specs=[row(n_heads * LANES), row(wm.shape[1])],
        out_shape=[jax.ShapeDtypeStruct((n, n_heads * LANES), BF16),
                   jax.ShapeDtypeStruct((n, wm.shape[1]), F32)],
        compiler_params=_cparams("parallel"), name="proj_b",
    )(x, g, wq, wm, gq2, c3q, jnp.asarray(sq_np, BF16))


def _fox_kernel(q_ref, k_ref, vt_ref, o_ref, acc_ref, s_ref, *, t0, tq, tk):
    i = pl.program_id(2)
    n_full = t0 // tk + i
    cw = min(FOX_COLS, tq)
    units = [(e, c) for e in range(2) for c in range(tq // cw)]
    acc_ref[...] = jnp.zeros_like(acc_ref)

    kc = min(FOX_KEYS, tk)

    def pipelined(ms, scores, values, seen):
        def score_chunk(n, r, m_run):
            e, c = units[n]
            s = scores(units[n], r)
            s_ref[n % 2, r * kc:(r + 1) * kc, :] = s
            return jnp.maximum(m_run, jnp.max(s, axis=0, keepdims=True))

        new_ms = []
        m_next = ms[0]
        for r in range(seen(units[0][1]) // kc):
            m_next = score_chunk(0, r, m_next)
        for n, (e, c) in enumerate(units):
            m_new = m_next
            nxt = n + 1 if n + 1 < len(units) else None
            if nxt is not None:
                m_next = ms[nxt]
            n_cur = seen(c) // kc
            n_nxt = seen(units[nxt][1]) // kc if nxt is not None else 0
            pv = None
            for r in range(max(n_cur, n_nxt)):
                if r < n_nxt:
                    m_next = score_chunk(nxt, r, m_next)
                if r < n_cur:
                    p = jnp.exp2(s_ref[n % 2, r * kc:(r + 1) * kc, :] - m_new).astype(BF16)
                    d = _dot(values(e)[:, r * kc:(r + 1) * kc], p)
                    pv = d if pv is None else pv + d
            cols = slice(c * cw, (c + 1) * cw)
            acc_ref[e, :, cols] = jnp.exp2(ms[n] - m_new) * acc_ref[e, :, cols] + pv
            new_ms.append(m_new)
        return tuple(new_ms)

    def full_tile(j, ms):
        row0 = pl.multiple_of(j * tk, tk)

        def scores(u, r):
            e, c = u
            return _dot_nt(k_ref[pl.ds(row0 + r * kc, kc), e * LANES:(e + 1) * LANES],
                           q_ref[c * cw:(c + 1) * cw, e * LANES:(e + 1) * LANES])

        return pipelined(ms, scores, lambda e: vt_ref[e, j], lambda c: tk)

    def diagonal_tile(ms):
        row0 = pl.multiple_of(n_full * tk, tk)
        seen = lambda c: (c + 1) * cw if cw == FOX_COLS else tk

        def scores(u, r):
            e, c = u
            s = _dot_nt(k_ref[pl.ds(row0 + r * kc, kc), e * LANES:(e + 1) * LANES],
                        q_ref[c * cw:(c + 1) * cw, e * LANES:(e + 1) * LANES])
            key = r * kc + lax.broadcasted_iota(jnp.int32, s.shape, 0)
            qry = c * cw + lax.broadcasted_iota(jnp.int32, s.shape, 1)
            return jnp.where(key <= qry, s, NEG)

        return pipelined(ms, scores, lambda e: vt_ref[e, n_full], seen)

    m0 = jnp.full((1, cw), NEG, F32)
    ms = lax.fori_loop(0, n_full, full_tile, (m0,) * len(units))
    diagonal_tile(ms)

    halves = [acc_ref[e, :HEAD_DIM, :] * (1.0 / acc_ref[e, HEAD_DIM:HEAD_DIM + 1, :]) for e in range(2)]
    o_ref[...] = jnp.concatenate(halves, axis=0).T.astype(o_ref.dtype)


def _fox(q_aug, k_aug, vt, t0):
    b, t, wq = q_aug.shape
    l = k_aug.shape[1]
    n_pairs = wq // (2 * LANES)
    tq = min(FOX_TQ, t)
    tk = vt.shape[-1]
    assert t0 % tk == 0 and (tq == tk or t == tq), (t0, tq, tk, t)
    return pl.pallas_call(
        functools.partial(_fox_kernel, t0=t0, tq=tq, tk=tk),
        grid=(b, n_pairs, t // tq),
        in_specs=[pl.BlockSpec((None, tq, 2 * LANES), lambda bi, p, i: (bi, i, p)),
                  pl.BlockSpec((None, l, 2 * LANES), lambda bi, p, i: (bi, 0, p)),
                  pl.BlockSpec((None, 2, l // tk, FOX_VROWS, tk), lambda bi, p, i: (bi, p, 0, 0, 0))],
        out_specs=pl.BlockSpec((None, tq, LANES), lambda bi, p, i: (bi, i, p)),
        out_shape=jax.ShapeDtypeStruct((b, t, n_pairs * LANES), BF16),
        scratch_shapes=[pltpu.VMEM((2, FOX_VROWS, tq), F32), pltpu.VMEM((2, tk, min(FOX_COLS, tq)), F32)],
        compiler_params=_cparams("parallel", "parallel", "arbitrary"), name="fox",
    )(q_aug, k_aug, vt)


def _tile2(g):
    return jnp.concatenate([g, g]).reshape(1, LANES).astype(F32)


def _trunk(x, mem_k, mem_v, hg_states, past, prm):
    b, t, d = x.shape
    depth = prm["norm_mix"].shape[0]
    n_a = prm["w_in_a"].shape[0]
    mem_width = mem_k[0].shape[-1]
    main_width = prm["w_in_b"].shape[2] - mem_width
    n_fox = main_width // HEAD_DIM
    t_pad = -(-t // CHUNK) * CHUNK
    h = x
    new_states = []
    new_kv = None
    for l in range(depth):
        g_mix = prm["norm_mix"][l].reshape(1, d)
        if l < n_a:
            widths = (main_width,) * 4 + (mem_width,)
            pq, pf, pi, pg, pm = _proj_split(h.reshape(b * t, d), g_mix, prm["w_in_a"][l], widths)
            chunked = [jnp.pad(a.reshape(b, t, main_width), ((0, 0), (0, t_pad - t), (0, 0)))
                       for a in (pq, pf, pi, pg)]
            o_main, s_new = _hgrn(*chunked, prm["lb_logits"], prm["hg_gnorm"][l], hg_states[l],
                                  layer=l, t_valid=t)
            o_main = o_main[:, :t]
            new_states.append(s_new)
        else:
            j = l - n_a
            q_aug, pm = _proj_b(h.reshape(b * t, d), g_mix, prm["w_q_b"][j], prm["w_m_b"][j],
                                _tile2(prm["fox_gq"][j]), c3_q.reshape(b * t, LANES), n_fox)
            o_main = _fox(q_aug.reshape(b, t, n_fox * LANES), k_aug, v_t, t0)
        h = _post(h, o_main, pm.reshape(b, t, -1), mem_k[l], mem_v[l], _tile2(prm["mem_gq"][l]),
                  prm["w_out"][l], prm["norm_ffn"][l].reshape(1, d), prm["w_ffn_up"][l],
                  prm["w_ffn_down"][l])
        if l == n_a - 1:
            k_new, v_new, lf_new, lf_pad = _kv_proj(
                h.reshape(b * t, d), prm["norm_kv"].reshape(1, d), prm["w_k"], prm["w_v"],
                _tile2(prm["fox_gk"]), prm["w_f"], prm["b_f"], n_fox)
            k_new = k_new.reshape(b, t, main_width)
            v_new = v_new.reshape(b, t, main_width)
            lf_pad = lf_pad.reshape(b, t, LANES)
            new_kv = (k_new.reshape(b, t, n_fox, HEAD_DIM), v_new.reshape(b, t, n_fox, HEAD_DIM),
                      lf_new.reshape(b, t, n_fox))
            if past is None:
                k_all, v_all, lf_all, t0 = k_new, v_new, lf_pad, 0
            else:
                t0 = past[0].shape[1]
                l_all = t0 + t
                tail = -(-l_all // FOX_TK) * FOX_TK - l_all
                lf_past = jnp.pad(past[2].astype(F32), ((0, 0), (0, 0), (0, LANES - n_fox)))
                k_all = jnp.pad(jnp.concatenate([past[0].reshape(b, t0, main_width), k_new], 1),
                                ((0, 0), (0, tail), (0, 0)))
                v_all = jnp.pad(jnp.concatenate([past[1].reshape(b, t0, main_width), v_new], 1),
                                ((0, 0), (0, tail), (0, 0)))
                lf_all = jnp.pad(jnp.concatenate([lf_past, lf_pad], 1), ((0, 0), (0, tail), (0, 0)))
            k_aug, v_t, c3 = _kv_prep(k_all, v_all, lf_all, n_fox)
            c3_q = c3[:, t0:t0 + t]
    return h, new_states, new_kv


def kernel(x_prompt, x_sample, mem_prompt, state_hgrn_0, state_hgrn_1, cache_fox_k, cache_fox_v, cache_fox_logf, cache_mem_k, cache_mem_v, norm_mix, w_in_a, lb_logits, hg_gnorm, w_in_b, fox_gq, norm_kv, w_kv, b_f, fox_gk, norm_mem, w_mem_kv, mem_gq, mem_gk, w_out, norm_ffn, w_ffn_up, w_ffn_down):
    depth, d = norm_mix.shape
    mem_width = cache_mem_k.shape[-1] * cache_mem_k.shape[-2]
    main_width = w_in_b.shape[2] - mem_width
    n_fox = b_f.shape[0]
    bsz, n_mem, _ = mem_prompt.shape

    w_f = jnp.pad(w_kv[:, 2 * main_width:], ((0, 0), (0, LANES - n_fox)))
    prm = {
        "norm_mix": norm_mix, "w_in_a": w_in_a.astype(BF16), "lb_logits": lb_logits.astype(F32),
        "hg_gnorm": hg_gnorm, "w_in_b": w_in_b,
        "w_q_b": w_in_b[:, :, :main_width].astype(BF16), "w_m_b": w_in_b[:, :, main_width:].astype(BF16),
        "fox_gq": fox_gq, "norm_kv": norm_kv,
        "w_k": w_kv[:, :main_width].astype(BF16), "w_v": w_kv[:, main_width:2 * main_width].astype(BF16),
        "w_f": w_f.astype(BF16), "b_f": jnp.pad(b_f, (0, LANES - n_fox)).reshape(1, LANES).astype(F32),
        "fox_gk": fox_gk, "mem_gq": mem_gq, "w_out": w_out.astype(BF16), "norm_ffn": norm_ffn,
        "w_ffn_up": w_ffn_up.astype(BF16), "w_ffn_down": w_ffn_down.astype(BF16),
    }

    mem_rows = mem_prompt.reshape(bsz * n_mem, d)
    mk, mv, mkb, mvb = [], [], [], []
    for l in range(depth):
        wkv = w_mem_kv[l].astype(BF16)
        k_l, v_l, kb_l, vb_l = _kv_proj(mem_rows, norm_mem[l].reshape(1, d), wkv[:, :mem_width],
                                        wkv[:, mem_width:], _tile2(mem_gk[l]))
        mk.append(k_l)
        mv.append(v_l)
        mkb.append(kb_l.reshape(bsz, n_mem, mem_width))
        mvb.append(vb_l.reshape(bsz, n_mem, mem_width))
    mem_shape = (depth, bsz, n_mem) + cache_mem_k.shape[-2:]
    p_mem_k = jnp.stack(mk).reshape(mem_shape)
    p_mem_v = jnp.stack(mv).reshape(mem_shape)
    s_zero = jnp.zeros((bsz,) + state_hgrn_0.shape[1:], F32)
    y_prompt, p_states, p_kv = _trunk(x_prompt, mkb, mvb, [s_zero] * w_in_a.shape[0], None, prm)

    dec_b = x_sample.shape[0]
    cmk = cache_mem_k.reshape(depth, dec_b, n_mem, mem_width).astype(BF16)
    cmv = cache_mem_v.reshape(depth, dec_b, n_mem, mem_width).astype(BF16)
    y_sample, s_states, s_kv = _trunk(x_sample, cmk, cmv, [state_hgrn_0, state_hgrn_1],
                                      (cache_fox_k, cache_fox_v, cache_fox_logf), prm)
    return (y_prompt, y_sample, p_states[0], p_states[1], p_kv[0], p_kv[1], p_kv[2], p_mem_k, p_mem_v,
            s_states[0], s_states[1], s_kv[0], s_kv[1], s_kv[2])
```

```python
import functools

import numpy as np
import jax
import jax.numpy as jnp
from jax import lax
from jax.experimental import pallas as pl
from jax.experimental.pallas import tpu as pltpu

F32 = jnp.float32
BF16 = jnp.bfloat16

EPS = 1e-6
K_MAX = 0.999999
NEG = -1e30
LOG2E = 1.4426950408889634

LANES = 128
HEAD_DIM = 64
CHUNK = 64
HG_DK = 128
FF_TILE = 256
ROW_TILE = 512
FOX_TQ = 2048
FOX_TK = 1024
FOX_COLS = 256
FOX_KEYS = 256
FOX_VROWS = HEAD_DIM + 16
N_SPLIT = 3
ONES_LANE = 36
VMEM_LIMIT_BYTES = 56 * 1024 * 1024


def _cparams(*sem):
    return pltpu.CompilerParams(dimension_semantics=sem, vmem_limit_bytes=VMEM_LIMIT_BYTES)


def _resident(shape):
    nd = len(shape)
    return pl.BlockSpec(shape, lambda *_: (0,) * nd, pipeline_mode=pl.Buffered(1))


def _dot(a, b):
    return jnp.dot(a, b, preferred_element_type=F32)


def _dot_nt(a, b):
    return lax.dot_general(a, b, (((1,), (1,)), ((), ())), preferred_element_type=F32)


def _dot_tn(a, b):
    return lax.dot_general(a, b, (((0,), (0,)), ((), ())), preferred_element_type=F32)


def _split_bf16(x, n):
    parts = []
    r = x
    for _ in range(n):
        p = r.astype(BF16)
        parts.append(p)
        r = r - p.astype(F32)
    return parts


def _rms(x, g):
    ms = jnp.mean(x * x, axis=-1, keepdims=True)
    return x * lax.rsqrt(ms + EPS) * g


def _sigmoid(x):
    return 0.5 + 0.5 * jnp.tanh(0.5 * x)


def _silu(x):
    hx = 0.5 * x
    return hx + hx * jnp.tanh(hx)


def _even_lanes(shape):
    return lax.broadcasted_iota(jnp.int32, shape, len(shape) - 1) < HEAD_DIM


def _pair_rms(x2, g2):
    even = _even_lanes(x2.shape)
    sq = x2 * x2
    se = jnp.sum(jnp.where(even, sq, 0.0), axis=-1, keepdims=True)
    so = jnp.sum(jnp.where(even, 0.0, sq), axis=-1, keepdims=True)
    ms = jnp.where(even, se, so) * (1.0 / HEAD_DIM)
    return x2 * lax.rsqrt(ms + EPS) * g2


def _proj_split_kernel(x_ref, g_ref, w_ref, *out_refs, widths):
    xn = _rms(x_ref[...], g_ref[...]).astype(BF16)
    off = 0
    for o_ref, wd in zip(out_refs, widths):
        o_ref[...] = _dot(xn, w_ref[:, off:off + wd]).astype(o_ref.dtype)
        off += wd


def _proj_split(x, g, w, widths):
    n, d = x.shape
    tm = min(ROW_TILE, n)
    return pl.pallas_call(
        functools.partial(_proj_split_kernel, widths=widths),
        grid=(n // tm,),
        in_specs=[pl.BlockSpec((tm, d), lambda i: (i, 0)), _resident((1, d)), _resident(w.shape)],
        out_specs=[pl.BlockSpec((tm, wd), lambda i: (i, 0)) for wd in widths],
        out_shape=[jax.ShapeDtypeStruct((n, wd), F32) for wd in widths],
        compiler_params=_cparams("parallel"), name="proj_a",
    )(x, g, w)


HG_MXU_LEVELS = (2, 4)
HG_ROW_LEVELS = (8, 16, 32)
HG_BLOCK = 256


def _hgrn_consts():
    c = CHUNK
    t = np.arange(c)
    m = []
    for h in HG_MXU_LEVELS:
        blk = t // (2 * h)
        second = (t // h) % 2 == 1
        boundary = blk * 2 * h + h
        mh = np.zeros((c, c), np.float32)
        for r in range(c):
            if second[r]:
                mh[r, boundary[r]:r + 1] = 1.0
            else:
                mh[r, r + 1:boundary[r]] = 1.0
        m.append(mh)
    m.append(np.tril(np.ones((c, c), np.float32)))
    masks = []
    for h in (1,) + HG_MXU_LEVELS + HG_ROW_LEVELS:
        blk = t // (2 * h)
        second = (t // h) % 2 == 1
        masks.append((blk[:, None] == blk[None, :]) & second[:, None] & ~second[None, :])
    masks.append(np.eye(c, dtype=bool))
    return np.concatenate(m, 0), np.stack(masks).astype(np.float32)


def _row_level_exponent(b, h):
    pieces = []
    for r0 in range(0, CHUNK, 2 * h):
        rho = b[r0 + h - 1:r0 + h, :]
        pieces.append(rho - b[r0:r0 + h, :])
        pieces.append(b[r0 + h:r0 + 2 * h, :] - rho)
    return jnp.concatenate(pieces, axis=0)


def _hgrn_kernel(pq_ref, pf_ref, pi_ref, pg_ref, lbl_ref, gn_ref, s0_ref, m_ref, lm_ref,
                 o_ref, sfin_ref, st_ref, *, n_heads, t_valid, layer):
    blk_idx = pl.program_id(1)
    tb = pq_ref.shape[0]
    n_mxu = len(HG_MXU_LEVELS)

    @pl.when(blk_idx == 0)
    def _():
        for h in range(n_heads):
            st_ref[h] = s0_ref[h].T

    rows = [lbl_ref[i:i + 1, :] for i in range(lbl_ref.shape[0])]
    mx = functools.reduce(jnp.maximum, rows)
    es = [jnp.exp(r - mx) for r in rows]
    tot = functools.reduce(lambda a, b: a + b, es)
    ps = [e / tot for e in es]
    cum = ps[0]
    for i in range(1, layer + 1):
        cum = cum + ps[i]
    lb = cum - ps[0]

    m_all = m_ref[...]
    gn = gn_ref[...]
    heads = [slice(h * HG_DK, (h + 1) * HG_DK) for h in range(n_heads)]
    pairs = [slice(p * 2 * HG_DK, (p + 1) * 2 * HG_DK) for p in range(n_heads // 2)]
    odd_row = lax.broadcasted_iota(jnp.int32, (CHUNK, pq_ref.shape[1]), 0) % 2 == 1
    n_lv = lm_ref.shape[0] - 1
    level_masks = [lm_ref[i] > 0.5 for i in range(n_lv)]
    on_diag = lm_ref[n_lv] > 0.5

    def block_rows(x2):
        z = jnp.zeros((x2.shape[0], HG_DK), x2.dtype)
        return jnp.concatenate([jnp.concatenate([x2[:, :HG_DK], z], 1),
                                jnp.concatenate([z, x2[:, HG_DK:]], 1)], 0)

    per_chunk = []
    for c in range(tb // CHUNK):
        rs = slice(c * CHUNK, (c + 1) * CHUNK)
        q = _silu(pq_ref[rs, :])
        kk = jnp.minimum((1.0 - lb) * _sigmoid(-pf_ref[rs, :]), K_MAX)
        if t_valid is not None:
            row = blk_idx * tb + c * CHUNK + lax.broadcasted_iota(jnp.int32, kk.shape, 0)
            kk = jnp.where(row < t_valid, kk, 0.0)
        f = 1.0 - kk
        g = jnp.log2(f)
        v_b = pi_ref[rs, :].astype(BF16)

        g_hi, g_lo = _split_bf16(g, 2)
        d_all = _dot(m_all, g_hi) + _dot(m_all, g_lo)
        b = d_all[n_mxu * CHUNK:]
        e_levels = [jnp.where(odd_row, f, 1.0)]
        e_levels += [jnp.exp2(d_all[i * CHUNK:(i + 1) * CHUNK]) for i in range(n_mxu)]
        e_levels += [jnp.exp2(_row_level_exponent(b, h)) for h in HG_ROW_LEVELS]

        q_b = q.astype(BF16)
        k_b = kk.astype(BF16)
        a = [jnp.where(on_diag, _dot_nt(q_b[:, pr], block_rows(k_b[:, pr])), 0.0) for pr in pairs]
        for in_level, el in zip(level_masks, e_levels):
            el_b = el.astype(BF16)
            q_l = q_b * el_b
            k_l = k_b * el_b
            a = [jnp.where(in_level, _dot_nt(q_l[:, pr], block_rows(k_l[:, pr])), a_p)
                 for a_p, pr in zip(a, pairs)]

        b_last = b[CHUNK - 1:CHUNK, :]
        per_chunk.append(dict(
            a=[a_p.astype(BF16) for a_p in a], v=v_b, q_e=(q * jnp.exp2(b)).astype(BF16),
            k_e=(kk * jnp.exp2(b_last - b)).astype(BF16), decay=jnp.exp2(b_last)))

    for c, pc in enumerate(per_chunk):
        rs = slice(c * CHUNK, (c + 1) * CHUNK)
        for p, pr in enumerate(pairs):
            st2 = jnp.concatenate([st_ref[2 * p], st_ref[2 * p + 1]], 1)
            o2 = (_dot_nt(pc["q_e"][:, pr], block_rows(st2.astype(BF16)))
                  + _dot(pc["a"][p], block_rows(pc["v"][:, pr])))
            for e in range(2):
                h = 2 * p + e
                sl = heads[h]
                st_ref[h] = pc["decay"][:, sl] * st_ref[h] + _dot_tn(pc["v"][:, sl], pc["k_e"][:, sl])
                o = o2[:, e * HG_DK:(e + 1) * HG_DK]
                o_ref[rs, sl] = (_rms(o, gn) * _silu(pg_ref[rs, sl])).astype(o_ref.dtype)

    @pl.when(blk_idx == pl.num_programs(1) - 1)
    def _():
        for h in range(n_heads):
            sfin_ref[h] = st_ref[h].T


def _hgrn(pq, pf, pi, pg, lb_logits, gnorm, s0, layer, t_valid):
    b, t, w = pq.shape
    n_heads = w // HG_DK
    tb = HG_BLOCK if t % HG_BLOCK == 0 else CHUNK
    m_np, masks_np = _hgrn_consts()
    m_all = jnp.asarray(m_np, BF16)
    masks = jnp.asarray(np.tile(masks_np, (1, 1, 2)), F32)
    tok = pl.BlockSpec((None, tb, w), lambda i, c: (i, c, 0))
    st_spec = pl.BlockSpec((None, n_heads, HG_DK, HG_DK), lambda i, c: (i, 0, 0, 0))
    return pl.pallas_call(
        functools.partial(_hgrn_kernel, n_heads=n_heads, layer=layer,
                          t_valid=None if t_valid == t else t_valid),
        grid=(b, t // tb),
        in_specs=[tok, tok, tok, tok, _resident(lb_logits.shape), _resident((1, HG_DK)), st_spec,
                  _resident(m_all.shape), _resident(masks.shape)],
        out_specs=[tok, st_spec],
        out_shape=[jax.ShapeDtypeStruct((b, t, w), BF16),
                   jax.ShapeDtypeStruct((b, n_heads, HG_DK, HG_DK), F32)],
        scratch_shapes=[pltpu.VMEM((n_heads, HG_DK, HG_DK), F32)],
        compiler_params=_cparams("parallel", "arbitrary"), name="hgrn",
    )(pq, pf, pi, pg, lb_logits, gnorm.reshape(1, HG_DK), s0, m_all, masks)


def _post_kernel(h_ref, om_ref, pm_ref, mk_ref, mv_ref, gq_ref, wo_ref, gf_ref, wu_ref, wd_ref,
                 out_ref, *, main_width, d_ff):
    x = h_ref[...]
    attn = _dot(om_ref[...], wo_ref[:main_width, :])
    gq2 = gq_ref[...]
    n_pairs = pm_ref.shape[-1] // LANES
    for p in range(n_pairs):
        sl = slice(p * LANES, (p + 1) * LANES)
        qn = _pair_rms(pm_ref[:, sl], gq2) * (HEAD_DIM ** -0.5)
        even = _even_lanes(qn.shape)
        k2 = mk_ref[:, sl]
        v2 = mv_ref[:, sl]
        halves = []
        for own in (even, jnp.logical_not(even)):
            s = _dot_nt(jnp.where(own, qn, 0.0).astype(BF16), k2)
            e = jnp.exp(s - jnp.max(s, axis=-1, keepdims=True))
            l = jnp.sum(e, axis=-1, keepdims=True)
            halves.append(_dot(e.astype(BF16), v2) * (1.0 / l))
        o2 = jnp.where(even, halves[0], halves[1])
        attn = attn + _dot(o2.astype(BF16), wo_ref[main_width + p * LANES:main_width + (p + 1) * LANES, :])
    h1 = x + attn

    xn = _rms(h1, gf_ref[...]).astype(BF16)
    acc = jnp.zeros_like(h1)
    for f in range(d_ff // FF_TILE):
        gate = _dot(xn, wu_ref[:, f * FF_TILE:(f + 1) * FF_TILE])
        up = _dot(xn, wu_ref[:, d_ff + f * FF_TILE:d_ff + (f + 1) * FF_TILE])
        act = (_silu(gate) * up).astype(BF16)
        acc = acc + _dot(act, wd_ref[f * FF_TILE:(f + 1) * FF_TILE, :])
    out_ref[...] = h1 + acc


def _post(h, o_main, pm, mem_k, mem_v, gq2, w_out, g_ffn, w_up, w_down):
    b, t, d = h.shape
    tm = min(ROW_TILE, t)
    main_width = o_main.shape[-1]
    mem_width = pm.shape[-1]
    n_mem = mem_k.shape[1]
    d_ff = w_down.shape[0]
    row = lambda wd: pl.BlockSpec((None, tm, wd), lambda i, r: (i, r, 0))
    mem = pl.BlockSpec((None, n_mem, mem_width), lambda i, r: (i, 0, 0))
    return pl.pallas_call(
        functools.partial(_post_kernel, main_width=main_width, d_ff=d_ff),
        grid=(b, t // tm),
        in_specs=[row(d), row(main_width), row(mem_width), mem, mem, _resident((1, LANES)),
                  _resident(w_out.shape), _resident((1, d)), _resident(w_up.shape),
                  _resident(w_down.shape)],
        out_specs=row(d),
        out_shape=jax.ShapeDtypeStruct((b, t, d), F32),
        compiler_params=_cparams("parallel", "parallel"), name="post",
    )(h, o_main, pm, mem_k, mem_v, gq2, w_out, g_ffn, w_up, w_down)


def _kv_proj_kernel(*refs, has_f, n_f):
    if has_f:
        (x_ref, g_ref, wk_ref, wv_ref, gk_ref, wf_ref, bf_ref,
         k_ref, v_ref, lf_ref, lfp_ref) = refs
    else:
        x_ref, g_ref, wk_ref, wv_ref, gk_ref, k_ref, v_ref, kb_ref, vb_ref = refs
    xn = _rms(x_ref[...], g_ref[...]).astype(BF16)
    gk2 = gk_ref[...]
    for p in range(k_ref.shape[-1] // LANES):
        sl = slice(p * LANES, (p + 1) * LANES)
        k2 = _pair_rms(_dot(xn, wk_ref[:, sl]), gk2)
        k_ref[:, sl] = k2
        if not has_f:
            kb_ref[:, sl] = k2.astype(BF16)
    v = _dot(xn, wv_ref[...])
    v_ref[...] = v
    if has_f:
        y = _dot(xn, wf_ref[...]) + bf_ref[...]
        lf = jnp.minimum(y, 0.0) - jnp.log1p(jnp.exp(-jnp.abs(y)))
        lane = lax.broadcasted_iota(jnp.int32, lf.shape, 1)
        lf = jnp.where(lane < n_f, lf, 0.0)
        lfp_ref[...] = lf
        lf_ref[...] = lf[:, :n_f]
    else:
        vb_ref[...] = v.astype(BF16)


def _kv_proj(x, g, wk, wv, gk2, wf=None, bf=None, n_f=0):
    n, d = x.shape
    tm = min(ROW_TILE, n)
    wk_w = wk.shape[1]
    has_f = wf is not None
    row = lambda wd: pl.BlockSpec((tm, wd), lambda i: (i, 0))
    in_specs = [row(d), _resident((1, d)), _resident(wk.shape), _resident(wv.shape),
                _resident((1, LANES))]
    args = [x, g, wk, wv, gk2]
    if has_f:
        in_specs += [_resident(wf.shape), _resident((1, LANES))]
        args += [wf, bf]
        out_specs = [row(wk_w), row(wk_w), row(n_f), row(LANES)]
        out_shape = [jax.ShapeDtypeStruct((n, wk_w), F32), jax.ShapeDtypeStruct((n, wk_w), F32),
                     jax.ShapeDtypeStruct((n, n_f), F32), jax.ShapeDtypeStruct((n, LANES), F32)]
    else:
        out_specs = [row(wk_w)] * 4
        out_shape = [jax.ShapeDtypeStruct((n, wk_w), F32), jax.ShapeDtypeStruct((n, wk_w), F32),
                     jax.ShapeDtypeStruct((n, wk_w), BF16), jax.ShapeDtypeStruct((n, wk_w), BF16)]
    return pl.pallas_call(
        functools.partial(_kv_proj_kernel, has_f=has_f, n_f=n_f),
        grid=(n // tm,),
        in_specs=in_specs, out_specs=out_specs, out_shape=out_shape,
        compiler_params=_cparams("parallel"), name="kv_proj",
    )(*args)


def _bias_lane_base(h):
    return h * LANES + (HEAD_DIM if h % 2 == 0 else 0)


def _pack_consts(n_heads):
    p = np.zeros((N_SPLIT, LANES, LANES), np.float32)
    for s in range(N_SPLIT):
        for h in range(n_heads):
            p[s, h, s * n_heads + h] = 1.0
    ones_row = np.zeros((1, LANES), np.float32)
    ones_row[0, ONES_LANE] = 1.0
    s_k = np.zeros((LANES, n_heads * LANES), np.float32)
    s_q = np.zeros((LANES, n_heads * LANES), np.float32)
    for h in range(n_heads):
        base = _bias_lane_base(h)
        for s in range(N_SPLIT):
            s_q[s * n_heads + h, base + s] = 1.0
            s_q[ONES_LANE, base + N_SPLIT + s] = 1.0
            s_k[ONES_LANE, base + s] = 1.0
            s_k[s * n_heads + h, base + N_SPLIT + s] = -1.0
    return p, ones_row, s_q, s_k


def _own_half(shape, h):
    even = _even_lanes(shape)
    return even if h % 2 == 0 else jnp.logical_not(even)


def _kv_prep_kernel(*refs, n_heads, n_main):
    n_src = 3 if n_main is None else 6
    tril_ref, p_ref, ones_ref, sk_ref, kaug_ref, vt_ref, c3_ref, carry_ref = refs[n_src:]
    tl = tril_ref.shape[0]
    j = pl.program_id(1)

    @pl.when(j == 0)
    def _():
        carry_ref[...] = jnp.zeros_like(carry_ref)

    def tile(k_get, v_get, lf):
        tril = tril_ref[...]
        cs = carry_ref[...]
        for part in _split_bf16(lf, N_SPLIT):
            cs = cs + _dot(tril, part)
        carry_ref[...] = cs[tl - 1:tl, :]

        c3 = ones_ref[...]
        for s, part in enumerate(_split_bf16(cs * LOG2E, N_SPLIT)):
            c3 = c3 + _dot(part, p_ref[s])
        c3 = c3.astype(BF16)
        c3_ref[...] = c3
        kbias = _dot(c3, sk_ref[...])
        ones_block = (lax.broadcasted_iota(jnp.int32, (FOX_VROWS - HEAD_DIM, tl), 0) == 0).astype(F32)
        for p in range(n_heads // 2):
            pair = slice(p * LANES, (p + 1) * LANES)
            k2 = k_get(pair)
            vt2 = v_get(pair).T
            for e in range(2):
                h = 2 * p + e
                slab = slice(h * LANES, (h + 1) * LANES)
                kaug_ref[:, slab] = jnp.where(_own_half(k2.shape, h), k2, kbias[:, slab]).astype(BF16)
                vt_ref[h] = jnp.concatenate([vt2[e * HEAD_DIM:(e + 1) * HEAD_DIM], ones_block], 0).astype(BF16)

    def from_refs(k_ref, v_ref, lf_ref):
        def grow(x):
            if x.shape[0] == tl:
                return x
            return jnp.concatenate([x, jnp.zeros((tl - x.shape[0], x.shape[1]), x.dtype)], 0)
        tile(lambda pair: grow(k_ref[:, pair]), lambda pair: grow(v_ref[:, pair]), grow(lf_ref[...]))

    if n_main is None:
        from_refs(*refs[:3])
    else:
        pl.when(j < n_main)(functools.partial(from_refs, *refs[:3]))
        pl.when(j >= n_main)(functools.partial(from_refs, *refs[3:6]))


def _kv_prep(k, v, lf_pad, n_heads, tail=None):
    b, l, w = k.shape
    tl = min(FOX_TK, l)
    n_main = l // tl
    n_tiles = n_main + (0 if tail is None else 1)
    p_np, ones_np, _, sk_np = _pack_consts(n_heads)
    tril = jnp.asarray(np.tril(np.ones((tl, tl), np.float32)), BF16)
    row = lambda wd: pl.BlockSpec((None, tl, wd), lambda i, j: (i, j, 0))
    main = lambda wd: pl.BlockSpec((None, tl, wd), lambda i, j: (i, jnp.minimum(j, n_main - 1), 0))
    srcs = [k, v, lf_pad]
    src_specs = [main(w), main(w), main(LANES)]
    if tail is not None:
        assert l % tl == 0 and tail[0].shape[1] <= tl
        srcs += list(tail)
        src_specs += [pl.BlockSpec((None,) + a.shape[1:], lambda i, j: (i, 0, 0)) for a in tail]
    return pl.pallas_call(
        functools.partial(_kv_prep_kernel, n_heads=n_heads, n_main=None if tail is None else n_main),
        grid=(b, n_tiles),
        in_specs=src_specs + [_resident((tl, tl)), _resident(p_np.shape),
                              _resident((1, LANES)), _resident(sk_np.shape)],
        out_specs=[row(n_heads * LANES),
                   pl.BlockSpec((None, n_heads, None, FOX_VROWS, tl), lambda i, j: (i, 0, j, 0, 0)),
                   row(LANES)],
        out_shape=[jax.ShapeDtypeStruct((b, n_tiles * tl, n_heads * LANES), BF16),
                   jax.ShapeDtypeStruct((b, n_heads, n_tiles, FOX_VROWS, tl), BF16),
                   jax.ShapeDtypeStruct((b, n_tiles * tl, LANES), BF16)],
        scratch_shapes=[pltpu.VMEM((1, LANES), F32)],
        compiler_params=_cparams("parallel", "arbitrary"), name="kv_prep",
    )(*srcs, tril, jnp.asarray(p_np, BF16), jnp.asarray(ones_np, F32), jnp.asarray(sk_np, BF16))


def _proj_b_kernel(x_ref, g_ref, wq_ref, wm_ref, gq_ref, c3_ref, sq_ref, qaug_ref, pm_ref,
                   *, n_heads):
    xn = _rms(x_ref[...], g_ref[...]).astype(BF16)
    pm_ref[...] = _dot(xn, wm_ref[...])
    qbias = _dot(c3_ref[...], sq_ref[...])
    gq2 = gq_ref[...]
    for p in range(n_heads // 2):
        pair = slice(p * LANES, (p + 1) * LANES)
        qn = _pair_rms(_dot(xn, wq_ref[:, pair]), gq2) * (LOG2E * HEAD_DIM ** -0.5)
        for h in (2 * p, 2 * p + 1):
            slab = slice(h * LANES, (h + 1) * LANES)
            qaug_ref[:, slab] = jnp.where(_own_half(qn.shape, h), qn, qbias[:, slab]).astype(BF16)


def _proj_b(x, g, wq, wm, gq2, c3q, n_heads):
    n, d = x.shape
    tm = min(ROW_TILE, n)
    _, _, sq_np, _ = _pack_consts(n_heads)
    row = lambda wd: pl.BlockSpec((tm, wd), lambda i: (i, 0))
    return pl.pallas_call(
        functools.partial(_proj_b_kernel, n_heads=n_heads),
        grid=(n // tm,),
        in_specs=[row(d), _resident((1, d)), _resident(wq.shape), _resident(wm.shape),
                  _resident((1, LANES)), row(LANES), _resident(sq_np.shape)],
        out_specs=[row(n_heads * LANES), row(wm.shape[1])],
        out_shape=[jax.ShapeDtypeStruct((n, n_heads * LANES), BF16),
                   jax.ShapeDtypeStruct((n, wm.shape[1]), F32)],
        compiler_params=_cparams("parallel"), name="proj_b",
    )(x, g, wq, wm, gq2, c3q, jnp.asarray(sq_np, BF16))


def _fox_kernel(q_ref, k_ref, vt_ref, o_ref, acc_ref, s_ref, *, t0, tq, tk, n_q):
    cw = min(FOX_COLS, tq)
    kc = min(FOX_KEYS, tk)
    acc_ref[...] = jnp.zeros_like(acc_ref)

    def attend(qi):
        first_tile = t0 // tk + qi * (tq // tk)
        blocks = [(e, c) for e in range(2) for c in range(tq // cw)]
        items = [(j, e, c, tk) for j in range(first_tile) for e, c in blocks]
        for d in range(max(1, tq // tk)):
            for e, c in blocks:
                seen = min(max((c + 1) * cw - d * tk, 0), tk) if cw == FOX_COLS else tk
                if seen > 0:
                    items.append((first_tile + d, e, c, seen))

        def score_chunk(pos, r, m_run):
            j, e, c, _ = items[pos]
            first_key = j * tk + r * kc
            s = _dot_nt(k_ref[first_key:first_key + kc, e * LANES:(e + 1) * LANES],
                        q_ref[c * cw:(c + 1) * cw, e * LANES:(e + 1) * LANES])
            first_query = t0 + qi * tq + c * cw
            if first_key + kc - 1 > first_query:
                key = first_key + lax.broadcasted_iota(jnp.int32, s.shape, 0)
                qry = first_query + lax.broadcasted_iota(jnp.int32, s.shape, 1)
                s = jnp.where(key <= qry, s, NEG)
            s_ref[pos % 2, r * kc:(r + 1) * kc, :] = s
            return jnp.maximum(m_run, jnp.max(s, axis=0, keepdims=True))

        ms = {blk: jnp.full((1, cw), NEG, F32) for blk in blocks}
        m_next = ms[items[0][1:3]]
        for r in range(items[0][3] // kc):
            m_next = score_chunk(0, r, m_next)
        for pos, (j, e, c, seen) in enumerate(items):
            m_new = m_next
            n_nxt = 0
            if pos + 1 < len(items):
                m_next = ms[items[pos + 1][1:3]] if items[pos + 1][1:3] != (e, c) else m_new
                n_nxt = items[pos + 1][3] // kc
            pv = None
            for r in range(max(seen // kc, n_nxt)):
                if r < n_nxt:
                    m_next = score_chunk(pos + 1, r, m_next)
                if r < seen // kc:
                    p = jnp.exp2(s_ref[pos % 2, r * kc:(r + 1) * kc, :] - m_new).astype(BF16)
                    pv_r = _dot(vt_ref[e, j][:, r * kc:(r + 1) * kc], p)
                    pv = pv_r if pv is None else pv + pv_r
            cols = slice(c * cw, (c + 1) * cw)
            acc_ref[e, :, cols] = jnp.exp2(ms[(e, c)] - m_new) * acc_ref[e, :, cols] + pv
            ms[(e, c)] = m_new

    if n_q == 1:
        attend(0)
    else:
        for qi in range(n_q):
            pl.when(pl.program_id(2) == qi)(functools.partial(attend, qi))

    halves = [acc_ref[e, :HEAD_DIM, :] * (1.0 / acc_ref[e, HEAD_DIM:HEAD_DIM + 1, :]) for e in range(2)]
    o_ref[...] = jnp.concatenate(halves, axis=0).T.astype(o_ref.dtype)


def _fox(q_aug, k_aug, vt, t0):
    b, t, wq = q_aug.shape
    l = k_aug.shape[1]
    n_pairs = wq // (2 * LANES)
    tq = min(FOX_TQ, t)
    tk = vt.shape[-1]
    assert t0 % tk == 0 and (tq % tk == 0 or t == tq), (t0, tq, tk, t)
    return pl.pallas_call(
        functools.partial(_fox_kernel, t0=t0, tq=tq, tk=tk, n_q=t // tq),
        grid=(b, n_pairs, t // tq),
        in_specs=[pl.BlockSpec((None, tq, 2 * LANES), lambda bi, p, i: (bi, i, p)),
                  pl.BlockSpec((None, l, 2 * LANES), lambda bi, p, i: (bi, 0, p)),
                  pl.BlockSpec((None, 2, l // tk, FOX_VROWS, tk), lambda bi, p, i: (bi, p, 0, 0, 0))],
        out_specs=pl.BlockSpec((None, tq, LANES), lambda bi, p, i: (bi, i, p)),
        out_shape=jax.ShapeDtypeStruct((b, t, n_pairs * LANES), BF16),
        scratch_shapes=[pltpu.VMEM((2, FOX_VROWS, tq), F32), pltpu.VMEM((2, tk, min(FOX_COLS, tq)), F32)],
        compiler_params=_cparams("parallel", "parallel", "arbitrary"), name="fox",
    )(q_aug, k_aug, vt)


def _tile2(g):
    return jnp.concatenate([g, g]).reshape(1, LANES).astype(F32)


def _trunk(x, mem_k, mem_v, hg_states, past, prm):
    b, t, d = x.shape
    depth = prm["norm_mix"].shape[0]
    n_a = prm["w_in_a"].shape[0]
    mem_width = mem_k[0].shape[-1]
    main_width = prm["w_in_b"].shape[2] - mem_width
    n_fox = main_width // HEAD_DIM
    t_pad = -(-t // CHUNK) * CHUNK
    h = x
    new_states = []
    new_kv = None
    for l in range(depth):
        g_mix = prm["norm_mix"][l].reshape(1, d)
        if l < n_a:
            widths = (main_width,) * 4 + (mem_width,)
            pq, pf, pi, pg, pm = _proj_split(h.reshape(b * t, d), g_mix, prm["w_in_a"][l], widths)
            chunked = [jnp.pad(a.reshape(b, t, main_width), ((0, 0), (0, t_pad - t), (0, 0)))
                       for a in (pq, pf, pi, pg)]
            o_main, s_new = _hgrn(*chunked, prm["lb_logits"], prm["hg_gnorm"][l], hg_states[l],
                                  layer=l, t_valid=t)
            o_main = o_main[:, :t]
            new_states.append(s_new)
        else:
            j = l - n_a
            q_aug, pm = _proj_b(h.reshape(b * t, d), g_mix, prm["w_q_b"][j], prm["w_m_b"][j],
                                _tile2(prm["fox_gq"][j]), c3_q.reshape(b * t, LANES), n_fox)
            o_main = _fox(q_aug.reshape(b, t, n_fox * LANES), k_aug, v_t, t0)
        h = _post(h, o_main, pm.reshape(b, t, -1), mem_k[l], mem_v[l], _tile2(prm["mem_gq"][l]),
                  prm["w_out"][l], prm["norm_ffn"][l].reshape(1, d), prm["w_ffn_up"][l],
                  prm["w_ffn_down"][l])
        if l == n_a - 1:
            k_new, v_new, lf_new, lf_pad = _kv_proj(
                h.reshape(b * t, d), prm["norm_kv"].reshape(1, d), prm["w_k"], prm["w_v"],
                _tile2(prm["fox_gk"]), prm["w_f"], prm["b_f"], n_fox)
            k_new = k_new.reshape(b, t, main_width)
            v_new = v_new.reshape(b, t, main_width)
            lf_pad = lf_pad.reshape(b, t, LANES)
            new_kv = (k_new.reshape(b, t, n_fox, HEAD_DIM), v_new.reshape(b, t, n_fox, HEAD_DIM),
                      lf_new.reshape(b, t, n_fox))
            if past is None:
                t0 = 0
                k_aug, v_t, c3 = _kv_prep(k_new, v_new, lf_pad, n_fox)
            else:
                t0 = past[0].shape[1]
                lf_past = jnp.pad(past[2].astype(F32), ((0, 0), (0, 0), (0, LANES - n_fox)))
                k_aug, v_t, c3 = _kv_prep(past[0].reshape(b, t0, main_width), past[1].reshape(b, t0, main_width),
                                          lf_past, n_fox, tail=(k_new, v_new, lf_pad))
            c3_q = c3[:, t0:t0 + t]
    return h, new_states, new_kv


def kernel(x_prompt, x_sample, mem_prompt, state_hgrn_0, state_hgrn_1, cache_fox_k, cache_fox_v, cache_fox_logf, cache_mem_k, cache_mem_v, norm_mix, w_in_a, lb_logits, hg_gnorm, w_in_b, fox_gq, norm_kv, w_kv, b_f, fox_gk, norm_mem, w_mem_kv, mem_gq, mem_gk, w_out, norm_ffn, w_ffn_up, w_ffn_down):
    depth, d = norm_mix.shape
    mem_width = cache_mem_k.shape[-1] * cache_mem_k.shape[-2]
    main_width = w_in_b.shape[2] - mem_width
    n_fox = b_f.shape[0]
    bsz, n_mem, _ = mem_prompt.shape

    w_f = jnp.pad(w_kv[:, 2 * main_width:], ((0, 0), (0, LANES - n_fox)))
    prm = {
        "norm_mix": norm_mix, "w_in_a": w_in_a.astype(BF16), "lb_logits": lb_logits.astype(F32),
        "hg_gnorm": hg_gnorm, "w_in_b": w_in_b,
        "w_q_b": w_in_b[:, :, :main_width].astype(BF16), "w_m_b": w_in_b[:, :, main_width:].astype(BF16),
        "fox_gq": fox_gq, "norm_kv": norm_kv,
        "w_k": w_kv[:, :main_width].astype(BF16), "w_v": w_kv[:, main_width:2 * main_width].astype(BF16),
        "w_f": w_f.astype(BF16), "b_f": jnp.pad(b_f, (0, LANES - n_fox)).reshape(1, LANES).astype(F32),
        "fox_gk": fox_gk, "mem_gq": mem_gq, "w_out": w_out.astype(BF16), "norm_ffn": norm_ffn,
        "w_ffn_up": w_ffn_up.astype(BF16), "w_ffn_down": w_ffn_down.astype(BF16),
    }

    mem_rows = mem_prompt.reshape(bsz * n_mem, d)
    mk, mv, mkb, mvb = [], [], [], []
    for l in range(depth):
        wkv = w_mem_kv[l].astype(BF16)
        k_l, v_l, kb_l, vb_l = _kv_proj(mem_rows, norm_mem[l].reshape(1, d), wkv[:, :mem_width],
                                        wkv[:, mem_width:], _tile2(mem_gk[l]))
        mk.append(k_l)
        mv.append(v_l)
        mkb.append(kb_l.reshape(bsz, n_mem, mem_width))
        mvb.append(vb_l.reshape(bsz, n_mem, mem_width))
    mem_shape = (depth, bsz, n_mem) + cache_mem_k.shape[-2:]
    p_mem_k = jnp.stack(mk).reshape(mem_shape)
    p_mem_v = jnp.stack(mv).reshape(mem_shape)
    s_zero = jnp.zeros((bsz,) + state_hgrn_0.shape[1:], F32)
    y_prompt, p_states, p_kv = _trunk(x_prompt, mkb, mvb, [s_zero] * w_in_a.shape[0], None, prm)

    dec_b = x_sample.shape[0]
    cmk = cache_mem_k.reshape(depth, dec_b, n_mem, mem_width).astype(BF16)
    cmv = cache_mem_v.reshape(depth, dec_b, n_mem, mem_width).astype(BF16)
    y_sample, s_states, s_kv = _trunk(x_sample, cmk, cmv, [state_hgrn_0, state_hgrn_1],
                                      (cache_fox_k, cache_fox_v, cache_fox_logf), prm)
    return (y_prompt, y_sample, p_states[0], p_states[1], p_kv[0], p_kv[1], p_kv[2], p_mem_k, p_mem_v,
            s_states[0], s_states[1], s_kv[0], s_kv[1], s_kv[2])
```

```python
import functools

import numpy as np
import jax
import jax.numpy as jnp
from jax import lax
from jax.experimental import pallas as pl
from jax.experimental.pallas import tpu as pltpu

F32 = jnp.float32
BF16 = jnp.bfloat16

EPS = 1e-6
K_MAX = 0.999999
NEG = -1e30
LOG2E = 1.4426950408889634

LANES = 128
HEAD_DIM = 64
CHUNK = 64
HG_DK = 128
FF_TILE = 256
ROW_TILE = 512
FOX_TQ = 2048
FOX_TK = 1024
FOX_COLS = 256
FOX_KEYS = 256
FOX_VROWS = HEAD_DIM + 16
PREP_SUB = 256
N_SPLIT = 3
ONES_LANE = 36
VMEM_LIMIT_BYTES = 56 * 1024 * 1024


def _cparams(*sem):
    return pltpu.CompilerParams(dimension_semantics=sem, vmem_limit_bytes=VMEM_LIMIT_BYTES)


def _resident(shape):
    nd = len(shape)
    return pl.BlockSpec(shape, lambda *_: (0,) * nd, pipeline_mode=pl.Buffered(1))


def _dot(a, b):
    return jnp.dot(a, b, preferred_element_type=F32)


def _dot_nt(a, b):
    return lax.dot_general(a, b, (((1,), (1,)), ((), ())), preferred_element_type=F32)


def _dot_tn(a, b):
    return lax.dot_general(a, b, (((0,), (0,)), ((), ())), preferred_element_type=F32)


def _split_bf16(x, n):
    parts = []
    r = x
    for _ in range(n):
        p = r.astype(BF16)
        parts.append(p)
        r = r - p.astype(F32)
    return parts


def _rms(x, g):
    ms = jnp.mean(x * x, axis=-1, keepdims=True)
    return x * lax.rsqrt(ms + EPS) * g


def _sigmoid(x):
    return 0.5 + 0.5 * jnp.tanh(0.5 * x)


def _silu(x):
    hx = 0.5 * x
    return hx + hx * jnp.tanh(hx)


def _even_lanes(shape):
    return lax.broadcasted_iota(jnp.int32, shape, len(shape) - 1) < HEAD_DIM


def _pair_rms(x2, g2):
    even = _even_lanes(x2.shape)
    sq = x2 * x2
    se = jnp.sum(jnp.where(even, sq, 0.0), axis=-1, keepdims=True)
    so = jnp.sum(jnp.where(even, 0.0, sq), axis=-1, keepdims=True)
    ms = jnp.where(even, se, so) * (1.0 / HEAD_DIM)
    return x2 * lax.rsqrt(ms + EPS) * g2


def _proj_split_kernel(x_ref, g_ref, w_ref, *out_refs, widths):
    xn = _rms(x_ref[...], g_ref[...]).astype(BF16)
    off = 0
    for o_ref, wd in zip(out_refs, widths):
        o_ref[...] = _dot(xn, w_ref[:, off:off + wd]).astype(o_ref.dtype)
        off += wd


def _proj_split(x, g, w, widths):
    n, d = x.shape
    tm = min(ROW_TILE, n)
    return pl.pallas_call(
        functools.partial(_proj_split_kernel, widths=widths),
        grid=(n // tm,),
        in_specs=[pl.BlockSpec((tm, d), lambda i: (i, 0)), _resident((1, d)), _resident(w.shape)],
        out_specs=[pl.BlockSpec((tm, wd), lambda i: (i, 0)) for wd in widths],
        out_shape=[jax.ShapeDtypeStruct((n, wd), F32) for wd in widths],
        compiler_params=_cparams("parallel"), name="proj_a",
    )(x, g, w)


HG_MXU_LEVELS = (2, 4)
HG_ROW_LEVELS = (8, 16, 32)
HG_BLOCK = 256


def _hgrn_consts():
    c = CHUNK
    t = np.arange(c)
    m = []
    for h in HG_MXU_LEVELS:
        blk = t // (2 * h)
        second = (t // h) % 2 == 1
        boundary = blk * 2 * h + h
        mh = np.zeros((c, c), np.float32)
        for r in range(c):
            if second[r]:
                mh[r, boundary[r]:r + 1] = 1.0
            else:
                mh[r, r + 1:boundary[r]] = 1.0
        m.append(mh)
    m.append(np.tril(np.ones((c, c), np.float32)))
    masks = []
    for h in (1,) + HG_MXU_LEVELS + HG_ROW_LEVELS:
        blk = t // (2 * h)
        second = (t // h) % 2 == 1
        masks.append((blk[:, None] == blk[None, :]) & second[:, None] & ~second[None, :])
    masks.append(np.eye(c, dtype=bool))
    return np.concatenate(m, 0), np.stack(masks).astype(np.float32)


def _row_level_exponent(b, h):
    pieces = []
    for r0 in range(0, CHUNK, 2 * h):
        rho = b[r0 + h - 1:r0 + h, :]
        pieces.append(rho - b[r0:r0 + h, :])
        pieces.append(b[r0 + h:r0 + 2 * h, :] - rho)
    return jnp.concatenate(pieces, axis=0)


def _hgrn_kernel(pq_ref, pf_ref, pi_ref, pg_ref, lbl_ref, gn_ref, s0_ref, m_ref, lm_ref,
                 o_ref, sfin_ref, st_ref, *, n_heads, t_valid, layer):
    blk_idx = pl.program_id(1)
    tb = pq_ref.shape[0]
    n_mxu = len(HG_MXU_LEVELS)

    @pl.when(blk_idx == 0)
    def _():
        for h in range(n_heads):
            st_ref[h] = s0_ref[h].T

    rows = [lbl_ref[i:i + 1, :] for i in range(lbl_ref.shape[0])]
    mx = functools.reduce(jnp.maximum, rows)
    es = [jnp.exp(r - mx) for r in rows]
    tot = functools.reduce(lambda a, b: a + b, es)
    ps = [e / tot for e in es]
    cum = ps[0]
    for i in range(1, layer + 1):
        cum = cum + ps[i]
    lb = cum - ps[0]

    m_all = m_ref[...]
    gn = gn_ref[...]
    heads = [slice(h * HG_DK, (h + 1) * HG_DK) for h in range(n_heads)]
    pairs = [slice(p * 2 * HG_DK, (p + 1) * 2 * HG_DK) for p in range(n_heads // 2)]
    odd_row = lax.broadcasted_iota(jnp.int32, (CHUNK, pq_ref.shape[1]), 0) % 2 == 1
    n_lv = lm_ref.shape[0] - 1
    level_masks = [lm_ref[i] > 0.5 for i in range(n_lv)]
    on_diag = lm_ref[n_lv] > 0.5

    def block_rows(x2):
        z = jnp.zeros((x2.shape[0], HG_DK), x2.dtype)
        return jnp.concatenate([jnp.concatenate([x2[:, :HG_DK], z], 1),
                                jnp.concatenate([z, x2[:, HG_DK:]], 1)], 0)

    per_chunk = []
    for c in range(tb // CHUNK):
        rs = slice(c * CHUNK, (c + 1) * CHUNK)
        q = _silu(pq_ref[rs, :])
        kk = jnp.minimum((1.0 - lb) * _sigmoid(-pf_ref[rs, :]), K_MAX)
        if t_valid is not None:
            row = blk_idx * tb + c * CHUNK + lax.broadcasted_iota(jnp.int32, kk.shape, 0)
            kk = jnp.where(row < t_valid, kk, 0.0)
        f = 1.0 - kk
        g = jnp.log2(f)
        v_b = pi_ref[rs, :].astype(BF16)

        g_hi, g_lo = _split_bf16(g, 2)
        d_all = _dot(m_all, g_hi) + _dot(m_all, g_lo)
        b = d_all[n_mxu * CHUNK:]
        e_levels = [jnp.where(odd_row, f, 1.0)]
        e_levels += [jnp.exp2(d_all[i * CHUNK:(i + 1) * CHUNK]) for i in range(n_mxu)]
        e_levels += [jnp.exp2(_row_level_exponent(b, h)) for h in HG_ROW_LEVELS]

        q_b = q.astype(BF16)
        k_b = kk.astype(BF16)
        a = [jnp.where(on_diag, _dot_nt(q_b[:, pr], block_rows(k_b[:, pr])), 0.0) for pr in pairs]
        for in_level, el in zip(level_masks, e_levels):
            el_b = el.astype(BF16)
            q_l = q_b * el_b
            k_l = k_b * el_b
            a = [jnp.where(in_level, _dot_nt(q_l[:, pr], block_rows(k_l[:, pr])), a_p)
                 for a_p, pr in zip(a, pairs)]

        b_last = b[CHUNK - 1:CHUNK, :]
        per_chunk.append(dict(
            a=[a_p.astype(BF16) for a_p in a], v=v_b, q_e=(q * jnp.exp2(b)).astype(BF16),
            k_e=(kk * jnp.exp2(b_last - b)).astype(BF16), decay=jnp.exp2(b_last)))

    for c, pc in enumerate(per_chunk):
        rs = slice(c * CHUNK, (c + 1) * CHUNK)
        for p, pr in enumerate(pairs):
            st2 = jnp.concatenate([st_ref[2 * p], st_ref[2 * p + 1]], 1)
            o2 = (_dot_nt(pc["q_e"][:, pr], block_rows(st2.astype(BF16)))
                  + _dot(pc["a"][p], block_rows(pc["v"][:, pr])))
            for e in range(2):
                h = 2 * p + e
                sl = heads[h]
                st_ref[h] = pc["decay"][:, sl] * st_ref[h] + _dot_tn(pc["v"][:, sl], pc["k_e"][:, sl])
                o = o2[:, e * HG_DK:(e + 1) * HG_DK]
                o_ref[rs, sl] = (_rms(o, gn) * _silu(pg_ref[rs, sl])).astype(o_ref.dtype)

    @pl.when(blk_idx == pl.num_programs(1) - 1)
    def _():
        for h in range(n_heads):
            sfin_ref[h] = st_ref[h].T


def _hgrn(pq, pf, pi, pg, lb_logits, gnorm, s0, layer, t_valid):
    b, t, w = pq.shape
    n_heads = w // HG_DK
    tb = HG_BLOCK if t % HG_BLOCK == 0 else CHUNK
    m_np, masks_np = _hgrn_consts()
    m_all = jnp.asarray(m_np, BF16)
    masks = jnp.asarray(np.tile(masks_np, (1, 1, 2)), F32)
    tok = pl.BlockSpec((None, tb, w), lambda i, c: (i, c, 0))
    st_spec = pl.BlockSpec((None, n_heads, HG_DK, HG_DK), lambda i, c: (i, 0, 0, 0))
    return pl.pallas_call(
        functools.partial(_hgrn_kernel, n_heads=n_heads, layer=layer,
                          t_valid=None if t_valid == t else t_valid),
        grid=(b, t // tb),
        in_specs=[tok, tok, tok, tok, _resident(lb_logits.shape), _resident((1, HG_DK)), st_spec,
                  _resident(m_all.shape), _resident(masks.shape)],
        out_specs=[tok, st_spec],
        out_shape=[jax.ShapeDtypeStruct((b, t, w), BF16),
                   jax.ShapeDtypeStruct((b, n_heads, HG_DK, HG_DK), F32)],
        scratch_shapes=[pltpu.VMEM((n_heads, HG_DK, HG_DK), F32)],
        compiler_params=_cparams("parallel", "arbitrary"), name="hgrn",
    )(pq, pf, pi, pg, lb_logits, gnorm.reshape(1, HG_DK), s0, m_all, masks)


def _post_kernel(h_ref, om_ref, pm_ref, mk_ref, mv_ref, gq_ref, wo_ref, gf_ref, wu_ref, wd_ref,
                 out_ref, *, main_width, d_ff):
    nb, tm, d = h_ref.shape
    x = h_ref[...].reshape(nb * tm, d)
    attn = _dot(om_ref[...].reshape(nb * tm, main_width), wo_ref[:main_width, :])
    gq2 = gq_ref[...]
    n_pairs = pm_ref.shape[-1] // LANES
    for p in range(n_pairs):
        sl = slice(p * LANES, (p + 1) * LANES)
        per_batch = []
        for bi in range(nb):
            qn = _pair_rms(pm_ref[bi, :, sl], gq2) * (HEAD_DIM ** -0.5)
            even = _even_lanes(qn.shape)
            k2 = mk_ref[bi, :, sl]
            v2 = mv_ref[bi, :, sl]
            halves = []
            for own in (even, jnp.logical_not(even)):
                s = _dot_nt(jnp.where(own, qn, 0.0).astype(BF16), k2)
                e = jnp.exp(s - jnp.max(s, axis=-1, keepdims=True))
                l = jnp.sum(e, axis=-1, keepdims=True)
                halves.append(_dot(e.astype(BF16), v2) * (1.0 / l))
            per_batch.append(jnp.where(even, halves[0], halves[1]))
        o2 = per_batch[0] if nb == 1 else jnp.concatenate(per_batch, axis=0)
        attn = attn + _dot(o2.astype(BF16), wo_ref[main_width + p * LANES:main_width + (p + 1) * LANES, :])
    h1 = x + attn

    xn = _rms(h1, gf_ref[...]).astype(BF16)
    acc = jnp.zeros_like(h1)
    for f in range(d_ff // FF_TILE):
        gate = _dot(xn, wu_ref[:, f * FF_TILE:(f + 1) * FF_TILE])
        up = _dot(xn, wu_ref[:, d_ff + f * FF_TILE:d_ff + (f + 1) * FF_TILE])
        act = (_silu(gate) * up).astype(BF16)
        acc = acc + _dot(act, wd_ref[f * FF_TILE:(f + 1) * FF_TILE, :])
    out_ref[...] = (h1 + acc).reshape(nb, tm, d)


def _post(h, o_main, pm, mem_k, mem_v, gq2, w_out, g_ffn, w_up, w_down):
    b, t, d = h.shape
    tm = min(ROW_TILE, t)
    nb = b if b * t <= ROW_TILE else 1
    main_width = o_main.shape[-1]
    mem_width = pm.shape[-1]
    n_mem = mem_k.shape[1]
    d_ff = w_down.shape[0]
    row = lambda wd: pl.BlockSpec((nb, tm, wd), lambda i, r: (i, r, 0))
    mem = pl.BlockSpec((nb, n_mem, mem_width), lambda i, r: (i, 0, 0))
    return pl.pallas_call(
        functools.partial(_post_kernel, main_width=main_width, d_ff=d_ff),
        grid=(b // nb, t // tm),
        in_specs=[row(d), row(main_width), row(mem_width), mem, mem, _resident((1, LANES)),
                  _resident(w_out.shape), _resident((1, d)), _resident(w_up.shape),
                  _resident(w_down.shape)],
        out_specs=row(d),
        out_shape=jax.ShapeDtypeStruct((b, t, d), F32),
        compiler_params=_cparams("parallel", "parallel"), name="post",
    )(h, o_main, pm, mem_k, mem_v, gq2, w_out, g_ffn, w_up, w_down)


def _kv_proj_kernel(*refs, has_f, n_f):
    if has_f:
        (x_ref, g_ref, wk_ref, wv_ref, gk_ref, wf_ref, bf_ref,
         k_ref, v_ref, lf_ref, lfp_ref) = refs
    else:
        x_ref, g_ref, wk_ref, wv_ref, gk_ref, k_ref, v_ref, kb_ref, vb_ref = refs
    xn = _rms(x_ref[...], g_ref[...]).astype(BF16)
    gk2 = gk_ref[...]
    for p in range(k_ref.shape[-1] // LANES):
        sl = slice(p * LANES, (p + 1) * LANES)
        k2 = _pair_rms(_dot(xn, wk_ref[:, sl]), gk2)
        k_ref[:, sl] = k2
        if not has_f:
            kb_ref[:, sl] = k2.astype(BF16)
    v = _dot(xn, wv_ref[...])
    v_ref[...] = v
    if has_f:
        y = _dot(xn, wf_ref[...]) + bf_ref[...]
        lf = jnp.minimum(y, 0.0) - jnp.log1p(jnp.exp(-jnp.abs(y)))
        lane = lax.broadcasted_iota(jnp.int32, lf.shape, 1)
        lf = jnp.where(lane < n_f, lf, 0.0)
        lfp_ref[...] = lf
        lf_ref[...] = lf[:, :n_f]
    else:
        vb_ref[...] = v.astype(BF16)


def _kv_proj(x, g, wk, wv, gk2, wf=None, bf=None, n_f=0):
    n, d = x.shape
    tm = min(ROW_TILE, n)
    wk_w = wk.shape[1]
    has_f = wf is not None
    row = lambda wd: pl.BlockSpec((tm, wd), lambda i: (i, 0))
    in_specs = [row(d), _resident((1, d)), _resident(wk.shape), _resident(wv.shape),
                _resident((1, LANES))]
    args = [x, g, wk, wv, gk2]
    if has_f:
        in_specs += [_resident(wf.shape), _resident((1, LANES))]
        args += [wf, bf]
        out_specs = [row(wk_w), row(wk_w), row(n_f), row(LANES)]
        out_shape = [jax.ShapeDtypeStruct((n, wk_w), F32), jax.ShapeDtypeStruct((n, wk_w), F32),
                     jax.ShapeDtypeStruct((n, n_f), F32), jax.ShapeDtypeStruct((n, LANES), F32)]
    else:
        out_specs = [row(wk_w)] * 4
        out_shape = [jax.ShapeDtypeStruct((n, wk_w), F32), jax.ShapeDtypeStruct((n, wk_w), F32),
                     jax.ShapeDtypeStruct((n, wk_w), BF16), jax.ShapeDtypeStruct((n, wk_w), BF16)]
    return pl.pallas_call(
        functools.partial(_kv_proj_kernel, has_f=has_f, n_f=n_f),
        grid=(n // tm,),
        in_specs=in_specs, out_specs=out_specs, out_shape=out_shape,
        compiler_params=_cparams("parallel"), name="kv_proj",
    )(*args)


def _bias_lane_base(h):
    return h * LANES + (HEAD_DIM if h % 2 == 0 else 0)


def _pack_consts(n_heads):
    p = np.zeros((N_SPLIT, LANES, LANES), np.float32)
    for s in range(N_SPLIT):
        for h in range(n_heads):
            p[s, h, s * n_heads + h] = 1.0
    ones_row = np.zeros((1, LANES), np.float32)
    ones_row[0, ONES_LANE] = 1.0
    s_k = np.zeros((LANES, n_heads * LANES), np.float32)
    s_q = np.zeros((LANES, n_heads * LANES), np.float32)
    for h in range(n_heads):
        base = _bias_lane_base(h)
        for s in range(N_SPLIT):
            s_q[s * n_heads + h, base + s] = 1.0
            s_q[ONES_LANE, base + N_SPLIT + s] = 1.0
            s_k[ONES_LANE, base + s] = 1.0
            s_k[s * n_heads + h, base + N_SPLIT + s] = -1.0
    return p, ones_row, s_q, s_k


def _own_half(shape, h):
    even = _even_lanes(shape)
    return even if h % 2 == 0 else jnp.logical_not(even)


def _kv_prep_kernel(*refs, n_heads, n_main):
    n_src = 3 if n_main is None else 6
    tril_ref, p_ref, ones_ref, sk_ref, kaug_ref, vt_ref, c3_ref, carry_ref = refs[n_src:]
    tl = kaug_ref.shape[0]
    j = pl.program_id(1)

    @pl.when(j == 0)
    def _():
        carry_ref[...] = jnp.zeros_like(carry_ref)

    def tile(k_get, v_get, lf):
        tril = tril_ref[...]
        sub = tril.shape[0]
        parts = _split_bf16(lf, N_SPLIT)
        total = carry_ref[...]
        sums = []
        for r in range(tl // sub):
            cs_r = total
            for part in parts:
                cs_r = cs_r + _dot(tril, part[r * sub:(r + 1) * sub])
            total = cs_r[sub - 1:sub, :]
            sums.append(cs_r)
        cs = jnp.concatenate(sums, axis=0)
        carry_ref[...] = total

        c3 = ones_ref[...]
        for s, part in enumerate(_split_bf16(cs * LOG2E, N_SPLIT)):
            c3 = c3 + _dot(part, p_ref[s])
        c3 = c3.astype(BF16)
        c3_ref[...] = c3
        kbias = _dot(c3, sk_ref[...])
        ones_block = (lax.broadcasted_iota(jnp.int32, (FOX_VROWS - HEAD_DIM, tl), 0) == 0).astype(F32)
        for p in range(n_heads // 2):
            pair = slice(p * LANES, (p + 1) * LANES)
            k2 = k_get(pair)
            vt2 = v_get(pair).T
            for e in range(2):
                h = 2 * p + e
                slab = slice(h * LANES, (h + 1) * LANES)
                kaug_ref[:, slab] = jnp.where(_own_half(k2.shape, h), k2, kbias[:, slab]).astype(BF16)
                vt_ref[h] = jnp.concatenate([vt2[e * HEAD_DIM:(e + 1) * HEAD_DIM], ones_block], 0).astype(BF16)

    def from_refs(k_ref, v_ref, lf_ref):
        def grow(x):
            if x.shape[0] == tl:
                return x
            return jnp.concatenate([x, jnp.zeros((tl - x.shape[0], x.shape[1]), x.dtype)], 0)
        tile(lambda pair: grow(k_ref[:, pair]), lambda pair: grow(v_ref[:, pair]), grow(lf_ref[...]))

    if n_main is None:
        from_refs(*refs[:3])
    else:
        pl.when(j < n_main)(functools.partial(from_refs, *refs[:3]))
        pl.when(j >= n_main)(functools.partial(from_refs, *refs[3:6]))


def _kv_prep(k, v, lf_pad, n_heads, tail=None):
    b, l, w = k.shape
    tl = min(FOX_TK, l)
    n_main = l // tl
    n_tiles = n_main + (0 if tail is None else 1)
    p_np, ones_np, _, sk_np = _pack_consts(n_heads)
    sub = min(PREP_SUB, tl)
    tril = jnp.asarray(np.tril(np.ones((sub, sub), np.float32)), BF16)
    row = lambda wd: pl.BlockSpec((None, tl, wd), lambda i, j: (i, j, 0))
    main = lambda wd: pl.BlockSpec((None, tl, wd), lambda i, j: (i, jnp.minimum(j, n_main - 1), 0))
    srcs = [k, v, lf_pad]
    src_specs = [main(w), main(w), main(LANES)]
    if tail is not None:
        assert l % tl == 0 and tail[0].shape[1] <= tl
        srcs += list(tail)
        src_specs += [pl.BlockSpec((None,) + a.shape[1:], lambda i, j: (i, 0, 0)) for a in tail]
    return pl.pallas_call(
        functools.partial(_kv_prep_kernel, n_heads=n_heads, n_main=None if tail is None else n_main),
        grid=(b, n_tiles),
        in_specs=src_specs + [_resident((sub, sub)), _resident(p_np.shape),
                              _resident((1, LANES)), _resident(sk_np.shape)],
        out_specs=[row(n_heads * LANES),
                   pl.BlockSpec((None, n_heads, None, FOX_VROWS, tl), lambda i, j: (i, 0, j, 0, 0)),
                   row(LANES)],
        out_shape=[jax.ShapeDtypeStruct((b, n_tiles * tl, n_heads * LANES), BF16),
                   jax.ShapeDtypeStruct((b, n_heads, n_tiles, FOX_VROWS, tl), BF16),
                   jax.ShapeDtypeStruct((b, n_tiles * tl, LANES), BF16)],
        scratch_shapes=[pltpu.VMEM((1, LANES), F32)],
        compiler_params=_cparams("parallel", "arbitrary"), name="kv_prep",
    )(*srcs, tril, jnp.asarray(p_np, BF16), jnp.asarray(ones_np, F32), jnp.asarray(sk_np, BF16))


def _proj_b_kernel(x_ref, g_ref, wq_ref, wm_ref, gq_ref, c3_ref, sq_ref, qaug_ref, pm_ref,
                   *, n_heads):
    xn = _rms(x_ref[...], g_ref[...]).astype(BF16)
    pm_ref[...] = _dot(xn, wm_ref[...])
    qbias = _dot(c3_ref[...], sq_ref[...])
    gq2 = gq_ref[...]
    for p in range(n_heads // 2):
        pair = slice(p * LANES, (p + 1) * LANES)
        qn = _pair_rms(_dot(xn, wq_ref[:, pair]), gq2) * (LOG2E * HEAD_DIM ** -0.5)
        for h in (2 * p, 2 * p + 1):
            slab = slice(h * LANES, (h + 1) * LANES)
            qaug_ref[:, slab] = jnp.where(_own_half(qn.shape, h), qn, qbias[:, slab]).astype(BF16)


def _proj_b(x, g, wq, wm, gq2, c3q, n_heads):
    n, d = x.shape
    tm = min(ROW_TILE, n)
    _, _, sq_np, _ = _pack_consts(n_heads)
    row = lambda wd: pl.BlockSpec((tm, wd), lambda i: (i, 0))
    return pl.pallas_call(
        functools.partial(_proj_b_kernel, n_heads=n_heads),
        grid=(n // tm,),
        in_specs=[row(d), _resident((1, d)), _resident(wq.shape), _resident(wm.shape),
                  _resident((1, LANES)), row(LANES), _resident(sq_np.shape)],
        out_specs=[row(n_heads * LANES), row(wm.shape[1])],
        out_shape=[jax.ShapeDtypeStruct((n, n_heads * LANES), BF16),
                   jax.ShapeDtypeStruct((n, wm.shape[1]), F32)],
        compiler_params=_cparams("parallel"), name="proj_b",
    )(x, g, wq, wm, gq2, c3q, jnp.asarray(sq_np, BF16))


def _fox_kernel(q_ref, k_ref, vt_ref, o_ref, acc_ref, s_ref, *, t0, tq, tk, n_q):
    cw = min(FOX_COLS, tq)
    kc = min(FOX_KEYS, tk)
    acc_ref[...] = jnp.zeros_like(acc_ref)

    def attend(qi):
        first_tile = t0 // tk + qi * (tq // tk)
        blocks = [(e, c) for e in range(2) for c in range(tq // cw)]
        items = [(j, e, c, tk) for j in range(first_tile) for e, c in blocks]
        for d in range(max(1, tq // tk)):
            for e, c in blocks:
                seen = min(max((c + 1) * cw - d * tk, 0), tk) if cw == FOX_COLS else tk
                if seen > 0:
                    items.append((first_tile + d, e, c, seen))

        def score_chunk(pos, r, m_run):
            j, e, c, _ = items[pos]
            first_key = j * tk + r * kc
            s = _dot_nt(k_ref[first_key:first_key + kc, e * LANES:(e + 1) * LANES],
                        q_ref[c * cw:(c + 1) * cw, e * LANES:(e + 1) * LANES])
            first_query = t0 + qi * tq + c * cw
            if first_key + kc - 1 > first_query:
                key = first_key + lax.broadcasted_iota(jnp.int32, s.shape, 0)
                qry = first_query + lax.broadcasted_iota(jnp.int32, s.shape, 1)
                s = jnp.where(key <= qry, s, NEG)
            s_ref[pos % 2, r * kc:(r + 1) * kc, :] = s
            return jnp.maximum(m_run, jnp.max(s, axis=0, keepdims=True))

        ms = {blk: jnp.full((1, cw), NEG, F32) for blk in blocks}
        m_next = ms[items[0][1:3]]
        for r in range(items[0][3] // kc):
            m_next = score_chunk(0, r, m_next)
        for pos, (j, e, c, seen) in enumerate(items):
            m_new = m_next
            n_nxt = 0
            if pos + 1 < len(items):
                m_next = ms[items[pos + 1][1:3]] if items[pos + 1][1:3] != (e, c) else m_new
                n_nxt = items[pos + 1][3] // kc
            pv = None
            for r in range(max(seen // kc, n_nxt)):
                if r < n_nxt:
                    m_next = score_chunk(pos + 1, r, m_next)
                if r < seen // kc:
                    p = jnp.exp2(s_ref[pos % 2, r * kc:(r + 1) * kc, :] - m_new).astype(BF16)
                    pv_r = _dot(vt_ref[e, j][:, r * kc:(r + 1) * kc], p)
                    pv = pv_r if pv is None else pv + pv_r
            cols = slice(c * cw, (c + 1) * cw)
            acc_ref[e, :, cols] = jnp.exp2(ms[(e, c)] - m_new) * acc_ref[e, :, cols] + pv
            ms[(e, c)] = m_new

    if n_q == 1:
        attend(0)
    else:
        for qi in range(n_q):
            pl.when(pl.program_id(2) == qi)(functools.partial(attend, qi))

    halves = [acc_ref[e, :HEAD_DIM, :] * (1.0 / acc_ref[e, HEAD_DIM:HEAD_DIM + 1, :]) for e in range(2)]
    o_ref[...] = jnp.concatenate(halves, axis=0).T.astype(o_ref.dtype)


def _fox(q_aug, k_aug, vt, t0):
    b, t, wq = q_aug.shape
    l = k_aug.shape[1]
    n_pairs = wq // (2 * LANES)
    tq = min(FOX_TQ, t)
    tk = vt.shape[-1]
    assert t0 % tk == 0 and (tq % tk == 0 or t == tq), (t0, tq, tk, t)
    return pl.pallas_call(
        functools.partial(_fox_kernel, t0=t0, tq=tq, tk=tk, n_q=t // tq),
        grid=(b, n_pairs, t // tq),
        in_specs=[pl.BlockSpec((None, tq, 2 * LANES), lambda bi, p, i: (bi, i, p)),
                  pl.BlockSpec((None, l, 2 * LANES), lambda bi, p, i: (bi, 0, p)),
                  pl.BlockSpec((None, 2, l // tk, FOX_VROWS, tk), lambda bi, p, i: (bi, p, 0, 0, 0))],
        out_specs=pl.BlockSpec((None, tq, LANES), lambda bi, p, i: (bi, i, p)),
        out_shape=jax.ShapeDtypeStruct((b, t, n_pairs * LANES), BF16),
        scratch_shapes=[pltpu.VMEM((2, FOX_VROWS, tq), F32), pltpu.VMEM((2, tk, min(FOX_COLS, tq)), F32)],
        compiler_params=_cparams("parallel", "parallel", "arbitrary"), name="fox",
    )(q_aug, k_aug, vt)


def _tile2(g):
    return jnp.concatenate([g, g]).reshape(1, LANES).astype(F32)


def _trunk(x, mem_k, mem_v, hg_states, past, prm):
    b, t, d = x.shape
    depth = prm["norm_mix"].shape[0]
    n_a = prm["w_in_a"].shape[0]
    mem_width = mem_k[0].shape[-1]
    main_width = prm["w_in_b"].shape[2] - mem_width
    n_fox = main_width // HEAD_DIM
    t_pad = -(-t // CHUNK) * CHUNK
    h = x
    new_states = []
    new_kv = None
    for l in range(depth):
        g_mix = prm["norm_mix"][l].reshape(1, d)
        if l < n_a:
            widths = (main_width,) * 4 + (mem_width,)
            pq, pf, pi, pg, pm = _proj_split(h.reshape(b * t, d), g_mix, prm["w_in_a"][l], widths)
            chunked = [jnp.pad(a.reshape(b, t, main_width), ((0, 0), (0, t_pad - t), (0, 0)))
                       for a in (pq, pf, pi, pg)]
            o_main, s_new = _hgrn(*chunked, prm["lb_logits"], prm["hg_gnorm"][l], hg_states[l],
                                  layer=l, t_valid=t)
            o_main = o_main[:, :t]
            new_states.append(s_new)
        else:
            j = l - n_a
            q_aug, pm = _proj_b(h.reshape(b * t, d), g_mix, prm["w_q_b"][j], prm["w_m_b"][j],
                                _tile2(prm["fox_gq"][j]), c3_q.reshape(b * t, LANES), n_fox)
            o_main = _fox(q_aug.reshape(b, t, n_fox * LANES), k_aug, v_t, t0)
        h = _post(h, o_main, pm.reshape(b, t, -1), mem_k[l], mem_v[l], _tile2(prm["mem_gq"][l]),
                  prm["w_out"][l], prm["norm_ffn"][l].reshape(1, d), prm["w_ffn_up"][l],
                  prm["w_ffn_down"][l])
        if l == n_a - 1:
            k_new, v_new, lf_new, lf_pad = _kv_proj(
                h.reshape(b * t, d), prm["norm_kv"].reshape(1, d), prm["w_k"], prm["w_v"],
                _tile2(prm["fox_gk"]), prm["w_f"], prm["b_f"], n_fox)
            k_new = k_new.reshape(b, t, main_width)
            v_new = v_new.reshape(b, t, main_width)
            lf_pad = lf_pad.reshape(b, t, LANES)
            new_kv = (k_new.reshape(b, t, n_fox, HEAD_DIM), v_new.reshape(b, t, n_fox, HEAD_DIM),
                      lf_new.reshape(b, t, n_fox))
            if past is None:
                t0 = 0
                k_aug, v_t, c3 = _kv_prep(k_new, v_new, lf_pad, n_fox)
            else:
                t0 = past[0].shape[1]
                lf_past = jnp.pad(past[2].astype(F32), ((0, 0), (0, 0), (0, LANES - n_fox)))
                k_aug, v_t, c3 = _kv_prep(past[0].reshape(b, t0, main_width), past[1].reshape(b, t0, main_width),
                                          lf_past, n_fox, tail=(k_new, v_new, lf_pad))
            c3_q = c3[:, t0:t0 + t]
    return h, new_states, new_kv


def kernel(x_prompt, x_sample, mem_prompt, state_hgrn_0, state_hgrn_1, cache_fox_k, cache_fox_v, cache_fox_logf, cache_mem_k, cache_mem_v, norm_mix, w_in_a, lb_logits, hg_gnorm, w_in_b, fox_gq, norm_kv, w_kv, b_f, fox_gk, norm_mem, w_mem_kv, mem_gq, mem_gk, w_out, norm_ffn, w_ffn_up, w_ffn_down):
    depth, d = norm_mix.shape
    mem_width = cache_mem_k.shape[-1] * cache_mem_k.shape[-2]
    main_width = w_in_b.shape[2] - mem_width
    n_fox = b_f.shape[0]
    bsz, n_mem, _ = mem_prompt.shape

    w_f = jnp.pad(w_kv[:, 2 * main_width:], ((0, 0), (0, LANES - n_fox)))
    prm = {
        "norm_mix": norm_mix, "w_in_a": w_in_a.astype(BF16), "lb_logits": lb_logits.astype(F32),
        "hg_gnorm": hg_gnorm, "w_in_b": w_in_b,
        "w_q_b": w_in_b[:, :, :main_width].astype(BF16), "w_m_b": w_in_b[:, :, main_width:].astype(BF16),
        "fox_gq": fox_gq, "norm_kv": norm_kv,
        "w_k": w_kv[:, :main_width].astype(BF16), "w_v": w_kv[:, main_width:2 * main_width].astype(BF16),
        "w_f": w_f.astype(BF16), "b_f": jnp.pad(b_f, (0, LANES - n_fox)).reshape(1, LANES).astype(F32),
        "fox_gk": fox_gk, "mem_gq": mem_gq, "w_out": w_out.astype(BF16), "norm_ffn": norm_ffn,
        "w_ffn_up": w_ffn_up.astype(BF16), "w_ffn_down": w_ffn_down.astype(BF16),
    }

    mem_rows = mem_prompt.reshape(bsz * n_mem, d)
    mk, mv, mkb, mvb = [], [], [], []
    for l in range(depth):
        wkv = w_mem_kv[l].astype(BF16)
        k_l, v_l, kb_l, vb_l = _kv_proj(mem_rows, norm_mem[l].reshape(1, d), wkv[:, :mem_width],
                                        wkv[:, mem_width:], _tile2(mem_gk[l]))
        mk.append(k_l)
        mv.append(v_l)
        mkb.append(kb_l.reshape(bsz, n_mem, mem_width))
        mvb.append(vb_l.reshape(bsz, n_mem, mem_width))
    mem_shape = (depth, bsz, n_mem) + cache_mem_k.shape[-2:]
    p_mem_k = jnp.stack(mk).reshape(mem_shape)
    p_mem_v = jnp.stack(mv).reshape(mem_shape)
    s_zero = jnp.zeros((bsz,) + state_hgrn_0.shape[1:], F32)
    y_prompt, p_states, p_kv = _trunk(x_prompt, mkb, mvb, [s_zero] * w_in_a.shape[0], None, prm)

    dec_b = x_sample.shape[0]
    cmk = cache_mem_k.reshape(depth, dec_b, n_mem, mem_width).astype(BF16)
    cmv = cache_mem_v.reshape(depth, dec_b, n_mem, mem_width).astype(BF16)
    y_sample, s_states, s_kv = _trunk(x_sample, cmk, cmv, [state_hgrn_0, state_hgrn_1],
                                      (cache_fox_k, cache_fox_v, cache_fox_logf), prm)
    return (y_prompt, y_sample, p_states[0], p_states[1], p_kv[0], p_kv[1], p_kv[2], p_mem_k, p_mem_v,
            s_states[0], s_states[1], s_kv[0], s_kv[1], s_kv[2])
```

```python
import functools

import numpy as np
import jax
import jax.numpy as jnp
from jax import lax
from jax.experimental import pallas as pl
from jax.experimental.pallas import tpu as pltpu

F32 = jnp.float32
BF16 = jnp.bfloat16

EPS = 1e-6
K_MAX = 0.999999
NEG = -1e30
LOG2E = 1.4426950408889634

LANES = 128
HEAD_DIM = 64
CHUNK = 64
HG_DK = 128
FF_TILE = 256
ROW_TILE = 512
FOX_TQ = 2048
FOX_TK = 1024
FOX_COLS = 256
FOX_KEYS = 256
FOX_VROWS = HEAD_DIM + 16
PREP_SUB = 256
N_SPLIT = 3
ONES_LANE = 36
VMEM_LIMIT_BYTES = 56 * 1024 * 1024


def _cparams(*sem):
    return pltpu.CompilerParams(dimension_semantics=sem, vmem_limit_bytes=VMEM_LIMIT_BYTES)


def _resident(shape):
    nd = len(shape)
    return pl.BlockSpec(shape, lambda *_: (0,) * nd, pipeline_mode=pl.Buffered(1))


def _dot(a, b):
    return jnp.dot(a, b, preferred_element_type=F32)


def _dot_nt(a, b):
    return lax.dot_general(a, b, (((1,), (1,)), ((), ())), preferred_element_type=F32)


def _dot_tn(a, b):
    return lax.dot_general(a, b, (((0,), (0,)), ((), ())), preferred_element_type=F32)


def _split_bf16(x, n):
    parts = []
    r = x
    for _ in range(n):
        p = r.astype(BF16)
        parts.append(p)
        r = r - p.astype(F32)
    return parts


def _rms(x, g):
    ms = jnp.mean(x * x, axis=-1, keepdims=True)
    return x * lax.rsqrt(ms + EPS) * g


def _sigmoid(x):
    return 0.5 + 0.5 * jnp.tanh(0.5 * x)


def _silu(x):
    hx = 0.5 * x
    return hx + hx * jnp.tanh(hx)


def _even_lanes(shape):
    return lax.broadcasted_iota(jnp.int32, shape, len(shape) - 1) < HEAD_DIM


def _pair_rms(x2, g2):
    even = _even_lanes(x2.shape)
    sq = x2 * x2
    se = jnp.sum(jnp.where(even, sq, 0.0), axis=-1, keepdims=True)
    so = jnp.sum(jnp.where(even, 0.0, sq), axis=-1, keepdims=True)
    ms = jnp.where(even, se, so) * (1.0 / HEAD_DIM)
    return x2 * lax.rsqrt(ms + EPS) * g2


def _proj_split_kernel(x_ref, g_ref, w_ref, *out_refs, widths):
    xn = _rms(x_ref[...], g_ref[...]).astype(BF16)
    off = 0
    for o_ref, wd in zip(out_refs, widths):
        o_ref[...] = _dot(xn, w_ref[:, off:off + wd]).astype(o_ref.dtype)
        off += wd


def _proj_split(x, g, w, widths):
    n, d = x.shape
    tm = min(ROW_TILE, n)
    return pl.pallas_call(
        functools.partial(_proj_split_kernel, widths=widths),
        grid=(n // tm,),
        in_specs=[pl.BlockSpec((tm, d), lambda i: (i, 0)), _resident((1, d)), _resident(w.shape)],
        out_specs=[pl.BlockSpec((tm, wd), lambda i: (i, 0)) for wd in widths],
        out_shape=[jax.ShapeDtypeStruct((n, wd), F32) for wd in widths],
        compiler_params=_cparams("parallel"), name="proj_a",
    )(x, g, w)


HG_MXU_LEVELS = (2, 4)
HG_ROW_LEVELS = (8, 16, 32)
HG_BLOCK = 256


def _hgrn_consts():
    c = CHUNK
    t = np.arange(c)
    m = []
    for h in HG_MXU_LEVELS:
        blk = t // (2 * h)
        second = (t // h) % 2 == 1
        boundary = blk * 2 * h + h
        mh = np.zeros((c, c), np.float32)
        for r in range(c):
            if second[r]:
                mh[r, boundary[r]:r + 1] = 1.0
            else:
                mh[r, r + 1:boundary[r]] = 1.0
        m.append(mh)
    m.append(np.tril(np.ones((c, c), np.float32)))
    masks = []
    for h in (1,) + HG_MXU_LEVELS + HG_ROW_LEVELS:
        blk = t // (2 * h)
        second = (t // h) % 2 == 1
        masks.append((blk[:, None] == blk[None, :]) & second[:, None] & ~second[None, :])
    masks.append(np.eye(c, dtype=bool))
    return np.concatenate(m, 0), np.stack(masks).astype(np.float32)


def _row_level_exponent(b, h):
    pieces = []
    for r0 in range(0, CHUNK, 2 * h):
        rho = b[r0 + h - 1:r0 + h, :]
        pieces.append(rho - b[r0:r0 + h, :])
        pieces.append(b[r0 + h:r0 + 2 * h, :] - rho)
    return jnp.concatenate(pieces, axis=0)


def _hgrn_kernel(pq_ref, pf_ref, pi_ref, pg_ref, lbl_ref, gn_ref, s0_ref, m_ref, lm_ref,
                 o_ref, sfin_ref, st_ref, *, n_heads, t_valid, layer):
    blk_idx = pl.program_id(1)
    tb = pq_ref.shape[0]
    n_mxu = len(HG_MXU_LEVELS)

    @pl.when(blk_idx == 0)
    def _():
        for h in range(n_heads):
            st_ref[h] = s0_ref[h].T

    rows = [lbl_ref[i:i + 1, :] for i in range(lbl_ref.shape[0])]
    mx = functools.reduce(jnp.maximum, rows)
    es = [jnp.exp(r - mx) for r in rows]
    tot = functools.reduce(lambda a, b: a + b, es)
    ps = [e / tot for e in es]
    cum = ps[0]
    for i in range(1, layer + 1):
        cum = cum + ps[i]
    lb = cum - ps[0]

    m_all = m_ref[...]
    gn = gn_ref[...]
    heads = [slice(h * HG_DK, (h + 1) * HG_DK) for h in range(n_heads)]
    odd_row = lax.broadcasted_iota(jnp.int32, (CHUNK, pq_ref.shape[1]), 0) % 2 == 1
    n_lv = lm_ref.shape[0] - 1
    level_masks = [lm_ref[i] > 0.5 for i in range(n_lv)]
    on_diag = lm_ref[n_lv] > 0.5

    per_chunk = []
    for c in range(tb // CHUNK):
        rs = slice(c * CHUNK, (c + 1) * CHUNK)
        q = _silu(pq_ref[rs, :])
        kk = jnp.minimum((1.0 - lb) * _sigmoid(-pf_ref[rs, :]), K_MAX)
        if t_valid is not None:
            row = blk_idx * tb + c * CHUNK + lax.broadcasted_iota(jnp.int32, kk.shape, 0)
            kk = jnp.where(row < t_valid, kk, 0.0)
        f = 1.0 - kk
        g = jnp.log2(f)
        v_b = pi_ref[rs, :].astype(BF16)

        g_hi, g_lo = _split_bf16(g, 2)
        d_all = _dot(m_all, g_hi) + _dot(m_all, g_lo)
        b = d_all[n_mxu * CHUNK:]
        e_levels = [jnp.where(odd_row, f, 1.0)]
        e_levels += [jnp.exp2(d_all[i * CHUNK:(i + 1) * CHUNK]) for i in range(n_mxu)]
        e_levels += [jnp.exp2(_row_level_exponent(b, h)) for h in HG_ROW_LEVELS]

        q_b = q.astype(BF16)
        k_b = kk.astype(BF16)
        a = [jnp.where(on_diag, _dot_nt(q_b[:, sl], k_b[:, sl]), 0.0) for sl in heads]
        for in_level, el in zip(level_masks, e_levels):
            el_b = el.astype(BF16)
            q_l = q_b * el_b
            k_l = k_b * el_b
            a = [jnp.where(in_level, _dot_nt(q_l[:, sl], k_l[:, sl]), a_h) for a_h, sl in zip(a, heads)]

        b_last = b[CHUNK - 1:CHUNK, :]
        per_chunk.append(dict(
            a=[a_h.astype(BF16) for a_h in a], v=v_b, q_e=(q * jnp.exp2(b)).astype(BF16),
            k_e=(kk * jnp.exp2(b_last - b)).astype(BF16), decay=jnp.exp2(b_last)))

    for c, pc in enumerate(per_chunk):
        rs = slice(c * CHUNK, (c + 1) * CHUNK)
        for h, sl in enumerate(heads):
            st = st_ref[h]
            o = _dot_nt(pc["q_e"][:, sl], st.astype(BF16)) + _dot(pc["a"][h], pc["v"][:, sl])
            st_ref[h] = pc["decay"][:, sl] * st + _dot_tn(pc["v"][:, sl], pc["k_e"][:, sl])
            o_ref[rs, sl] = (_rms(o, gn) * _silu(pg_ref[rs, sl])).astype(o_ref.dtype)

    @pl.when(blk_idx == pl.num_programs(1) - 1)
    def _():
        for h in range(n_heads):
            sfin_ref[h] = st_ref[h].T


def _hgrn(pq, pf, pi, pg, lb_logits, gnorm, s0, layer, t_valid):
    b, t, w = pq.shape
    n_heads = w // HG_DK
    tb = HG_BLOCK if t % HG_BLOCK == 0 else CHUNK
    m_np, masks_np = _hgrn_consts()
    m_all = jnp.asarray(m_np, BF16)
    masks = jnp.asarray(masks_np, F32)
    tok = pl.BlockSpec((None, tb, w), lambda i, c: (i, c, 0))
    st_spec = pl.BlockSpec((None, n_heads, HG_DK, HG_DK), lambda i, c: (i, 0, 0, 0))
    return pl.pallas_call(
        functools.partial(_hgrn_kernel, n_heads=n_heads, layer=layer,
                          t_valid=None if t_valid == t else t_valid),
        grid=(b, t // tb),
        in_specs=[tok, tok, tok, tok, _resident(lb_logits.shape), _resident((1, HG_DK)), st_spec,
                  _resident(m_all.shape), _resident(masks.shape)],
        out_specs=[tok, st_spec],
        out_shape=[jax.ShapeDtypeStruct((b, t, w), BF16),
                   jax.ShapeDtypeStruct((b, n_heads, HG_DK, HG_DK), F32)],
        scratch_shapes=[pltpu.VMEM((n_heads, HG_DK, HG_DK), F32)],
        compiler_params=_cparams("parallel", "arbitrary"), name="hgrn",
    )(pq, pf, pi, pg, lb_logits, gnorm.reshape(1, HG_DK), s0, m_all, masks)


def _post_kernel(h_ref, om_ref, pm_ref, mk_ref, mv_ref, gq_ref, wo_ref, gf_ref, wu_ref, wd_ref,
                 out_ref, *, main_width, d_ff):
    nb, tm, d = h_ref.shape
    x = h_ref[...].reshape(nb * tm, d)
    attn = _dot(om_ref[...].reshape(nb * tm, main_width), wo_ref[:main_width, :])
    gq2 = gq_ref[...]
    n_pairs = pm_ref.shape[-1] // LANES
    for p in range(n_pairs):
        sl = slice(p * LANES, (p + 1) * LANES)
        per_batch = []
        for bi in range(nb):
            qn = _pair_rms(pm_ref[bi, :, sl], gq2) * (HEAD_DIM ** -0.5)
            even = _even_lanes(qn.shape)
            k2 = mk_ref[bi, :, sl]
            v2 = mv_ref[bi, :, sl]
            halves = []
            for own in (even, jnp.logical_not(even)):
                s = _dot_nt(jnp.where(own, qn, 0.0).astype(BF16), k2)
                e = jnp.exp(s - jnp.max(s, axis=-1, keepdims=True))
                l = jnp.sum(e, axis=-1, keepdims=True)
                halves.append(_dot(e.astype(BF16), v2) * (1.0 / l))
            per_batch.append(jnp.where(even, halves[0], halves[1]))
        o2 = per_batch[0] if nb == 1 else jnp.concatenate(per_batch, axis=0)
        attn = attn + _dot(o2.astype(BF16), wo_ref[main_width + p * LANES:main_width + (p + 1) * LANES, :])
    h1 = x + attn

    xn = _rms(h1, gf_ref[...]).astype(BF16)
    acc = jnp.zeros_like(h1)
    for f in range(d_ff // FF_TILE):
        gate = _dot(xn, wu_ref[:, f * FF_TILE:(f + 1) * FF_TILE])
        up = _dot(xn, wu_ref[:, d_ff + f * FF_TILE:d_ff + (f + 1) * FF_TILE])
        act = (_silu(gate) * up).astype(BF16)
        acc = acc + _dot(act, wd_ref[f * FF_TILE:(f + 1) * FF_TILE, :])
    out_ref[...] = (h1 + acc).reshape(nb, tm, d)


def _post(h, o_main, pm, mem_k, mem_v, gq2, w_out, g_ffn, w_up, w_down):
    b, t, d = h.shape
    tm = min(ROW_TILE, t)
    nb = b if b * t <= ROW_TILE else 1
    main_width = o_main.shape[-1]
    mem_width = pm.shape[-1]
    n_mem = mem_k.shape[1]
    d_ff = w_down.shape[0]
    row = lambda wd: pl.BlockSpec((nb, tm, wd), lambda i, r: (i, r, 0))
    mem = pl.BlockSpec((nb, n_mem, mem_width), lambda i, r: (i, 0, 0))
    return pl.pallas_call(
        functools.partial(_post_kernel, main_width=main_width, d_ff=d_ff),
        grid=(b // nb, t // tm),
        in_specs=[row(d), row(main_width), row(mem_width), mem, mem, _resident((1, LANES)),
                  _resident(w_out.shape), _resident((1, d)), _resident(w_up.shape),
                  _resident(w_down.shape)],
        out_specs=row(d),
        out_shape=jax.ShapeDtypeStruct((b, t, d), F32),
        compiler_params=_cparams("parallel", "parallel"), name="post",
    )(h, o_main, pm, mem_k, mem_v, gq2, w_out, g_ffn, w_up, w_down)


def _kv_proj_kernel(*refs, has_f, n_f):
    if has_f:
        (x_ref, g_ref, wk_ref, wv_ref, gk_ref, wf_ref, bf_ref,
         k_ref, v_ref, lf_ref, lfp_ref) = refs
    else:
        x_ref, g_ref, wk_ref, wv_ref, gk_ref, k_ref, v_ref, kb_ref, vb_ref = refs
    xn = _rms(x_ref[...], g_ref[...]).astype(BF16)
    gk2 = gk_ref[...]
    for p in range(k_ref.shape[-1] // LANES):
        sl = slice(p * LANES, (p + 1) * LANES)
        k2 = _pair_rms(_dot(xn, wk_ref[:, sl]), gk2)
        k_ref[:, sl] = k2
        if not has_f:
            kb_ref[:, sl] = k2.astype(BF16)
    v = _dot(xn, wv_ref[...])
    v_ref[...] = v
    if has_f:
        y = _dot(xn, wf_ref[...]) + bf_ref[...]
        lf = jnp.minimum(y, 0.0) - jnp.log1p(jnp.exp(-jnp.abs(y)))
        lane = lax.broadcasted_iota(jnp.int32, lf.shape, 1)
        lf = jnp.where(lane < n_f, lf, 0.0)
        lfp_ref[...] = lf
        lf_ref[...] = lf[:, :n_f]
    else:
        vb_ref[...] = v.astype(BF16)


def _kv_proj(x, g, wk, wv, gk2, wf=None, bf=None, n_f=0):
    n, d = x.shape
    tm = min(ROW_TILE, n)
    wk_w = wk.shape[1]
    has_f = wf is not None
    row = lambda wd: pl.BlockSpec((tm, wd), lambda i: (i, 0))
    in_specs = [row(d), _resident((1, d)), _resident(wk.shape), _resident(wv.shape),
                _resident((1, LANES))]
    args = [x, g, wk, wv, gk2]
    if has_f:
        in_specs += [_resident(wf.shape), _resident((1, LANES))]
        args += [wf, bf]
        out_specs = [row(wk_w), row(wk_w), row(n_f), row(LANES)]
        out_shape = [jax.ShapeDtypeStruct((n, wk_w), F32), jax.ShapeDtypeStruct((n, wk_w), F32),
                     jax.ShapeDtypeStruct((n, n_f), F32), jax.ShapeDtypeStruct((n, LANES), F32)]
    else:
        out_specs = [row(wk_w)] * 4
        out_shape = [jax.ShapeDtypeStruct((n, wk_w), F32), jax.ShapeDtypeStruct((n, wk_w), F32),
                     jax.ShapeDtypeStruct((n, wk_w), BF16), jax.ShapeDtypeStruct((n, wk_w), BF16)]
    return pl.pallas_call(
        functools.partial(_kv_proj_kernel, has_f=has_f, n_f=n_f),
        grid=(n // tm,),
        in_specs=in_specs, out_specs=out_specs, out_shape=out_shape,
        compiler_params=_cparams("parallel"), name="kv_proj",
    )(*args)


def _bias_lane_base(h):
    return h * LANES + (HEAD_DIM if h % 2 == 0 else 0)


def _pack_consts(n_heads):
    p = np.zeros((N_SPLIT, LANES, LANES), np.float32)
    for s in range(N_SPLIT):
        for h in range(n_heads):
            p[s, h, s * n_heads + h] = 1.0
    ones_row = np.zeros((1, LANES), np.float32)
    ones_row[0, ONES_LANE] = 1.0
    s_k = np.zeros((LANES, n_heads * LANES), np.float32)
    s_q = np.zeros((LANES, n_heads * LANES), np.float32)
    for h in range(n_heads):
        base = _bias_lane_base(h)
        for s in range(N_SPLIT):
            s_q[s * n_heads + h, base + s] = 1.0
            s_q[ONES_LANE, base + N_SPLIT + s] = 1.0
            s_k[ONES_LANE, base + s] = 1.0
            s_k[s * n_heads + h, base + N_SPLIT + s] = -1.0
    return p, ones_row, s_q, s_k


def _own_half(shape, h):
    even = _even_lanes(shape)
    return even if h % 2 == 0 else jnp.logical_not(even)


def _kv_prep_kernel(*refs, n_heads, n_main):
    n_src = 3 if n_main is None else 6
    tril_ref, p_ref, ones_ref, sk_ref, kaug_ref, vt_ref, c3_ref, carry_ref = refs[n_src:]
    tl = kaug_ref.shape[0]
    j = pl.program_id(1)

    @pl.when(j == 0)
    def _():
        carry_ref[...] = jnp.zeros_like(carry_ref)

    def tile(k_get, v_get, lf):
        tril = tril_ref[...]
        sub = tril.shape[0]
        parts = _split_bf16(lf, N_SPLIT)
        total = carry_ref[...]
        sums = []
        for r in range(tl // sub):
            cs_r = total
            for part in parts:
                cs_r = cs_r + _dot(tril, part[r * sub:(r + 1) * sub])
            total = cs_r[sub - 1:sub, :]
            sums.append(cs_r)
        cs = jnp.concatenate(sums, axis=0)
        carry_ref[...] = total

        c3 = ones_ref[...]
        for s, part in enumerate(_split_bf16(cs * LOG2E, N_SPLIT)):
            c3 = c3 + _dot(part, p_ref[s])
        c3 = c3.astype(BF16)
        c3_ref[...] = c3
        kbias = _dot(c3, sk_ref[...])
        ones_block = (lax.broadcasted_iota(jnp.int32, (FOX_VROWS - HEAD_DIM, tl), 0) == 0).astype(F32)
        for p in range(n_heads // 2):
            pair = slice(p * LANES, (p + 1) * LANES)
            k2 = k_get(pair)
            vt2 = v_get(pair).T
            for e in range(2):
                h = 2 * p + e
                slab = slice(h * LANES, (h + 1) * LANES)
                kaug_ref[:, slab] = jnp.where(_own_half(k2.shape, h), k2, kbias[:, slab]).astype(BF16)
                vt_ref[h] = jnp.concatenate([vt2[e * HEAD_DIM:(e + 1) * HEAD_DIM], ones_block], 0).astype(BF16)

    def from_refs(k_ref, v_ref, lf_ref):
        def grow(x):
            if x.shape[0] == tl:
                return x
            return jnp.concatenate([x, jnp.zeros((tl - x.shape[0], x.shape[1]), x.dtype)], 0)
        tile(lambda pair: grow(k_ref[:, pair]), lambda pair: grow(v_ref[:, pair]), grow(lf_ref[...]))

    if n_main is None:
        from_refs(*refs[:3])
    else:
        pl.when(j < n_main)(functools.partial(from_refs, *refs[:3]))
        pl.when(j >= n_main)(functools.partial(from_refs, *refs[3:6]))


def _kv_prep(k, v, lf_pad, n_heads, tail=None):
    b, l, w = k.shape
    tl = min(FOX_TK, l)
    n_main = l // tl
    n_tiles = n_main + (0 if tail is None else 1)
    p_np, ones_np, _, sk_np = _pack_consts(n_heads)
    sub = min(PREP_SUB, tl)
    tril = jnp.asarray(np.tril(np.ones((sub, sub), np.float32)), BF16)
    row = lambda wd: pl.BlockSpec((None, tl, wd), lambda i, j: (i, j, 0))
    main = lambda wd: pl.BlockSpec((None, tl, wd), lambda i, j: (i, jnp.minimum(j, n_main - 1), 0))
    srcs = [k, v, lf_pad]
    src_specs = [main(w), main(w), main(LANES)]
    if tail is not None:
        assert l % tl == 0 and tail[0].shape[1] <= tl
        srcs += list(tail)
        src_specs += [pl.BlockSpec((None,) + a.shape[1:], lambda i, j: (i, 0, 0)) for a in tail]
    return pl.pallas_call(
        functools.partial(_kv_prep_kernel, n_heads=n_heads, n_main=None if tail is None else n_main),
        grid=(b, n_tiles),
        in_specs=src_specs + [_resident((sub, sub)), _resident(p_np.shape),
                              _resident((1, LANES)), _resident(sk_np.shape)],
        out_specs=[row(n_heads * LANES),
                   pl.BlockSpec((None, n_heads, None, FOX_VROWS, tl), lambda i, j: (i, 0, j, 0, 0)),
                   row(LANES)],
        out_shape=[jax.ShapeDtypeStruct((b, n_tiles * tl, n_heads * LANES), BF16),
                   jax.ShapeDtypeStruct((b, n_heads, n_tiles, FOX_VROWS, tl), BF16),
                   jax.ShapeDtypeStruct((b, n_tiles * tl, LANES), BF16)],
        scratch_shapes=[pltpu.VMEM((1, LANES), F32)],
        compiler_params=_cparams("parallel", "arbitrary"), name="kv_prep",
    )(*srcs, tril, jnp.asarray(p_np, BF16), jnp.asarray(ones_np, F32), jnp.asarray(sk_np, BF16))


def _proj_b_kernel(x_ref, g_ref, wq_ref, wm_ref, gq_ref, c3_ref, sq_ref, qt_ref, pm_ref,
                   *, n_heads):
    xn = _rms(x_ref[...], g_ref[...]).astype(BF16)
    pm_ref[...] = _dot(xn, wm_ref[...])
    qbias = _dot(c3_ref[...], sq_ref[...])
    gq2 = gq_ref[...]
    for p in range(n_heads // 2):
        pair = slice(p * LANES, (p + 1) * LANES)
        qn = _pair_rms(_dot(xn, wq_ref[:, pair]), gq2) * (LOG2E * HEAD_DIM ** -0.5)
        for h in (2 * p, 2 * p + 1):
            slab = slice(h * LANES, (h + 1) * LANES)
            q_slab = jnp.where(_own_half(qn.shape, h), qn, qbias[:, slab])
            qt_ref[h] = q_slab.T.astype(BF16)


def _proj_b(x, g, wq, wm, gq2, c3q, n_heads):
    b, t, d = x.shape
    tm = min(ROW_TILE, t)
    _, _, sq_np, _ = _pack_consts(n_heads)
    row = lambda wd: pl.BlockSpec((None, tm, wd), lambda i, r: (i, r, 0))
    return pl.pallas_call(
        functools.partial(_proj_b_kernel, n_heads=n_heads),
        grid=(b, t // tm),
        in_specs=[row(d), _resident((1, d)), _resident(wq.shape), _resident(wm.shape),
                  _resident((1, LANES)), row(LANES), _resident(sq_np.shape)],
        out_specs=[pl.BlockSpec((None, n_heads, LANES, tm), lambda i, r: (i, 0, 0, r)),
                   row(wm.shape[1])],
        out_shape=[jax.ShapeDtypeStruct((b, n_heads, LANES, t), BF16),
                   jax.ShapeDtypeStruct((b, t, wm.shape[1]), F32)],
        compiler_params=_cparams("parallel", "parallel"), name="proj_b",
    )(x, g, wq, wm, gq2, c3q, jnp.asarray(sq_np, BF16))


def _fox_kernel(q_ref, k_ref, vt_ref, o_ref, acc_ref, s_ref, *, t0, tq, tk, n_q):
    cw = min(FOX_COLS, tq)
    kc = min(FOX_KEYS, tk)
    acc_ref[...] = jnp.zeros_like(acc_ref)

    def attend(qi):
        first_tile = t0 // tk + qi * (tq // tk)
        blocks = [(e, c) for e in range(2) for c in range(tq // cw)]
        items = [(j, e, c, tk) for j in range(first_tile) for e, c in blocks]
        for d in range(max(1, tq // tk)):
            for e, c in blocks:
                seen = min(max((c + 1) * cw - d * tk, 0), tk) if cw == FOX_COLS else tk
                if seen > 0:
                    items.append((first_tile + d, e, c, seen))

        def score_chunk(pos, r, m_run):
            j, e, c, _ = items[pos]
            first_key = j * tk + r * kc
            s = _dot(k_ref[first_key:first_key + kc, e * LANES:(e + 1) * LANES],
                     q_ref[e, :, c * cw:(c + 1) * cw])
            first_query = t0 + qi * tq + c * cw
            if first_key + kc - 1 > first_query:
                key = first_key + lax.broadcasted_iota(jnp.int32, s.shape, 0)
                qry = first_query + lax.broadcasted_iota(jnp.int32, s.shape, 1)
                s = jnp.where(key <= qry, s, NEG)
            s_ref[pos % 2, r * kc:(r + 1) * kc, :] = s
            return jnp.maximum(m_run, jnp.max(s, axis=0, keepdims=True))

        ms = {blk: jnp.full((1, cw), NEG, F32) for blk in blocks}
        m_next = ms[items[0][1:3]]
        for r in range(items[0][3] // kc):
            m_next = score_chunk(0, r, m_next)
        for pos, (j, e, c, seen) in enumerate(items):
            m_new = m_next
            n_nxt = 0
            if pos + 1 < len(items):
                m_next = ms[items[pos + 1][1:3]] if items[pos + 1][1:3] != (e, c) else m_new
                n_nxt = items[pos + 1][3] // kc
            pv = None
            for r in range(max(seen // kc, n_nxt)):
                if r < n_nxt:
                    m_next = score_chunk(pos + 1, r, m_next)
                if r < seen // kc:
                    p = jnp.exp2(s_ref[pos % 2, r * kc:(r + 1) * kc, :] - m_new).astype(BF16)
                    pv_r = _dot(vt_ref[e, j][:, r * kc:(r + 1) * kc], p)
                    pv = pv_r if pv is None else pv + pv_r
            cols = slice(c * cw, (c + 1) * cw)
            acc_ref[e, :, cols] = jnp.exp2(ms[(e, c)] - m_new) * acc_ref[e, :, cols] + pv
            ms[(e, c)] = m_new

    if n_q == 1:
        attend(0)
    else:
        for qi in range(n_q):
            pl.when(pl.program_id(2) == qi)(functools.partial(attend, qi))

    halves = [acc_ref[e, :HEAD_DIM, :] * (1.0 / acc_ref[e, HEAD_DIM:HEAD_DIM + 1, :]) for e in range(2)]
    o_ref[...] = jnp.concatenate(halves, axis=0).T.astype(o_ref.dtype)


def _fox(q_t, k_aug, vt, t0):
    b, n_heads, _, t = q_t.shape
    l = k_aug.shape[1]
    n_pairs = n_heads // 2
    tq = min(FOX_TQ, t)
    tk = vt.shape[-1]
    assert t0 % tk == 0 and (tq % tk == 0 or t == tq), (t0, tq, tk, t)
    return pl.pallas_call(
        functools.partial(_fox_kernel, t0=t0, tq=tq, tk=tk, n_q=t // tq),
        grid=(b, n_pairs, t // tq),
        in_specs=[pl.BlockSpec((None, 2, LANES, tq), lambda bi, p, i: (bi, p, 0, i)),
                  pl.BlockSpec((None, l, 2 * LANES), lambda bi, p, i: (bi, 0, p)),
                  pl.BlockSpec((None, 2, l // tk, FOX_VROWS, tk), lambda bi, p, i: (bi, p, 0, 0, 0))],
        out_specs=pl.BlockSpec((None, tq, LANES), lambda bi, p, i: (bi, i, p)),
        out_shape=jax.ShapeDtypeStruct((b, t, n_pairs * LANES), BF16),
        scratch_shapes=[pltpu.VMEM((2, FOX_VROWS, tq), F32), pltpu.VMEM((2, tk, min(FOX_COLS, tq)), F32)],
        compiler_params=_cparams("parallel", "parallel", "arbitrary"), name="fox",
    )(q_t, k_aug, vt)


def _tile2(g):
    return jnp.concatenate([g, g]).reshape(1, LANES).astype(F32)


def _trunk(x, mem_k, mem_v, hg_states, past, prm):
    b, t, d = x.shape
    depth = prm["norm_mix"].shape[0]
    n_a = prm["w_in_a"].shape[0]
    mem_width = mem_k[0].shape[-1]
    main_width = prm["w_in_b"].shape[2] - mem_width
    n_fox = main_width // HEAD_DIM
    t_pad = -(-t // CHUNK) * CHUNK
    h = x
    new_states = []
    new_kv = None
    for l in range(depth):
        g_mix = prm["norm_mix"][l].reshape(1, d)
        if l < n_a:
            widths = (main_width,) * 4 + (mem_width,)
            pq, pf, pi, pg, pm = _proj_split(h.reshape(b * t, d), g_mix, prm["w_in_a"][l], widths)
            chunked = [jnp.pad(a.reshape(b, t, main_width), ((0, 0), (0, t_pad - t), (0, 0)))
                       for a in (pq, pf, pi, pg)]
            o_main, s_new = _hgrn(*chunked, prm["lb_logits"], prm["hg_gnorm"][l], hg_states[l],
                                  layer=l, t_valid=t)
            o_main = o_main[:, :t]
            new_states.append(s_new)
        else:
            j = l - n_a
            q_t, pm = _proj_b(h, g_mix, prm["w_q_b"][j], prm["w_m_b"][j],
                              _tile2(prm["fox_gq"][j]), c3_q, n_fox)
            o_main = _fox(q_t, k_aug, v_t, t0)
        h = _post(h, o_main, pm.reshape(b, t, -1), mem_k[l], mem_v[l], _tile2(prm["mem_gq"][l]),
                  prm["w_out"][l], prm["norm_ffn"][l].reshape(1, d), prm["w_ffn_up"][l],
                  prm["w_ffn_down"][l])
        if l == n_a - 1:
            k_new, v_new, lf_new, lf_pad = _kv_proj(
                h.reshape(b * t, d), prm["norm_kv"].reshape(1, d), prm["w_k"], prm["w_v"],
                _tile2(prm["fox_gk"]), prm["w_f"], prm["b_f"], n_fox)
            k_new = k_new.reshape(b, t, main_width)
            v_new = v_new.reshape(b, t, main_width)
            lf_pad = lf_pad.reshape(b, t, LANES)
            new_kv = (k_new.reshape(b, t, n_fox, HEAD_DIM), v_new.reshape(b, t, n_fox, HEAD_DIM),
                      lf_new.reshape(b, t, n_fox))
            if past is None:
                t0 = 0
                k_aug, v_t, c3 = _kv_prep(k_new, v_new, lf_pad, n_fox)
            else:
                t0 = past[0].shape[1]
                lf_past = jnp.pad(past[2].astype(F32), ((0, 0), (0, 0), (0, LANES - n_fox)))
                k_aug, v_t, c3 = _kv_prep(past[0].reshape(b, t0, main_width), past[1].reshape(b, t0, main_width),
                                          lf_past, n_fox, tail=(k_new, v_new, lf_pad))
            c3_q = c3[:, t0:t0 + t]
    return h, new_states, new_kv


def kernel(x_prompt, x_sample, mem_prompt, state_hgrn_0, state_hgrn_1, cache_fox_k, cache_fox_v, cache_fox_logf, cache_mem_k, cache_mem_v, norm_mix, w_in_a, lb_logits, hg_gnorm, w_in_b, fox_gq, norm_kv, w_kv, b_f, fox_gk, norm_mem, w_mem_kv, mem_gq, mem_gk, w_out, norm_ffn, w_ffn_up, w_ffn_down):
    depth, d = norm_mix.shape
    mem_width = cache_mem_k.shape[-1] * cache_mem_k.shape[-2]
    main_width = w_in_b.shape[2] - mem_width
    n_fox = b_f.shape[0]
    bsz, n_mem, _ = mem_prompt.shape

    w_f = jnp.pad(w_kv[:, 2 * main_width:], ((0, 0), (0, LANES - n_fox)))
    prm = {
        "norm_mix": norm_mix, "w_in_a": w_in_a.astype(BF16), "lb_logits": lb_logits.astype(F32),
        "hg_gnorm": hg_gnorm, "w_in_b": w_in_b,
        "w_q_b": w_in_b[:, :, :main_width].astype(BF16), "w_m_b": w_in_b[:, :, main_width:].astype(BF16),
        "fox_gq": fox_gq, "norm_kv": norm_kv,
        "w_k": w_kv[:, :main_width].astype(BF16), "w_v": w_kv[:, main_width:2 * main_width].astype(BF16),
        "w_f": w_f.astype(BF16), "b_f": jnp.pad(b_f, (0, LANES - n_fox)).reshape(1, LANES).astype(F32),
        "fox_gk": fox_gk, "mem_gq": mem_gq, "w_out": w_out.astype(BF16), "norm_ffn": norm_ffn,
        "w_ffn_up": w_ffn_up.astype(BF16), "w_ffn_down": w_ffn_down.astype(BF16),
    }

    mem_rows = mem_prompt.reshape(bsz * n_mem, d)
    mk, mv, mkb, mvb = [], [], [], []
    for l in range(depth):
        wkv = w_mem_kv[l].astype(BF16)
        k_l, v_l, kb_l, vb_l = _kv_proj(mem_rows, norm_mem[l].reshape(1, d), wkv[:, :mem_width],
                                        wkv[:, mem_width:], _tile2(mem_gk[l]))
        mk.append(k_l)
        mv.append(v_l)
        mkb.append(kb_l.reshape(bsz, n_mem, mem_width))
        mvb.append(vb_l.reshape(bsz, n_mem, mem_width))
    mem_shape = (depth, bsz, n_mem) + cache_mem_k.shape[-2:]
    p_mem_k = jnp.stack(mk).reshape(mem_shape)
    p_mem_v = jnp.stack(mv).reshape(mem_shape)
    s_zero = jnp.zeros((bsz,) + state_hgrn_0.shape[1:], F32)
    y_prompt, p_states, p_kv = _trunk(x_prompt, mkb, mvb, [s_zero] * w_in_a.shape[0], None, prm)

    dec_b = x_sample.shape[0]
    cmk = cache_mem_k.reshape(depth, dec_b, n_mem, mem_width).astype(BF16)
    cmv = cache_mem_v.reshape(depth, dec_b, n_mem, mem_width).astype(BF16)
    y_sample, s_states, s_kv = _trunk(x_sample, cmk, cmv, [state_hgrn_0, state_hgrn_1],
                                      (cache_fox_k, cache_fox_v, cache_fox_logf), prm)
    return (y_prompt, y_sample, p_states[0], p_states[1], p_kv[0], p_kv[1], p_kv[2], p_mem_k, p_mem_v,
            s_states[0], s_states[1], s_kv[0], s_kv[1], s_kv[2])
```

```python
import functools

import numpy as np
import jax
import jax.numpy as jnp
from jax import lax
from jax.experimental import pallas as pl
from jax.experimental.pallas import tpu as pltpu

F32 = jnp.float32
BF16 = jnp.bfloat16

EPS = 1e-6
K_MAX = 0.999999
NEG = -1e30
LOG2E = 1.4426950408889634

LANES = 128
HEAD_DIM = 64
CHUNK = 64
HG_DK = 128
FF_TILE = 256
ROW_TILE = 512
FOX_TQ = 2048
FOX_TK = 1024
FOX_COLS = 256
FOX_KEYS = 256
FOX_VROWS = HEAD_DIM + 16
PREP_SUB = 256
N_SPLIT = 3
ONES_LANE = 36
VMEM_LIMIT_BYTES = 56 * 1024 * 1024


def _cparams(*sem):
    return pltpu.CompilerParams(dimension_semantics=sem, vmem_limit_bytes=VMEM_LIMIT_BYTES)


def _resident(shape):
    nd = len(shape)
    return pl.BlockSpec(shape, lambda *_: (0,) * nd, pipeline_mode=pl.Buffered(1))


def _dot(a, b):
    return jnp.dot(a, b, preferred_element_type=F32)


def _dot_nt(a, b):
    return lax.dot_general(a, b, (((1,), (1,)), ((), ())), preferred_element_type=F32)


def _dot_tn(a, b):
    return lax.dot_general(a, b, (((0,), (0,)), ((), ())), preferred_element_type=F32)


def _split_bf16(x, n):
    parts = []
    r = x
    for _ in range(n):
        p = r.astype(BF16)
        parts.append(p)
        r = r - p.astype(F32)
    return parts


def _rms(x, g):
    ms = jnp.mean(x * x, axis=-1, keepdims=True)
    return x * lax.rsqrt(ms + EPS) * g


def _sigmoid(x):
    return 0.5 + 0.5 * jnp.tanh(0.5 * x)


def _silu(x):
    hx = 0.5 * x
    return hx + hx * jnp.tanh(hx)


def _even_lanes(shape):
    return lax.broadcasted_iota(jnp.int32, shape, len(shape) - 1) < HEAD_DIM


def _pair_rms(x2, g2):
    even = _even_lanes(x2.shape)
    sq = x2 * x2
    se = jnp.sum(jnp.where(even, sq, 0.0), axis=-1, keepdims=True)
    so = jnp.sum(jnp.where(even, 0.0, sq), axis=-1, keepdims=True)
    ms = jnp.where(even, se, so) * (1.0 / HEAD_DIM)
    return x2 * lax.rsqrt(ms + EPS) * g2


def _proj_split_kernel(x_ref, g_ref, w_ref, *out_refs, widths):
    xn = _rms(x_ref[...], g_ref[...]).astype(BF16)
    off = 0
    for o_ref, wd in zip(out_refs, widths):
        o_ref[...] = _dot(xn, w_ref[:, off:off + wd]).astype(o_ref.dtype)
        off += wd


def _proj_split(x, g, w, widths):
    n, d = x.shape
    tm = min(ROW_TILE, n)
    return pl.pallas_call(
        functools.partial(_proj_split_kernel, widths=widths),
        grid=(n // tm,),
        in_specs=[pl.BlockSpec((tm, d), lambda i: (i, 0)), _resident((1, d)), _resident(w.shape)],
        out_specs=[pl.BlockSpec((tm, wd), lambda i: (i, 0)) for wd in widths],
        out_shape=[jax.ShapeDtypeStruct((n, wd), F32) for wd in widths],
        compiler_params=_cparams("parallel"), name="proj_a",
    )(x, g, w)


HG_MXU_LEVELS = (2, 4)
HG_ROW_LEVELS = (8, 16, 32)
HG_BLOCK = 512


def _hgrn_consts():
    c = CHUNK
    t = np.arange(c)
    m = []
    for h in HG_MXU_LEVELS:
        blk = t // (2 * h)
        second = (t // h) % 2 == 1
        boundary = blk * 2 * h + h
        mh = np.zeros((c, c), np.float32)
        for r in range(c):
            if second[r]:
                mh[r, boundary[r]:r + 1] = 1.0
            else:
                mh[r, r + 1:boundary[r]] = 1.0
        m.append(mh)
    m.append(np.tril(np.ones((c, c), np.float32)))
    masks = []
    for h in (1,) + HG_MXU_LEVELS + HG_ROW_LEVELS:
        blk = t // (2 * h)
        second = (t // h) % 2 == 1
        masks.append((blk[:, None] == blk[None, :]) & second[:, None] & ~second[None, :])
    masks.append(np.eye(c, dtype=bool))
    return np.concatenate(m, 0), np.stack(masks).astype(np.float32)


def _row_level_exponent(b, h):
    pieces = []
    for r0 in range(0, CHUNK, 2 * h):
        rho = b[r0 + h - 1:r0 + h, :]
        pieces.append(rho - b[r0:r0 + h, :])
        pieces.append(b[r0 + h:r0 + 2 * h, :] - rho)
    return jnp.concatenate(pieces, axis=0)


def _hgrn_kernel(pq_ref, pf_ref, pi_ref, pg_ref, lbl_ref, gn_ref, s0_ref, m_ref, lm_ref,
                 o_ref, sfin_ref, st_ref, *, n_heads, t_valid, layer):
    blk_idx = pl.program_id(1)
    tb = pq_ref.shape[0]
    n_mxu = len(HG_MXU_LEVELS)

    @pl.when(blk_idx == 0)
    def _():
        for h in range(n_heads):
            st_ref[h] = s0_ref[h].T

    rows = [lbl_ref[i:i + 1, :] for i in range(lbl_ref.shape[0])]
    mx = functools.reduce(jnp.maximum, rows)
    es = [jnp.exp(r - mx) for r in rows]
    tot = functools.reduce(lambda a, b: a + b, es)
    ps = [e / tot for e in es]
    cum = ps[0]
    for i in range(1, layer + 1):
        cum = cum + ps[i]
    lb = cum - ps[0]

    m_all = m_ref[...]
    gn = gn_ref[...]
    heads = [slice(h * HG_DK, (h + 1) * HG_DK) for h in range(n_heads)]
    odd_row = lax.broadcasted_iota(jnp.int32, (CHUNK, pq_ref.shape[1]), 0) % 2 == 1
    n_lv = lm_ref.shape[0] - 1
    level_masks = [lm_ref[i] > 0.5 for i in range(n_lv)]
    on_diag = lm_ref[n_lv] > 0.5

    per_chunk = []
    for c in range(tb // CHUNK):
        rs = slice(c * CHUNK, (c + 1) * CHUNK)
        q = _silu(pq_ref[rs, :])
        kk = jnp.minimum((1.0 - lb) * _sigmoid(-pf_ref[rs, :]), K_MAX)
        if t_valid is not None:
            row = blk_idx * tb + c * CHUNK + lax.broadcasted_iota(jnp.int32, kk.shape, 0)
            kk = jnp.where(row < t_valid, kk, 0.0)
        f = 1.0 - kk
        g = jnp.log2(f)
        v_b = pi_ref[rs, :].astype(BF16)

        g_hi, g_lo = _split_bf16(g, 2)
        d_all = _dot(m_all, g_hi) + _dot(m_all, g_lo)
        b = d_all[n_mxu * CHUNK:]
        e_levels = [jnp.where(odd_row, f, 1.0)]
        e_levels += [jnp.exp2(d_all[i * CHUNK:(i + 1) * CHUNK]) for i in range(n_mxu)]
        e_levels += [jnp.exp2(_row_level_exponent(b, h)) for h in HG_ROW_LEVELS]

        q_b = q.astype(BF16)
        k_b = kk.astype(BF16)
        a = [jnp.where(on_diag, _dot_nt(q_b[:, sl], k_b[:, sl]), 0.0) for sl in heads]
        for in_level, el in zip(level_masks, e_levels):
            el_b = el.astype(BF16)
            q_l = q_b * el_b
            k_l = k_b * el_b
            a = [jnp.where(in_level, _dot_nt(q_l[:, sl], k_l[:, sl]), a_h) for a_h, sl in zip(a, heads)]

        b_last = b[CHUNK - 1:CHUNK, :]
        per_chunk.append(dict(
            a=[a_h.astype(BF16) for a_h in a], v=v_b, q_e=(q * jnp.exp2(b)).astype(BF16),
            k_e=(kk * jnp.exp2(b_last - b)).astype(BF16), decay=jnp.exp2(b_last)))

    for c, pc in enumerate(per_chunk):
        rs = slice(c * CHUNK, (c + 1) * CHUNK)
        for h, sl in enumerate(heads):
            st = st_ref[h]
            o = _dot_nt(pc["q_e"][:, sl], st.astype(BF16)) + _dot(pc["a"][h], pc["v"][:, sl])
            st_ref[h] = pc["decay"][:, sl] * st + _dot_tn(pc["v"][:, sl], pc["k_e"][:, sl])
            o_ref[rs, sl] = (_rms(o, gn) * _silu(pg_ref[rs, sl])).astype(o_ref.dtype)

    @pl.when(blk_idx == pl.num_programs(1) - 1)
    def _():
        for h in range(n_heads):
            sfin_ref[h] = st_ref[h].T


def _hgrn(pq, pf, pi, pg, lb_logits, gnorm, s0, layer, t_valid):
    b, t, w = pq.shape
    n_heads = w // HG_DK
    tb = HG_BLOCK if t % HG_BLOCK == 0 else CHUNK
    m_np, masks_np = _hgrn_consts()
    m_all = jnp.asarray(m_np, BF16)
    masks = jnp.asarray(masks_np, F32)
    tok = pl.BlockSpec((None, tb, w), lambda i, c: (i, c, 0))
    st_spec = pl.BlockSpec((None, n_heads, HG_DK, HG_DK), lambda i, c: (i, 0, 0, 0))
    return pl.pallas_call(
        functools.partial(_hgrn_kernel, n_heads=n_heads, layer=layer,
                          t_valid=None if t_valid == t else t_valid),
        grid=(b, t // tb),
        in_specs=[tok, tok, tok, tok, _resident(lb_logits.shape), _resident((1, HG_DK)), st_spec,
                  _resident(m_all.shape), _resident(masks.shape)],
        out_specs=[tok, st_spec],
        out_shape=[jax.ShapeDtypeStruct((b, t, w), BF16),
                   jax.ShapeDtypeStruct((b, n_heads, HG_DK, HG_DK), F32)],
        scratch_shapes=[pltpu.VMEM((n_heads, HG_DK, HG_DK), F32)],
        compiler_params=_cparams("parallel", "arbitrary"), name="hgrn",
    )(pq, pf, pi, pg, lb_logits, gnorm.reshape(1, HG_DK), s0, m_all, masks)


def _post_kernel(h_ref, om_ref, pm_ref, mk_ref, mv_ref, gq_ref, wo_ref, gf_ref, wu_ref, wd_ref,
                 out_ref, *, main_width, d_ff):
    nb, tm, d = h_ref.shape
    x = h_ref[...].reshape(nb * tm, d)
    attn = _dot(om_ref[...].reshape(nb * tm, main_width), wo_ref[:main_width, :])
    gq2 = gq_ref[...]
    n_pairs = pm_ref.shape[-1] // LANES
    o_mem = []
    for p in range(n_pairs):
        sl = slice(p * LANES, (p + 1) * LANES)
        per_batch = []
        for bi in range(nb):
            qn = _pair_rms(pm_ref[bi, :, sl], gq2) * (HEAD_DIM ** -0.5)
            even = _even_lanes(qn.shape)
            k2 = mk_ref[bi, :, sl]
            v2 = mv_ref[bi, :, sl]
            halves = []
            for own in (even, jnp.logical_not(even)):
                s = _dot_nt(jnp.where(own, qn, 0.0).astype(BF16), k2)
                e = jnp.exp(s - jnp.max(s, axis=-1, keepdims=True))
                l = jnp.sum(e, axis=-1, keepdims=True)
                halves.append(_dot(e.astype(BF16), v2) * (1.0 / l))
            per_batch.append(jnp.where(even, halves[0], halves[1]))
        o2 = per_batch[0] if nb == 1 else jnp.concatenate(per_batch, axis=0)
        o_mem.append(o2.astype(BF16))
    attn = attn + _dot(jnp.concatenate(o_mem, axis=1), wo_ref[main_width:, :])
    h1 = x + attn

    xn = _rms(h1, gf_ref[...]).astype(BF16)
    acc = jnp.zeros_like(h1)
    for f in range(d_ff // FF_TILE):
        gate = _dot(xn, wu_ref[:, f * FF_TILE:(f + 1) * FF_TILE])
        up = _dot(xn, wu_ref[:, d_ff + f * FF_TILE:d_ff + (f + 1) * FF_TILE])
        act = (_silu(gate) * up).astype(BF16)
        acc = acc + _dot(act, wd_ref[f * FF_TILE:(f + 1) * FF_TILE, :])
    out_ref[...] = (h1 + acc).reshape(nb, tm, d)


def _post(h, o_main, pm, mem_k, mem_v, gq2, w_out, g_ffn, w_up, w_down):
    b, t, d = h.shape
    tm = min(ROW_TILE, t)
    nb = b if b * t <= ROW_TILE else 1
    main_width = o_main.shape[-1]
    mem_width = pm.shape[-1]
    n_mem = mem_k.shape[1]
    d_ff = w_down.shape[0]
    row = lambda wd: pl.BlockSpec((nb, tm, wd), lambda i, r: (i, r, 0))
    mem = pl.BlockSpec((nb, n_mem, mem_width), lambda i, r: (i, 0, 0))
    return pl.pallas_call(
        functools.partial(_post_kernel, main_width=main_width, d_ff=d_ff),
        grid=(b // nb, t // tm),
        in_specs=[row(d), row(main_width), row(mem_width), mem, mem, _resident((1, LANES)),
                  _resident(w_out.shape), _resident((1, d)), _resident(w_up.shape),
                  _resident(w_down.shape)],
        out_specs=row(d),
        out_shape=jax.ShapeDtypeStruct((b, t, d), F32),
        compiler_params=_cparams("parallel", "parallel"), name="post",
    )(h, o_main, pm, mem_k, mem_v, gq2, w_out, g_ffn, w_up, w_down)


def _kv_proj_kernel(*refs, has_f, n_f):
    if has_f:
        (x_ref, g_ref, wk_ref, wv_ref, gk_ref, wf_ref, bf_ref,
         k_ref, v_ref, lf_ref, lfp_ref) = refs
    else:
        x_ref, g_ref, wk_ref, wv_ref, gk_ref, k_ref, v_ref, kb_ref, vb_ref = refs
    xn = _rms(x_ref[...], g_ref[...]).astype(BF16)
    gk2 = gk_ref[...]
    k_raw = _dot(xn, wk_ref[...])
    for p in range(k_ref.shape[-1] // LANES):
        sl = slice(p * LANES, (p + 1) * LANES)
        k2 = _pair_rms(k_raw[:, sl], gk2)
        k_ref[:, sl] = k2
        if not has_f:
            kb_ref[:, sl] = k2.astype(BF16)
    v = _dot(xn, wv_ref[...])
    v_ref[...] = v
    if has_f:
        y = _dot(xn, wf_ref[...]) + bf_ref[...]
        lf = jnp.minimum(y, 0.0) - jnp.log1p(jnp.exp(-jnp.abs(y)))
        lane = lax.broadcasted_iota(jnp.int32, lf.shape, 1)
        lf = jnp.where(lane < n_f, lf, 0.0)
        lfp_ref[...] = lf
        lf_ref[...] = lf[:, :n_f]
    else:
        vb_ref[...] = v.astype(BF16)


def _kv_proj(x, g, wk, wv, gk2, wf=None, bf=None, n_f=0):
    n, d = x.shape
    tm = min(ROW_TILE, n)
    wk_w = wk.shape[1]
    has_f = wf is not None
    row = lambda wd: pl.BlockSpec((tm, wd), lambda i: (i, 0))
    in_specs = [row(d), _resident((1, d)), _resident(wk.shape), _resident(wv.shape),
                _resident((1, LANES))]
    args = [x, g, wk, wv, gk2]
    if has_f:
        in_specs += [_resident(wf.shape), _resident((1, LANES))]
        args += [wf, bf]
        out_specs = [row(wk_w), row(wk_w), row(n_f), row(LANES)]
        out_shape = [jax.ShapeDtypeStruct((n, wk_w), F32), jax.ShapeDtypeStruct((n, wk_w), F32),
                     jax.ShapeDtypeStruct((n, n_f), F32), jax.ShapeDtypeStruct((n, LANES), F32)]
    else:
        out_specs = [row(wk_w)] * 4
        out_shape = [jax.ShapeDtypeStruct((n, wk_w), F32), jax.ShapeDtypeStruct((n, wk_w), F32),
                     jax.ShapeDtypeStruct((n, wk_w), BF16), jax.ShapeDtypeStruct((n, wk_w), BF16)]
    return pl.pallas_call(
        functools.partial(_kv_proj_kernel, has_f=has_f, n_f=n_f),
        grid=(n // tm,),
        in_specs=in_specs, out_specs=out_specs, out_shape=out_shape,
        compiler_params=_cparams("parallel"), name="kv_proj",
    )(*args)


def _bias_lane_base(h):
    return h * LANES + (HEAD_DIM if h % 2 == 0 else 0)


def _pack_consts(n_heads):
    p = np.zeros((N_SPLIT, LANES, LANES), np.float32)
    for s in range(N_SPLIT):
        for h in range(n_heads):
            p[s, h, s * n_heads + h] = 1.0
    ones_row = np.zeros((1, LANES), np.float32)
    ones_row[0, ONES_LANE] = 1.0
    s_k = np.zeros((LANES, n_heads * LANES), np.float32)
    s_q = np.zeros((LANES, n_heads * LANES), np.float32)
    for h in range(n_heads):
        base = _bias_lane_base(h)
        for s in range(N_SPLIT):
            s_q[s * n_heads + h, base + s] = 1.0
            s_q[ONES_LANE, base + N_SPLIT + s] = 1.0
            s_k[ONES_LANE, base + s] = 1.0
            s_k[s * n_heads + h, base + N_SPLIT + s] = -1.0
    return p, ones_row, s_q, s_k


def _own_half(shape, h):
    even = _even_lanes(shape)
    return even if h % 2 == 0 else jnp.logical_not(even)


def _kv_prep_kernel(*refs, n_heads, n_main):
    n_src = 3 if n_main is None else 6
    tril_ref, p_ref, ones_ref, sk_ref, kaug_ref, vt_ref, c3_ref, carry_ref = refs[n_src:]
    tl = kaug_ref.shape[0]
    j = pl.program_id(1)

    @pl.when(j == 0)
    def _():
        carry_ref[...] = jnp.zeros_like(carry_ref)

    def tile(k_get, v_get, lf):
        tril = tril_ref[...]
        sub = tril.shape[0]
        parts = _split_bf16(lf, N_SPLIT)
        total = carry_ref[...]
        sums = []
        for r in range(tl // sub):
            cs_r = total
            for part in parts:
                cs_r = cs_r + _dot(tril, part[r * sub:(r + 1) * sub])
            total = cs_r[sub - 1:sub, :]
            sums.append(cs_r)
        cs = jnp.concatenate(sums, axis=0)
        carry_ref[...] = total

        c3 = ones_ref[...]
        for s, part in enumerate(_split_bf16(cs * LOG2E, N_SPLIT)):
            c3 = c3 + _dot(part, p_ref[s])
        c3 = c3.astype(BF16)
        c3_ref[...] = c3
        kbias = _dot(c3, sk_ref[...])
        ones_block = (lax.broadcasted_iota(jnp.int32, (FOX_VROWS - HEAD_DIM, tl), 0) == 0).astype(F32)
        for p in range(n_heads // 2):
            pair = slice(p * LANES, (p + 1) * LANES)
            k2 = k_get(pair)
            vt2 = v_get(pair).T
            for e in range(2):
                h = 2 * p + e
                slab = slice(h * LANES, (h + 1) * LANES)
                kaug_ref[:, slab] = jnp.where(_own_half(k2.shape, h), k2, kbias[:, slab]).astype(BF16)
                vt_ref[h] = jnp.concatenate([vt2[e * HEAD_DIM:(e + 1) * HEAD_DIM], ones_block], 0).astype(BF16)

    def from_refs(k_ref, v_ref, lf_ref):
        def grow(x):
            if x.shape[0] == tl:
                return x
            return jnp.concatenate([x, jnp.zeros((tl - x.shape[0], x.shape[1]), x.dtype)], 0)
        tile(lambda pair: grow(k_ref[:, pair]), lambda pair: grow(v_ref[:, pair]), grow(lf_ref[...]))

    if n_main is None:
        from_refs(*refs[:3])
    else:
        pl.when(j < n_main)(functools.partial(from_refs, *refs[:3]))
        pl.when(j >= n_main)(functools.partial(from_refs, *refs[3:6]))


def _kv_prep(k, v, lf_pad, n_heads, tail=None):
    b, l, w = k.shape
    tl = min(FOX_TK, l)
    n_main = l // tl
    n_tiles = n_main + (0 if tail is None else 1)
    p_np, ones_np, _, sk_np = _pack_consts(n_heads)
    sub = min(PREP_SUB, tl)
    tril = jnp.asarray(np.tril(np.ones((sub, sub), np.float32)), BF16)
    row = lambda wd: pl.BlockSpec((None, tl, wd), lambda i, j: (i, j, 0))
    main = lambda wd: pl.BlockSpec((None, tl, wd), lambda i, j: (i, jnp.minimum(j, n_main - 1), 0))
    srcs = [k, v, lf_pad]
    src_specs = [main(w), main(w), main(LANES)]
    if tail is not None:
        assert l % tl == 0 and tail[0].shape[1] <= tl
        srcs += list(tail)
        src_specs += [pl.BlockSpec((None,) + a.shape[1:], lambda i, j: (i, 0, 0)) for a in tail]
    return pl.pallas_call(
        functools.partial(_kv_prep_kernel, n_heads=n_heads, n_main=None if tail is None else n_main),
        grid=(b, n_tiles),
        in_specs=src_specs + [_resident((sub, sub)), _resident(p_np.shape),
                              _resident((1, LANES)), _resident(sk_np.shape)],
        out_specs=[row(n_heads * LANES),
                   pl.BlockSpec((None, n_heads, None, FOX_VROWS, tl), lambda i, j: (i, 0, j, 0, 0)),
                   row(LANES)],
        out_shape=[jax.ShapeDtypeStruct((b, n_tiles * tl, n_heads * LANES), BF16),
                   jax.ShapeDtypeStruct((b, n_heads, n_tiles, FOX_VROWS, tl), BF16),
                   jax.ShapeDtypeStruct((b, n_tiles * tl, LANES), BF16)],
        scratch_shapes=[pltpu.VMEM((1, LANES), F32)],
        compiler_params=_cparams("parallel", "arbitrary"), name="kv_prep",
    )(*srcs, tril, jnp.asarray(p_np, BF16), jnp.asarray(ones_np, F32), jnp.asarray(sk_np, BF16))


def _proj_b_kernel(x_ref, g_ref, wq_ref, wm_ref, gq_ref, c3_ref, sq_ref, qaug_ref, pm_ref,
                   *, n_heads):
    xn = _rms(x_ref[...], g_ref[...]).astype(BF16)
    pm_ref[...] = _dot(xn, wm_ref[...])
    qbias = _dot(c3_ref[...], sq_ref[...])
    gq2 = gq_ref[...]
    q_raw = _dot(xn, wq_ref[...])
    for p in range(n_heads // 2):
        pair = slice(p * LANES, (p + 1) * LANES)
        qn = _pair_rms(q_raw[:, pair], gq2) * (LOG2E * HEAD_DIM ** -0.5)
        for h in (2 * p, 2 * p + 1):
            slab = slice(h * LANES, (h + 1) * LANES)
            qaug_ref[:, slab] = jnp.where(_own_half(qn.shape, h), qn, qbias[:, slab]).astype(BF16)


def _proj_b(x, g, wq, wm, gq2, c3q, n_heads):
    n, d = x.shape
    tm = min(ROW_TILE, n)
    _, _, sq_np, _ = _pack_consts(n_heads)
    row = lambda wd: pl.BlockSpec((tm, wd), lambda i: (i, 0))
    return pl.pallas_call(
        functools.partial(_proj_b_kernel, n_heads=n_heads),
        grid=(n // tm,),
        in_specs=[row(d), _resident((1, d)), _resident(wq.shape), _resident(wm.shape),
                  _resident((1, LANES)), row(LANES), _resident(sq_np.shape)],
        out_specs=[row(n_heads * LANES), row(wm.shape[1])],
        out_shape=[jax.ShapeDtypeStruct((n, n_heads * LANES), BF16),
                   jax.ShapeDtypeStruct((n, wm.shape[1]), F32)],
        compiler_params=_cparams("parallel"), name="proj_b",
    )(x, g, wq, wm, gq2, c3q, jnp.asarray(sq_np, BF16))


def _fox_kernel(q_ref, k_ref, vt_ref, o_ref, acc_ref, s_ref, *, t0, tq, tk, n_q):
    cw = min(FOX_COLS, tq)
    kc = min(FOX_KEYS, tk)
    acc_ref[...] = jnp.zeros_like(acc_ref)

    def attend(qi):
        first_tile = t0 // tk + qi * (tq // tk)
        blocks = [(e, c) for e in range(2) for c in range(tq // cw)]
        items = [(j, e, c, tk) for j in range(first_tile) for e, c in blocks]
        for d in range(max(1, tq // tk)):
            for e, c in blocks:
                seen = min(max((c + 1) * cw - d * tk, 0), tk) if cw == FOX_COLS else tk
                if seen > 0:
                    items.append((first_tile + d, e, c, seen))

        def score_chunk(pos, r, m_run):
            j, e, c, _ = items[pos]
            first_key = j * tk + r * kc
            s = _dot_nt(k_ref[first_key:first_key + kc, e * LANES:(e + 1) * LANES],
                        q_ref[c * cw:(c + 1) * cw, e * LANES:(e + 1) * LANES])
            first_query = t0 + qi * tq + c * cw
            if first_key + kc - 1 > first_query:
                key = first_key + lax.broadcasted_iota(jnp.int32, s.shape, 0)
                qry = first_query + lax.broadcasted_iota(jnp.int32, s.shape, 1)
                s = jnp.where(key <= qry, s, NEG)
            s_ref[pos % 2, r * kc:(r + 1) * kc, :] = s
            return jnp.maximum(m_run, jnp.max(s, axis=0, keepdims=True))

        ms = {blk: jnp.full((1, cw), NEG, F32) for blk in blocks}
        m_next = ms[items[0][1:3]]
        for r in range(items[0][3] // kc):
            m_next = score_chunk(0, r, m_next)
        for pos, (j, e, c, seen) in enumerate(items):
            m_new = m_next
            n_nxt = 0
            if pos + 1 < len(items):
                m_next = ms[items[pos + 1][1:3]] if items[pos + 1][1:3] != (e, c) else m_new
                n_nxt = items[pos + 1][3] // kc
            pv = None
            for r in range(max(seen // kc, n_nxt)):
                if r < n_nxt:
                    m_next = score_chunk(pos + 1, r, m_next)
                if r < seen // kc:
                    p = jnp.exp2(s_ref[pos % 2, r * kc:(r + 1) * kc, :] - m_new).astype(BF16)
                    pv_r = _dot(vt_ref[e, j][:, r * kc:(r + 1) * kc], p)
                    pv = pv_r if pv is None else pv + pv_r
            cols = slice(c * cw, (c + 1) * cw)
            acc_ref[e, :, cols] = jnp.exp2(ms[(e, c)] - m_new) * acc_ref[e, :, cols] + pv
            ms[(e, c)] = m_new

    if n_q == 1:
        attend(0)
    else:
        for qi in range(n_q):
            pl.when(pl.program_id(2) == qi)(functools.partial(attend, qi))

    halves = [acc_ref[e, :HEAD_DIM, :] * (1.0 / acc_ref[e, HEAD_DIM:HEAD_DIM + 1, :]) for e in range(2)]
    o_ref[...] = jnp.concatenate(halves, axis=0).T.astype(o_ref.dtype)


def _fox(q_aug, k_aug, vt, t0):
    b, t, wq = q_aug.shape
    l = k_aug.shape[1]
    n_pairs = wq // (2 * LANES)
    tq = min(FOX_TQ, t)
    tk = vt.shape[-1]
    assert t0 % tk == 0 and (tq % tk == 0 or t == tq), (t0, tq, tk, t)
    return pl.pallas_call(
        functools.partial(_fox_kernel, t0=t0, tq=tq, tk=tk, n_q=t // tq),
        grid=(b, n_pairs, t // tq),
        in_specs=[pl.BlockSpec((None, tq, 2 * LANES), lambda bi, p, i: (bi, i, p)),
                  pl.BlockSpec((None, l, 2 * LANES), lambda bi, p, i: (bi, 0, p)),
                  pl.BlockSpec((None, 2, l // tk, FOX_VROWS, tk), lambda bi, p, i: (bi, p, 0, 0, 0))],
        out_specs=pl.BlockSpec((None, tq, LANES), lambda bi, p, i: (bi, i, p)),
        out_shape=jax.ShapeDtypeStruct((b, t, n_pairs * LANES), BF16),
        scratch_shapes=[pltpu.VMEM((2, FOX_VROWS, tq), F32), pltpu.VMEM((2, tk, min(FOX_COLS, tq)), F32)],
        compiler_params=_cparams("parallel", "parallel", "arbitrary"), name="fox",
    )(q_aug, k_aug, vt)


def _tile2(g):
    return jnp.concatenate([g, g]).reshape(1, LANES).astype(F32)


def _trunk(x, mem_k, mem_v, hg_states, past, prm):
    b, t, d = x.shape
    depth = prm["norm_mix"].shape[0]
    n_a = prm["w_in_a"].shape[0]
    mem_width = mem_k[0].shape[-1]
    main_width = prm["w_in_b"].shape[2] - mem_width
    n_fox = main_width // HEAD_DIM
    t_pad = -(-t // CHUNK) * CHUNK
    h = x
    new_states = []
    new_kv = None
    for l in range(depth):
        g_mix = prm["norm_mix"][l].reshape(1, d)
        if l < n_a:
            widths = (main_width,) * 4 + (mem_width,)
            pq, pf, pi, pg, pm = _proj_split(h.reshape(b * t, d), g_mix, prm["w_in_a"][l], widths)
            chunked = [jnp.pad(a.reshape(b, t, main_width), ((0, 0), (0, t_pad - t), (0, 0)))
                       for a in (pq, pf, pi, pg)]
            o_main, s_new = _hgrn(*chunked, prm["lb_logits"], prm["hg_gnorm"][l], hg_states[l],
                                  layer=l, t_valid=t)
            o_main = o_main[:, :t]
            new_states.append(s_new)
        else:
            j = l - n_a
            q_aug, pm = _proj_b(h.reshape(b * t, d), g_mix, prm["w_q_b"][j], prm["w_m_b"][j],
                                _tile2(prm["fox_gq"][j]), c3_q.reshape(b * t, LANES), n_fox)
            o_main = _fox(q_aug.reshape(b, t, n_fox * LANES), k_aug, v_t, t0)
        h = _post(h, o_main, pm.reshape(b, t, -1), mem_k[l], mem_v[l], _tile2(prm["mem_gq"][l]),
                  prm["w_out"][l], prm["norm_ffn"][l].reshape(1, d), prm["w_ffn_up"][l],
                  prm["w_ffn_down"][l])
        if l == n_a - 1:
            k_new, v_new, lf_new, lf_pad = _kv_proj(
                h.reshape(b * t, d), prm["norm_kv"].reshape(1, d), prm["w_k"], prm["w_v"],
                _tile2(prm["fox_gk"]), prm["w_f"], prm["b_f"], n_fox)
            k_new = k_new.reshape(b, t, main_width)
            v_new = v_new.reshape(b, t, main_width)
            lf_pad = lf_pad.reshape(b, t, LANES)
            new_kv = (k_new.reshape(b, t, n_fox, HEAD_DIM), v_new.reshape(b, t, n_fox, HEAD_DIM),
                      lf_new.reshape(b, t, n_fox))
            if past is None:
                t0 = 0
                k_aug, v_t, c3 = _kv_prep(k_new, v_new, lf_pad, n_fox)
            else:
                t0 = past[0].shape[1]
                lf_past = jnp.pad(past[2].astype(F32), ((0, 0), (0, 0), (0, LANES - n_fox)))
                k_aug, v_t, c3 = _kv_prep(past[0].reshape(b, t0, main_width), past[1].reshape(b, t0, main_width),
                                          lf_past, n_fox, tail=(k_new, v_new, lf_pad))
            c3_q = c3[:, t0:t0 + t]
    return h, new_states, new_kv


def kernel(x_prompt, x_sample, mem_prompt, state_hgrn_0, state_hgrn_1, cache_fox_k, cache_fox_v, cache_fox_logf, cache_mem_k, cache_mem_v, norm_mix, w_in_a, lb_logits, hg_gnorm, w_in_b, fox_gq, norm_kv, w_kv, b_f, fox_gk, norm_mem, w_mem_kv, mem_gq, mem_gk, w_out, norm_ffn, w_ffn_up, w_ffn_down):
    depth, d = norm_mix.shape
    mem_width = cache_mem_k.shape[-1] * cache_mem_k.shape[-2]
    main_width = w_in_b.shape[2] - mem_width
    n_fox = b_f.shape[0]
    bsz, n_mem, _ = mem_prompt.shape

    w_f = jnp.pad(w_kv[:, 2 * main_width:], ((0, 0), (0, LANES - n_fox)))
    prm = {
        "norm_mix": norm_mix, "w_in_a": w_in_a.astype(BF16), "lb_logits": lb_logits.astype(F32),
        "hg_gnorm": hg_gnorm, "w_in_b": w_in_b,
        "w_q_b": w_in_b[:, :, :main_width].astype(BF16), "w_m_b": w_in_b[:, :, main_width:].astype(BF16),
        "fox_gq": fox_gq, "norm_kv": norm_kv,
        "w_k": w_kv[:, :main_width].astype(BF16), "w_v": w_kv[:, main_width:2 * main_width].astype(BF16),
        "w_f": w_f.astype(BF16), "b_f": jnp.pad(b_f, (0, LANES - n_fox)).reshape(1, LANES).astype(F32),
        "fox_gk": fox_gk, "mem_gq": mem_gq, "w_out": w_out.astype(BF16), "norm_ffn": norm_ffn,
        "w_ffn_up": w_ffn_up.astype(BF16), "w_ffn_down": w_ffn_down.astype(BF16),
    }

    mem_rows = mem_prompt.reshape(bsz * n_mem, d)
    mk, mv, mkb, mvb = [], [], [], []
    for l in range(depth):
        wkv = w_mem_kv[l].astype(BF16)
        k_l, v_l, kb_l, vb_l = _kv_proj(mem_rows, norm_mem[l].reshape(1, d), wkv[:, :mem_width],
                                        wkv[:, mem_width:], _tile2(mem_gk[l]))
        mk.append(k_l)
        mv.append(v_l)
        mkb.append(kb_l.reshape(bsz, n_mem, mem_width))
        mvb.append(vb_l.reshape(bsz, n_mem, mem_width))
    mem_shape = (depth, bsz, n_mem) + cache_mem_k.shape[-2:]
    p_mem_k = jnp.stack(mk).reshape(mem_shape)
    p_mem_v = jnp.stack(mv).reshape(mem_shape)
    s_zero = jnp.zeros((bsz,) + state_hgrn_0.shape[1:], F32)
    y_prompt, p_states, p_kv = _trunk(x_prompt, mkb, mvb, [s_zero] * w_in_a.shape[0], None, prm)

    dec_b = x_sample.shape[0]
    cmk = cache_mem_k.reshape(depth, dec_b, n_mem, mem_width).astype(BF16)
    cmv = cache_mem_v.reshape(depth, dec_b, n_mem, mem_width).astype(BF16)
    y_sample, s_states, s_kv = _trunk(x_sample, cmk, cmv, [state_hgrn_0, state_hgrn_1],
                                      (cache_fox_k, cache_fox_v, cache_fox_logf), prm)
    return (y_prompt, y_sample, p_states[0], p_states[1], p_kv[0], p_kv[1], p_kv[2], p_mem_k, p_mem_v,
            s_states[0], s_states[1], s_kv[0], s_kv[1], s_kv[2])
```

```python
import functools

import numpy as np
import jax
import jax.numpy as jnp
from jax import lax
from jax.experimental import pallas as pl
from jax.experimental.pallas import tpu as pltpu

F32 = jnp.float32
BF16 = jnp.bfloat16

EPS = 1e-6
K_MAX = 0.999999
NEG = -1e30
LOG2E = 1.4426950408889634

LANES = 128
HEAD_DIM = 64
CHUNK = 64
HG_DK = 128
FF_TILE = 256
ROW_TILE = 512
FOX_TQ = 2048
FOX_TK = 1024
FOX_COLS = 512
FOX_KEYS = 256
FOX_VROWS = HEAD_DIM + 16
PREP_SUB = 256
N_SPLIT = 3
ONES_LANE = 36
VMEM_LIMIT_BYTES = 56 * 1024 * 1024


def _cparams(*sem):
    return pltpu.CompilerParams(dimension_semantics=sem, vmem_limit_bytes=VMEM_LIMIT_BYTES)


def _resident(shape):
    nd = len(shape)
    return pl.BlockSpec(shape, lambda *_: (0,) * nd, pipeline_mode=pl.Buffered(1))


def _dot(a, b):
    return jnp.dot(a, b, preferred_element_type=F32)


def _dot_nt(a, b):
    return lax.dot_general(a, b, (((1,), (1,)), ((), ())), preferred_element_type=F32)


def _dot_tn(a, b):
    return lax.dot_general(a, b, (((0,), (0,)), ((), ())), preferred_element_type=F32)


def _split_bf16(x, n):
    parts = []
    r = x
    for _ in range(n):
        p = r.astype(BF16)
        parts.append(p)
        r = r - p.astype(F32)
    return parts


def _rms(x, g):
    ms = jnp.mean(x * x, axis=-1, keepdims=True)
    return x * lax.rsqrt(ms + EPS) * g


def _sigmoid(x):
    return 0.5 + 0.5 * jnp.tanh(0.5 * x)


def _silu(x):
    hx = 0.5 * x
    return hx + hx * jnp.tanh(hx)


def _even_lanes(shape):
    return lax.broadcasted_iota(jnp.int32, shape, len(shape) - 1) < HEAD_DIM


def _pair_rms(x2, g2):
    even = _even_lanes(x2.shape)
    sq = x2 * x2
    se = jnp.sum(jnp.where(even, sq, 0.0), axis=-1, keepdims=True)
    so = jnp.sum(jnp.where(even, 0.0, sq), axis=-1, keepdims=True)
    ms = jnp.where(even, se, so) * (1.0 / HEAD_DIM)
    return x2 * lax.rsqrt(ms + EPS) * g2


def _proj_split_kernel(x_ref, g_ref, w_ref, *out_refs, widths):
    xn = _rms(x_ref[...], g_ref[...]).astype(BF16)
    off = 0
    for o_ref, wd in zip(out_refs, widths):
        o_ref[...] = _dot(xn, w_ref[:, off:off + wd]).astype(o_ref.dtype)
        off += wd


def _proj_split(x, g, w, widths):
    n, d = x.shape
    tm = min(ROW_TILE, n)
    return pl.pallas_call(
        functools.partial(_proj_split_kernel, widths=widths),
        grid=(n // tm,),
        in_specs=[pl.BlockSpec((tm, d), lambda i: (i, 0)), _resident((1, d)), _resident(w.shape)],
        out_specs=[pl.BlockSpec((tm, wd), lambda i: (i, 0)) for wd in widths],
        out_shape=[jax.ShapeDtypeStruct((n, wd), F32) for wd in widths],
        compiler_params=_cparams("parallel"), name="proj_a",
    )(x, g, w)


HG_MXU_LEVELS = (2, 4)
HG_ROW_LEVELS = (8, 16, 32)
HG_BLOCK = 1024


def _hgrn_consts():
    c = CHUNK
    t = np.arange(c)
    m = []
    for h in HG_MXU_LEVELS:
        blk = t // (2 * h)
        second = (t // h) % 2 == 1
        boundary = blk * 2 * h + h
        mh = np.zeros((c, c), np.float32)
        for r in range(c):
            if second[r]:
                mh[r, boundary[r]:r + 1] = 1.0
            else:
                mh[r, r + 1:boundary[r]] = 1.0
        m.append(mh)
    m.append(np.tril(np.ones((c, c), np.float32)))
    masks = []
    for h in (1,) + HG_MXU_LEVELS + HG_ROW_LEVELS:
        blk = t // (2 * h)
        second = (t // h) % 2 == 1
        masks.append((blk[:, None] == blk[None, :]) & second[:, None] & ~second[None, :])
    masks.append(np.eye(c, dtype=bool))
    return np.concatenate(m, 0), np.stack(masks).astype(np.float32)


def _row_level_exponent(b, h):
    pieces = []
    for r0 in range(0, CHUNK, 2 * h):
        rho = b[r0 + h - 1:r0 + h, :]
        pieces.append(rho - b[r0:r0 + h, :])
        pieces.append(b[r0 + h:r0 + 2 * h, :] - rho)
    return jnp.concatenate(pieces, axis=0)


def _hgrn_kernel(pq_ref, pf_ref, pi_ref, pg_ref, lbl_ref, gn_ref, s0_ref, m_ref, lm_ref,
                 o_ref, sfin_ref, st_ref, *, n_heads, t_valid, layer):
    blk_idx = pl.program_id(1)
    tb = pq_ref.shape[0]
    n_mxu = len(HG_MXU_LEVELS)

    @pl.when(blk_idx == 0)
    def _():
        for h in range(n_heads):
            st_ref[h] = s0_ref[h].T

    rows = [lbl_ref[i:i + 1, :] for i in range(lbl_ref.shape[0])]
    mx = functools.reduce(jnp.maximum, rows)
    es = [jnp.exp(r - mx) for r in rows]
    tot = functools.reduce(lambda a, b: a + b, es)
    ps = [e / tot for e in es]
    cum = ps[0]
    for i in range(1, layer + 1):
        cum = cum + ps[i]
    lb = cum - ps[0]

    m_all = m_ref[...]
    gn = gn_ref[...]
    heads = [slice(h * HG_DK, (h + 1) * HG_DK) for h in range(n_heads)]
    odd_row = lax.broadcasted_iota(jnp.int32, (CHUNK, pq_ref.shape[1]), 0) % 2 == 1
    n_lv = lm_ref.shape[0] - 1
    level_masks = [lm_ref[i] > 0.5 for i in range(n_lv)]
    on_diag = lm_ref[n_lv] > 0.5

    per_chunk = []
    for c in range(tb // CHUNK):
        rs = slice(c * CHUNK, (c + 1) * CHUNK)
        q = _silu(pq_ref[rs, :])
        kk = jnp.minimum((1.0 - lb) * _sigmoid(-pf_ref[rs, :]), K_MAX)
        if t_valid is not None:
            row = blk_idx * tb + c * CHUNK + lax.broadcasted_iota(jnp.int32, kk.shape, 0)
            kk = jnp.where(row < t_valid, kk, 0.0)
        f = 1.0 - kk
        g = jnp.log2(f)
        v_b = pi_ref[rs, :].astype(BF16)

        g_hi, g_lo = _split_bf16(g, 2)
        d_all = _dot(m_all, g_hi) + _dot(m_all, g_lo)
        b = d_all[n_mxu * CHUNK:]
        e_levels = [jnp.where(odd_row, f, 1.0)]
        e_levels += [jnp.exp2(d_all[i * CHUNK:(i + 1) * CHUNK]) for i in range(n_mxu)]
        e_levels += [jnp.exp2(_row_level_exponent(b, h)) for h in HG_ROW_LEVELS]

        q_b = q.astype(BF16)
        k_b = kk.astype(BF16)
        a = [jnp.where(on_diag, _dot_nt(q_b[:, sl], k_b[:, sl]), 0.0) for sl in heads]
        for in_level, el in zip(level_masks, e_levels):
            el_b = el.astype(BF16)
            q_l = q_b * el_b
            k_l = k_b * el_b
            a = [jnp.where(in_level, _dot_nt(q_l[:, sl], k_l[:, sl]), a_h) for a_h, sl in zip(a, heads)]

        b_last = b[CHUNK - 1:CHUNK, :]
        per_chunk.append(dict(
            a=[a_h.astype(BF16) for a_h in a], v=v_b, q_e=(q * jnp.exp2(b)).astype(BF16),
            k_e=(kk * jnp.exp2(b_last - b)).astype(BF16), decay=jnp.exp2(b_last)))

    for c, pc in enumerate(per_chunk):
        rs = slice(c * CHUNK, (c + 1) * CHUNK)
        for h, sl in enumerate(heads):
            st = st_ref[h]
            o = _dot_nt(pc["q_e"][:, sl], st.astype(BF16)) + _dot(pc["a"][h], pc["v"][:, sl])
            st_ref[h] = pc["decay"][:, sl] * st + _dot_tn(pc["v"][:, sl], pc["k_e"][:, sl])
            o_ref[rs, sl] = (_rms(o, gn) * _silu(pg_ref[rs, sl])).astype(o_ref.dtype)

    @pl.when(blk_idx == pl.num_programs(1) - 1)
    def _():
        for h in range(n_heads):
            sfin_ref[h] = st_ref[h].T


def _hgrn(pq, pf, pi, pg, lb_logits, gnorm, s0, layer, t_valid):
    b, t, w = pq.shape
    n_heads = w // HG_DK
    tb = HG_BLOCK if t % HG_BLOCK == 0 else CHUNK
    m_np, masks_np = _hgrn_consts()
    m_all = jnp.asarray(m_np, BF16)
    masks = jnp.asarray(masks_np, F32)
    tok = pl.BlockSpec((None, tb, w), lambda i, c: (i, c, 0))
    st_spec = pl.BlockSpec((None, n_heads, HG_DK, HG_DK), lambda i, c: (i, 0, 0, 0))
    return pl.pallas_call(
        functools.partial(_hgrn_kernel, n_heads=n_heads, layer=layer,
                          t_valid=None if t_valid == t else t_valid),
        grid=(b, t // tb),
        in_specs=[tok, tok, tok, tok, _resident(lb_logits.shape), _resident((1, HG_DK)), st_spec,
                  _resident(m_all.shape), _resident(masks.shape)],
        out_specs=[tok, st_spec],
        out_shape=[jax.ShapeDtypeStruct((b, t, w), BF16),
                   jax.ShapeDtypeStruct((b, n_heads, HG_DK, HG_DK), F32)],
        scratch_shapes=[pltpu.VMEM((n_heads, HG_DK, HG_DK), F32)],
        compiler_params=_cparams("parallel", "arbitrary"), name="hgrn",
    )(pq, pf, pi, pg, lb_logits, gnorm.reshape(1, HG_DK), s0, m_all, masks)


def _post_kernel(h_ref, om_ref, pm_ref, mk_ref, mv_ref, gq_ref, wo_ref, gf_ref, wu_ref, wd_ref,
                 out_ref, *, main_width, d_ff):
    nb, tm, d = h_ref.shape
    x = h_ref[...].reshape(nb * tm, d)
    attn = _dot(om_ref[...].reshape(nb * tm, main_width), wo_ref[:main_width, :])
    gq2 = gq_ref[...]
    n_pairs = pm_ref.shape[-1] // LANES
    o_mem = []
    for p in range(n_pairs):
        sl = slice(p * LANES, (p + 1) * LANES)
        per_batch = []
        for bi in range(nb):
            qn = _pair_rms(pm_ref[bi, :, sl], gq2) * (HEAD_DIM ** -0.5)
            even = _even_lanes(qn.shape)
            k2 = mk_ref[bi, :, sl]
            v2 = mv_ref[bi, :, sl]
            halves = []
            for own in (even, jnp.logical_not(even)):
                s = _dot_nt(jnp.where(own, qn, 0.0).astype(BF16), k2)
                e = jnp.exp(s - jnp.max(s, axis=-1, keepdims=True))
                l = jnp.sum(e, axis=-1, keepdims=True)
                halves.append(_dot(e.astype(BF16), v2) * (1.0 / l))
            per_batch.append(jnp.where(even, halves[0], halves[1]))
        o2 = per_batch[0] if nb == 1 else jnp.concatenate(per_batch, axis=0)
        o_mem.append(o2.astype(BF16))
    attn = attn + _dot(jnp.concatenate(o_mem, axis=1), wo_ref[main_width:, :])
    h1 = x + attn

    xn = _rms(h1, gf_ref[...]).astype(BF16)
    acc = jnp.zeros_like(h1)
    for f in range(d_ff // FF_TILE):
        gate = _dot(xn, wu_ref[:, f * FF_TILE:(f + 1) * FF_TILE])
        up = _dot(xn, wu_ref[:, d_ff + f * FF_TILE:d_ff + (f + 1) * FF_TILE])
        act = (_silu(gate) * up).astype(BF16)
        acc = acc + _dot(act, wd_ref[f * FF_TILE:(f + 1) * FF_TILE, :])
    out_ref[...] = (h1 + acc).reshape(nb, tm, d)


def _post(h, o_main, pm, mem_k, mem_v, gq2, w_out, g_ffn, w_up, w_down):
    b, t, d = h.shape
    tm = min(ROW_TILE, t)
    nb = b if b * t <= ROW_TILE else 1
    main_width = o_main.shape[-1]
    mem_width = pm.shape[-1]
    n_mem = mem_k.shape[1]
    d_ff = w_down.shape[0]
    row = lambda wd: pl.BlockSpec((nb, tm, wd), lambda i, r: (i, r, 0))
    mem = pl.BlockSpec((nb, n_mem, mem_width), lambda i, r: (i, 0, 0))
    return pl.pallas_call(
        functools.partial(_post_kernel, main_width=main_width, d_ff=d_ff),
        grid=(b // nb, t // tm),
        in_specs=[row(d), row(main_width), row(mem_width), mem, mem, _resident((1, LANES)),
                  _resident(w_out.shape), _resident((1, d)), _resident(w_up.shape),
                  _resident(w_down.shape)],
        out_specs=row(d),
        out_shape=jax.ShapeDtypeStruct((b, t, d), F32),
        compiler_params=_cparams("parallel", "parallel"), name="post",
    )(h, o_main, pm, mem_k, mem_v, gq2, w_out, g_ffn, w_up, w_down)


def _kv_proj_kernel(*refs, has_f, n_f):
    if has_f:
        (x_ref, g_ref, wk_ref, wv_ref, gk_ref, wf_ref, bf_ref,
         k_ref, v_ref, lf_ref, lfp_ref) = refs
    else:
        x_ref, g_ref, wk_ref, wv_ref, gk_ref, k_ref, v_ref, kb_ref, vb_ref = refs
    xn = _rms(x_ref[...], g_ref[...]).astype(BF16)
    gk2 = gk_ref[...]
    k_raw = _dot(xn, wk_ref[...])
    for p in range(k_ref.shape[-1] // LANES):
        sl = slice(p * LANES, (p + 1) * LANES)
        k2 = _pair_rms(k_raw[:, sl], gk2)
        k_ref[:, sl] = k2
        if not has_f:
            kb_ref[:, sl] = k2.astype(BF16)
    v = _dot(xn, wv_ref[...])
    v_ref[...] = v
    if has_f:
        y = _dot(xn, wf_ref[...]) + bf_ref[...]
        lf = jnp.minimum(y, 0.0) - jnp.log1p(jnp.exp(-jnp.abs(y)))
        lane = lax.broadcasted_iota(jnp.int32, lf.shape, 1)
        lf = jnp.where(lane < n_f, lf, 0.0)
        lfp_ref[...] = lf
        lf_ref[...] = lf[:, :n_f]
    else:
        vb_ref[...] = v.astype(BF16)


def _kv_proj(x, g, wk, wv, gk2, wf=None, bf=None, n_f=0):
    n, d = x.shape
    tm = min(ROW_TILE, n)
    wk_w = wk.shape[1]
    has_f = wf is not None
    row = lambda wd: pl.BlockSpec((tm, wd), lambda i: (i, 0))
    in_specs = [row(d), _resident((1, d)), _resident(wk.shape), _resident(wv.shape),
                _resident((1, LANES))]
    args = [x, g, wk, wv, gk2]
    if has_f:
        in_specs += [_resident(wf.shape), _resident((1, LANES))]
        args += [wf, bf]
        out_specs = [row(wk_w), row(wk_w), row(n_f), row(LANES)]
        out_shape = [jax.ShapeDtypeStruct((n, wk_w), F32), jax.ShapeDtypeStruct((n, wk_w), F32),
                     jax.ShapeDtypeStruct((n, n_f), F32), jax.ShapeDtypeStruct((n, LANES), F32)]
    else:
        out_specs = [row(wk_w)] * 4
        out_shape = [jax.ShapeDtypeStruct((n, wk_w), F32), jax.ShapeDtypeStruct((n, wk_w), F32),
                     jax.ShapeDtypeStruct((n, wk_w), BF16), jax.ShapeDtypeStruct((n, wk_w), BF16)]
    return pl.pallas_call(
        functools.partial(_kv_proj_kernel, has_f=has_f, n_f=n_f),
        grid=(n // tm,),
        in_specs=in_specs, out_specs=out_specs, out_shape=out_shape,
        compiler_params=_cparams("parallel"), name="kv_proj",
    )(*args)


def _bias_lane_base(h):
    return h * LANES + (HEAD_DIM if h % 2 == 0 else 0)


def _pack_consts(n_heads):
    p = np.zeros((N_SPLIT, LANES, LANES), np.float32)
    for s in range(N_SPLIT):
        for h in range(n_heads):
            p[s, h, s * n_heads + h] = 1.0
    ones_row = np.zeros((1, LANES), np.float32)
    ones_row[0, ONES_LANE] = 1.0
    s_k = np.zeros((LANES, n_heads * LANES), np.float32)
    s_q = np.zeros((LANES, n_heads * LANES), np.float32)
    for h in range(n_heads):
        base = _bias_lane_base(h)
        for s in range(N_SPLIT):
            s_q[s * n_heads + h, base + s] = 1.0
            s_q[ONES_LANE, base + N_SPLIT + s] = 1.0
            s_k[ONES_LANE, base + s] = 1.0
            s_k[s * n_heads + h, base + N_SPLIT + s] = -1.0
    return p, ones_row, s_q, s_k


def _own_half(shape, h):
    even = _even_lanes(shape)
    return even if h % 2 == 0 else jnp.logical_not(even)


def _kv_prep_kernel(*refs, n_heads, n_main):
    n_src = 3 if n_main is None else 6
    tril_ref, p_ref, ones_ref, sk_ref, kaug_ref, vt_ref, c3_ref, carry_ref = refs[n_src:]
    tl = kaug_ref.shape[0]
    j = pl.program_id(1)

    @pl.when(j == 0)
    def _():
        carry_ref[...] = jnp.zeros_like(carry_ref)

    def tile(k_get, v_get, lf):
        tril = tril_ref[...]
        sub = tril.shape[0]
        parts = _split_bf16(lf, N_SPLIT)
        total = carry_ref[...]
        sums = []
        for r in range(tl // sub):
            cs_r = total
            for part in parts:
                cs_r = cs_r + _dot(tril, part[r * sub:(r + 1) * sub])
            total = cs_r[sub - 1:sub, :]
            sums.append(cs_r)
        cs = jnp.concatenate(sums, axis=0)
        carry_ref[...] = total

        c3 = ones_ref[...]
        for s, part in enumerate(_split_bf16(cs * LOG2E, N_SPLIT)):
            c3 = c3 + _dot(part, p_ref[s])
        c3 = c3.astype(BF16)
        c3_ref[...] = c3
        kbias = _dot(c3, sk_ref[...])
        ones_block = (lax.broadcasted_iota(jnp.int32, (FOX_VROWS - HEAD_DIM, tl), 0) == 0).astype(F32)
        for p in range(n_heads // 2):
            pair = slice(p * LANES, (p + 1) * LANES)
            k2 = k_get(pair)
            vt2 = v_get(pair).T
            for e in range(2):
                h = 2 * p + e
                slab = slice(h * LANES, (h + 1) * LANES)
                kaug_ref[:, slab] = jnp.where(_own_half(k2.shape, h), k2, kbias[:, slab]).astype(BF16)
                vt_ref[h] = jnp.concatenate([vt2[e * HEAD_DIM:(e + 1) * HEAD_DIM], ones_block], 0).astype(BF16)

    def from_refs(k_ref, v_ref, lf_ref):
        def grow(x):
            if x.shape[0] == tl:
                return x
            return jnp.concatenate([x, jnp.zeros((tl - x.shape[0], x.shape[1]), x.dtype)], 0)
        tile(lambda pair: grow(k_ref[:, pair]), lambda pair: grow(v_ref[:, pair]), grow(lf_ref[...]))

    if n_main is None:
        from_refs(*refs[:3])
    else:
        pl.when(j < n_main)(functools.partial(from_refs, *refs[:3]))
        pl.when(j >= n_main)(functools.partial(from_refs, *refs[3:6]))


def _kv_prep(k, v, lf_pad, n_heads, tail=None):
    b, l, w = k.shape
    tl = min(FOX_TK, l)
    n_main = l // tl
    n_tiles = n_main + (0 if tail is None else 1)
    p_np, ones_np, _, sk_np = _pack_consts(n_heads)
    sub = min(PREP_SUB, tl)
    tril = jnp.asarray(np.tril(np.ones((sub, sub), np.float32)), BF16)
    row = lambda wd: pl.BlockSpec((None, tl, wd), lambda i, j: (i, j, 0))
    main = lambda wd: pl.BlockSpec((None, tl, wd), lambda i, j: (i, jnp.minimum(j, n_main - 1), 0))
    srcs = [k, v, lf_pad]
    src_specs = [main(w), main(w), main(LANES)]
    if tail is not None:
        assert l % tl == 0 and tail[0].shape[1] <= tl
        srcs += list(tail)
        src_specs += [pl.BlockSpec((None,) + a.shape[1:], lambda i, j: (i, 0, 0)) for a in tail]
    return pl.pallas_call(
        functools.partial(_kv_prep_kernel, n_heads=n_heads, n_main=None if tail is None else n_main),
        grid=(b, n_tiles),
        in_specs=src_specs + [_resident((sub, sub)), _resident(p_np.shape),
                              _resident((1, LANES)), _resident(sk_np.shape)],
        out_specs=[row(n_heads * LANES),
                   pl.BlockSpec((None, n_heads, None, FOX_VROWS, tl), lambda i, j: (i, 0, j, 0, 0)),
                   row(LANES)],
        out_shape=[jax.ShapeDtypeStruct((b, n_tiles * tl, n_heads * LANES), BF16),
                   jax.ShapeDtypeStruct((b, n_heads, n_tiles, FOX_VROWS, tl), BF16),
                   jax.ShapeDtypeStruct((b, n_tiles * tl, LANES), BF16)],
        scratch_shapes=[pltpu.VMEM((1, LANES), F32)],
        compiler_params=_cparams("parallel", "arbitrary"), name="kv_prep",
    )(*srcs, tril, jnp.asarray(p_np, BF16), jnp.asarray(ones_np, F32), jnp.asarray(sk_np, BF16))


def _proj_b_kernel(x_ref, g_ref, wq_ref, wm_ref, gq_ref, c3_ref, sq_ref, qaug_ref, pm_ref,
                   *, n_heads):
    xn = _rms(x_ref[...], g_ref[...]).astype(BF16)
    pm_ref[...] = _dot(xn, wm_ref[...])
    qbias = _dot(c3_ref[...], sq_ref[...])
    gq2 = gq_ref[...]
    q_raw = _dot(xn, wq_ref[...])
    for p in range(n_heads // 2):
        pair = slice(p * LANES, (p + 1) * LANES)
        qn = _pair_rms(q_raw[:, pair], gq2) * (LOG2E * HEAD_DIM ** -0.5)
        for h in (2 * p, 2 * p + 1):
            slab = slice(h * LANES, (h + 1) * LANES)
            qaug_ref[:, slab] = jnp.where(_own_half(qn.shape, h), qn, qbias[:, slab]).astype(BF16)


def _proj_b(x, g, wq, wm, gq2, c3q, n_heads):
    n, d = x.shape
    tm = min(ROW_TILE, n)
    _, _, sq_np, _ = _pack_consts(n_heads)
    row = lambda wd: pl.BlockSpec((tm, wd), lambda i: (i, 0))
    return pl.pallas_call(
        functools.partial(_proj_b_kernel, n_heads=n_heads),
        grid=(n // tm,),
        in_specs=[row(d), _resident((1, d)), _resident(wq.shape), _resident(wm.shape),
                  _resident((1, LANES)), row(LANES), _resident(sq_np.shape)],
        out_specs=[row(n_heads * LANES), row(wm.shape[1])],
        out_shape=[jax.ShapeDtypeStruct((n, n_heads * LANES), BF16),
                   jax.ShapeDtypeStruct((n, wm.shape[1]), F32)],
        compiler_params=_cparams("parallel"), name="proj_b",
    )(x, g, wq, wm, gq2, c3q, jnp.asarray(sq_np, BF16))


def _fox_kernel(q_ref, k_ref, vt_ref, o_ref, acc_ref, s_ref, *, t0, tq, tk, n_q):
    cw = min(FOX_COLS, tq)
    kc = min(FOX_KEYS, tk)
    acc_ref[...] = jnp.zeros_like(acc_ref)

    def attend(qi):
        first_tile = t0 // tk + qi * (tq // tk)
        blocks = [(e, c) for e in range(2) for c in range(tq // cw)]
        items = [(j, e, c, tk) for j in range(first_tile) for e, c in blocks]
        for d in range(max(1, tq // tk)):
            for e, c in blocks:
                seen = min(max((c + 1) * cw - d * tk, 0), tk) if cw == FOX_COLS else tk
                if seen > 0:
                    items.append((first_tile + d, e, c, seen))

        def score_chunk(pos, r, m_run):
            j, e, c, _ = items[pos]
            first_key = j * tk + r * kc
            s = _dot_nt(k_ref[first_key:first_key + kc, e * LANES:(e + 1) * LANES],
                        q_ref[c * cw:(c + 1) * cw, e * LANES:(e + 1) * LANES])
            first_query = t0 + qi * tq + c * cw
            if first_key + kc - 1 > first_query:
                key = first_key + lax.broadcasted_iota(jnp.int32, s.shape, 0)
                qry = first_query + lax.broadcasted_iota(jnp.int32, s.shape, 1)
                s = jnp.where(key <= qry, s, NEG)
            s_ref[pos % 2, r * kc:(r + 1) * kc, :] = s
            return jnp.maximum(m_run, jnp.max(s, axis=0, keepdims=True))

        ms = {blk: jnp.full((1, cw), NEG, F32) for blk in blocks}
        m_next = ms[items[0][1:3]]
        for r in range(items[0][3] // kc):
            m_next = score_chunk(0, r, m_next)
        for pos, (j, e, c, seen) in enumerate(items):
            m_new = m_next
            n_nxt = 0
            if pos + 1 < len(items):
                m_next = ms[items[pos + 1][1:3]] if items[pos + 1][1:3] != (e, c) else m_new
                n_nxt = items[pos + 1][3] // kc
            pv = None
            for r in range(max(seen // kc, n_nxt)):
                if r < n_nxt:
                    m_next = score_chunk(pos + 1, r, m_next)
                if r < seen // kc:
                    p = jnp.exp2(s_ref[pos % 2, r * kc:(r + 1) * kc, :] - m_new).astype(BF16)
                    pv_r = _dot(vt_ref[e, j][:, r * kc:(r + 1) * kc], p)
                    pv = pv_r if pv is None else pv + pv_r
            cols = slice(c * cw, (c + 1) * cw)
            acc_ref[e, :, cols] = jnp.exp2(ms[(e, c)] - m_new) * acc_ref[e, :, cols] + pv
            ms[(e, c)] = m_new

    if n_q == 1:
        attend(0)
    else:
        for qi in range(n_q):
            pl.when(pl.program_id(2) == qi)(functools.partial(attend, qi))

    halves = [acc_ref[e, :HEAD_DIM, :] * (1.0 / acc_ref[e, HEAD_DIM:HEAD_DIM + 1, :]) for e in range(2)]
    o_ref[...] = jnp.concatenate(halves, axis=0).T.astype(o_ref.dtype)


def _fox(q_aug, k_aug, vt, t0):
    b, t, wq = q_aug.shape
    l = k_aug.shape[1]
    n_pairs = wq // (2 * LANES)
    tq = min(FOX_TQ, t)
    tk = vt.shape[-1]
    assert t0 % tk == 0 and (tq % tk == 0 or t == tq), (t0, tq, tk, t)
    return pl.pallas_call(
        functools.partial(_fox_kernel, t0=t0, tq=tq, tk=tk, n_q=t // tq),
        grid=(b, n_pairs, t // tq),
        in_specs=[pl.BlockSpec((None, tq, 2 * LANES), lambda bi, p, i: (bi, i, p)),
                  pl.BlockSpec((None, l, 2 * LANES), lambda bi, p, i: (bi, 0, p)),
                  pl.BlockSpec((None, 2, l // tk, FOX_VROWS, tk), lambda bi, p, i: (bi, p, 0, 0, 0))],
        out_specs=pl.BlockSpec((None, tq, LANES), lambda bi, p, i: (bi, i, p)),
        out_shape=jax.ShapeDtypeStruct((b, t, n_pairs * LANES), BF16),
        scratch_shapes=[pltpu.VMEM((2, FOX_VROWS, tq), F32), pltpu.VMEM((2, tk, min(FOX_COLS, tq)), F32)],
        compiler_params=_cparams("parallel", "parallel", "arbitrary"), name="fox",
    )(q_aug, k_aug, vt)


def _tile2(g):
    return jnp.concatenate([g, g]).reshape(1, LANES).astype(F32)


def _trunk(x, mem_k, mem_v, hg_states, past, prm):
    b, t, d = x.shape
    depth = prm["norm_mix"].shape[0]
    n_a = prm["w_in_a"].shape[0]
    mem_width = mem_k[0].shape[-1]
    main_width = prm["w_in_b"].shape[2] - mem_width
    n_fox = main_width // HEAD_DIM
    t_pad = -(-t // CHUNK) * CHUNK
    h = x
    new_states = []
    new_kv = None
    for l in range(depth):
        g_mix = prm["norm_mix"][l].reshape(1, d)
        if l < n_a:
            widths = (main_width,) * 4 + (mem_width,)
            pq, pf, pi, pg, pm = _proj_split(h.reshape(b * t, d), g_mix, prm["w_in_a"][l], widths)
            chunked = [jnp.pad(a.reshape(b, t, main_width), ((0, 0), (0, t_pad - t), (0, 0)))
                       for a in (pq, pf, pi, pg)]
            o_main, s_new = _hgrn(*chunked, prm["lb_logits"], prm["hg_gnorm"][l], hg_states[l],
                                  layer=l, t_valid=t)
            o_main = o_main[:, :t]
            new_states.append(s_new)
        else:
            j = l - n_a
            q_aug, pm = _proj_b(h.reshape(b * t, d), g_mix, prm["w_q_b"][j], prm["w_m_b"][j],
                                _tile2(prm["fox_gq"][j]), c3_q.reshape(b * t, LANES), n_fox)
            o_main = _fox(q_aug.reshape(b, t, n_fox * LANES), k_aug, v_t, t0)
        h = _post(h, o_main, pm.reshape(b, t, -1), mem_k[l], mem_v[l], _tile2(prm["mem_gq"][l]),
                  prm["w_out"][l], prm["norm_ffn"][l].reshape(1, d), prm["w_ffn_up"][l],
                  prm["w_ffn_down"][l])
        if l == n_a - 1:
            k_new, v_new, lf_new, lf_pad = _kv_proj(
                h.reshape(b * t, d), prm["norm_kv"].reshape(1, d), prm["w_k"], prm["w_v"],
                _tile2(prm["fox_gk"]), prm["w_f"], prm["b_f"], n_fox)
            k_new = k_new.reshape(b, t, main_width)
            v_new = v_new.reshape(b, t, main_width)
            lf_pad = lf_pad.reshape(b, t, LANES)
            new_kv = (k_new.reshape(b, t, n_fox, HEAD_DIM), v_new.reshape(b, t, n_fox, HEAD_DIM),
                      lf_new.reshape(b, t, n_fox))
            if past is None:
                t0 = 0
                k_aug, v_t, c3 = _kv_prep(k_new, v_new, lf_pad, n_fox)
            else:
                t0 = past[0].shape[1]
                lf_past = jnp.pad(past[2].astype(F32), ((0, 0), (0, 0), (0, LANES - n_fox)))
                k_aug, v_t, c3 = _kv_prep(past[0].reshape(b, t0, main_width), past[1].reshape(b, t0, main_width),
                                          lf_past, n_fox, tail=(k_new, v_new, lf_pad))
            c3_q = c3[:, t0:t0 + t]
    return h, new_states, new_kv


def kernel(x_prompt, x_sample, mem_prompt, state_hgrn_0, state_hgrn_1, cache_fox_k, cache_fox_v, cache_fox_logf, cache_mem_k, cache_mem_v, norm_mix, w_in_a, lb_logits, hg_gnorm, w_in_b, fox_gq, norm_kv, w_kv, b_f, fox_gk, norm_mem, w_mem_kv, mem_gq, mem_gk, w_out, norm_ffn, w_ffn_up, w_ffn_down):
    depth, d = norm_mix.shape
    mem_width = cache_mem_k.shape[-1] * cache_mem_k.shape[-2]
    main_width = w_in_b.shape[2] - mem_width
    n_fox = b_f.shape[0]
    bsz, n_mem, _ = mem_prompt.shape

    w_f = jnp.pad(w_kv[:, 2 * main_width:], ((0, 0), (0, LANES - n_fox)))
    prm = {
        "norm_mix": norm_mix, "w_in_a": w_in_a.astype(BF16), "lb_logits": lb_logits.astype(F32),
        "hg_gnorm": hg_gnorm, "w_in_b": w_in_b,
        "w_q_b": w_in_b[:, :, :main_width].astype(BF16), "w_m_b": w_in_b[:, :, main_width:].astype(BF16),
        "fox_gq": fox_gq, "norm_kv": norm_kv,
        "w_k": w_kv[:, :main_width].astype(BF16), "w_v": w_kv[:, main_width:2 * main_width].astype(BF16),
        "w_f": w_f.astype(BF16), "b_f": jnp.pad(b_f, (0, LANES - n_fox)).reshape(1, LANES).astype(F32),
        "fox_gk": fox_gk, "mem_gq": mem_gq, "w_out": w_out.astype(BF16), "norm_ffn": norm_ffn,
        "w_ffn_up": w_ffn_up.astype(BF16), "w_ffn_down": w_ffn_down.astype(BF16),
    }

    mem_rows = mem_prompt.reshape(bsz * n_mem, d)
    mk, mv, mkb, mvb = [], [], [], []
    for l in range(depth):
        wkv = w_mem_kv[l].astype(BF16)
        k_l, v_l, kb_l, vb_l = _kv_proj(mem_rows, norm_mem[l].reshape(1, d), wkv[:, :mem_width],
                                        wkv[:, mem_width:], _tile2(mem_gk[l]))
        mk.append(k_l)
        mv.append(v_l)
        mkb.append(kb_l.reshape(bsz, n_mem, mem_width))
        mvb.append(vb_l.reshape(bsz, n_mem, mem_width))
    mem_shape = (depth, bsz, n_mem) + cache_mem_k.shape[-2:]
    p_mem_k = jnp.stack(mk).reshape(mem_shape)
    p_mem_v = jnp.stack(mv).reshape(mem_shape)
    s_zero = jnp.zeros((bsz,) + state_hgrn_0.shape[1:], F32)
    y_prompt, p_states, p_kv = _trunk(x_prompt, mkb, mvb, [s_zero] * w_in_a.shape[0], None, prm)

    dec_b = x_sample.shape[0]
    cmk = cache_mem_k.reshape(depth, dec_b, n_mem, mem_width).astype(BF16)
    cmv = cache_mem_v.reshape(depth, dec_b, n_mem, mem_width).astype(BF16)
    y_sample, s_states, s_kv = _trunk(x_sample, cmk, cmv, [state_hgrn_0, state_hgrn_1],
                                      (cache_fox_k, cache_fox_v, cache_fox_logf), prm)
    return (y_prompt, y_sample, p_states[0], p_states[1], p_kv[0], p_kv[1], p_kv[2], p_mem_k, p_mem_v,
            s_states[0], s_states[1], s_kv[0], s_kv[1], s_kv[2])
```

```python
import functools

import numpy as np
import jax
import jax.numpy as jnp
from jax import lax
from jax.experimental import pallas as pl
from jax.experimental.pallas import tpu as pltpu

F32 = jnp.float32
BF16 = jnp.bfloat16

EPS = 1e-6
K_MAX = 0.999999
NEG = -1e30
LOG2E = 1.4426950408889634

LANES = 128
HEAD_DIM = 64
CHUNK = 64
HG_DK = 128
FF_TILE = 256
ROW_TILE = 512
FOX_TQ = 2048
FOX_TK = 1024
FOX_COLS = 512
FOX_KEYS = 256
FOX_VROWS = HEAD_DIM + 16
PREP_SUB = 256
N_SPLIT = 3
ONES_LANE = 36
VMEM_LIMIT_BYTES = 56 * 1024 * 1024


def _cparams(*sem):
    return pltpu.CompilerParams(dimension_semantics=sem, vmem_limit_bytes=VMEM_LIMIT_BYTES)


def _resident(shape):
    nd = len(shape)
    return pl.BlockSpec(shape, lambda *_: (0,) * nd, pipeline_mode=pl.Buffered(1))


def _resident_layer(stacked, layer):
    nd = stacked.ndim - 1
    return pl.BlockSpec((None,) + stacked.shape[1:], lambda *_: (layer,) + (0,) * nd,
                        pipeline_mode=pl.Buffered(1))


def _dot(a, b):
    return jnp.dot(a, b, preferred_element_type=F32)


def _dot_nt(a, b):
    return lax.dot_general(a, b, (((1,), (1,)), ((), ())), preferred_element_type=F32)


def _dot_tn(a, b):
    return lax.dot_general(a, b, (((0,), (0,)), ((), ())), preferred_element_type=F32)


def _split_bf16(x, n):
    parts = []
    r = x
    for _ in range(n):
        p = r.astype(BF16)
        parts.append(p)
        r = r - p.astype(F32)
    return parts


def _rms(x, g):
    ms = jnp.mean(x * x, axis=-1, keepdims=True)
    return x * lax.rsqrt(ms + EPS) * g


def _sigmoid(x):
    return 0.5 + 0.5 * jnp.tanh(0.5 * x)


def _silu(x):
    hx = 0.5 * x
    return hx + hx * jnp.tanh(hx)


def _even_lanes(shape):
    return lax.broadcasted_iota(jnp.int32, shape, len(shape) - 1) < HEAD_DIM


def _pair_rms(x2, g2):
    even = _even_lanes(x2.shape)
    sq = x2 * x2
    se = jnp.sum(jnp.where(even, sq, 0.0), axis=-1, keepdims=True)
    so = jnp.sum(jnp.where(even, 0.0, sq), axis=-1, keepdims=True)
    ms = jnp.where(even, se, so) * (1.0 / HEAD_DIM)
    return x2 * lax.rsqrt(ms + EPS) * g2


def _proj_split_kernel(x_ref, g_ref, w_ref, *out_refs, widths):
    xn = _rms(x_ref[...], g_ref[...]).astype(BF16)
    off = 0
    for o_ref, wd in zip(out_refs, widths):
        o_ref[...] = _dot(xn, w_ref[:, off:off + wd]).astype(o_ref.dtype)
        off += wd


def _proj_split(x, g, w, layer, widths):
    n, d = x.shape
    tm = min(ROW_TILE, n)
    return pl.pallas_call(
        functools.partial(_proj_split_kernel, widths=widths),
        grid=(n // tm,),
        in_specs=[pl.BlockSpec((tm, d), lambda i: (i, 0)), _resident((1, d)), _resident_layer(w, layer)],
        out_specs=[pl.BlockSpec((tm, wd), lambda i: (i, 0)) for wd in widths],
        out_shape=[jax.ShapeDtypeStruct((n, wd), F32) for wd in widths],
        compiler_params=_cparams("parallel"), name="proj_a",
    )(x, g, w)


HG_MXU_LEVELS = (2, 4)
HG_ROW_LEVELS = (8, 16, 32)
HG_BLOCK = 1024


def _hgrn_consts():
    c = CHUNK
    t = np.arange(c)
    m = []
    for h in HG_MXU_LEVELS:
        blk = t // (2 * h)
        second = (t // h) % 2 == 1
        boundary = blk * 2 * h + h
        mh = np.zeros((c, c), np.float32)
        for r in range(c):
            if second[r]:
                mh[r, boundary[r]:r + 1] = 1.0
            else:
                mh[r, r + 1:boundary[r]] = 1.0
        m.append(mh)
    m.append(np.tril(np.ones((c, c), np.float32)))
    masks = []
    for h in (1,) + HG_MXU_LEVELS + HG_ROW_LEVELS:
        blk = t // (2 * h)
        second = (t // h) % 2 == 1
        masks.append((blk[:, None] == blk[None, :]) & second[:, None] & ~second[None, :])
    masks.append(np.eye(c, dtype=bool))
    return np.concatenate(m, 0), np.stack(masks).astype(np.float32)


def _row_level_exponent(b, h):
    pieces = []
    for r0 in range(0, CHUNK, 2 * h):
        rho = b[r0 + h - 1:r0 + h, :]
        pieces.append(rho - b[r0:r0 + h, :])
        pieces.append(b[r0 + h:r0 + 2 * h, :] - rho)
    return jnp.concatenate(pieces, axis=0)


def _hgrn_kernel(pq_ref, pf_ref, pi_ref, pg_ref, lbl_ref, gn_ref, s0_ref, m_ref, lm_ref,
                 o_ref, sfin_ref, st_ref, *, n_heads, t_valid, layer):
    blk_idx = pl.program_id(1)
    tb = pq_ref.shape[0]
    n_mxu = len(HG_MXU_LEVELS)

    @pl.when(blk_idx == 0)
    def _():
        for h in range(n_heads):
            st_ref[h] = s0_ref[h].T

    rows = [lbl_ref[i:i + 1, :] for i in range(lbl_ref.shape[0])]
    mx = functools.reduce(jnp.maximum, rows)
    es = [jnp.exp(r - mx) for r in rows]
    tot = functools.reduce(lambda a, b: a + b, es)
    ps = [e / tot for e in es]
    cum = ps[0]
    for i in range(1, layer + 1):
        cum = cum + ps[i]
    lb = cum - ps[0]

    m_all = m_ref[...]
    gn = gn_ref[...]
    heads = [slice(h * HG_DK, (h + 1) * HG_DK) for h in range(n_heads)]
    odd_row = lax.broadcasted_iota(jnp.int32, (CHUNK, pq_ref.shape[1]), 0) % 2 == 1
    n_lv = lm_ref.shape[0] - 1
    level_masks = [lm_ref[i] > 0.5 for i in range(n_lv)]
    on_diag = lm_ref[n_lv] > 0.5

    per_chunk = []
    for c in range(tb // CHUNK):
        rs = slice(c * CHUNK, (c + 1) * CHUNK)
        q = _silu(pq_ref[rs, :])
        kk = jnp.minimum((1.0 - lb) * _sigmoid(-pf_ref[rs, :]), K_MAX)
        if t_valid is not None:
            row = blk_idx * tb + c * CHUNK + lax.broadcasted_iota(jnp.int32, kk.shape, 0)
            kk = jnp.where(row < t_valid, kk, 0.0)
        f = 1.0 - kk
        g = jnp.log2(f)
        v_b = pi_ref[rs, :].astype(BF16)

        g_hi, g_lo = _split_bf16(g, 2)
        d_all = _dot(m_all, g_hi) + _dot(m_all, g_lo)
        b = d_all[n_mxu * CHUNK:]
        e_levels = [jnp.where(odd_row, f, 1.0)]
        e_levels += [jnp.exp2(d_all[i * CHUNK:(i + 1) * CHUNK]) for i in range(n_mxu)]
        e_levels += [jnp.exp2(_row_level_exponent(b, h)) for h in HG_ROW_LEVELS]

        q_b = q.astype(BF16)
        k_b = kk.astype(BF16)
        a = [jnp.where(on_diag, _dot_nt(q_b[:, sl], k_b[:, sl]), 0.0) for sl in heads]
        for in_level, el in zip(level_masks, e_levels):
            el_b = el.astype(BF16)
            q_l = q_b * el_b
            k_l = k_b * el_b
            a = [jnp.where(in_level, _dot_nt(q_l[:, sl], k_l[:, sl]), a_h) for a_h, sl in zip(a, heads)]

        b_last = b[CHUNK - 1:CHUNK, :]
        per_chunk.append(dict(
            a=[a_h.astype(BF16) for a_h in a], v=v_b, q_e=(q * jnp.exp2(b)).astype(BF16),
            k_e=(kk * jnp.exp2(b_last - b)).astype(BF16), decay=jnp.exp2(b_last)))

    for c, pc in enumerate(per_chunk):
        rs = slice(c * CHUNK, (c + 1) * CHUNK)
        for h, sl in enumerate(heads):
            st = st_ref[h]
            o = _dot_nt(pc["q_e"][:, sl], st.astype(BF16)) + _dot(pc["a"][h], pc["v"][:, sl])
            st_ref[h] = pc["decay"][:, sl] * st + _dot_tn(pc["v"][:, sl], pc["k_e"][:, sl])
            o_ref[rs, sl] = (_rms(o, gn) * _silu(pg_ref[rs, sl])).astype(o_ref.dtype)

    @pl.when(blk_idx == pl.num_programs(1) - 1)
    def _():
        for h in range(n_heads):
            sfin_ref[h] = st_ref[h].T


def _hgrn(pq, pf, pi, pg, lb_logits, gnorm, s0, layer, t_valid):
    b, t, w = pq.shape
    n_heads = w // HG_DK
    tb = HG_BLOCK if t % HG_BLOCK == 0 else CHUNK
    m_np, masks_np = _hgrn_consts()
    m_all = jnp.asarray(m_np, BF16)
    masks = jnp.asarray(masks_np, F32)
    tok = pl.BlockSpec((None, tb, w), lambda i, c: (i, c, 0))
    st_spec = pl.BlockSpec((None, n_heads, HG_DK, HG_DK), lambda i, c: (i, 0, 0, 0))
    return pl.pallas_call(
        functools.partial(_hgrn_kernel, n_heads=n_heads, layer=layer,
                          t_valid=None if t_valid == t else t_valid),
        grid=(b, t // tb),
        in_specs=[tok, tok, tok, tok, _resident(lb_logits.shape), _resident((1, HG_DK)), st_spec,
                  _resident(m_all.shape), _resident(masks.shape)],
        out_specs=[tok, st_spec],
        out_shape=[jax.ShapeDtypeStruct((b, t, w), BF16),
                   jax.ShapeDtypeStruct((b, n_heads, HG_DK, HG_DK), F32)],
        scratch_shapes=[pltpu.VMEM((n_heads, HG_DK, HG_DK), F32)],
        compiler_params=_cparams("parallel", "arbitrary"), name="hgrn",
    )(pq, pf, pi, pg, lb_logits, gnorm.reshape(1, HG_DK), s0, m_all, masks)


def _post_kernel(h_ref, om_ref, pm_ref, mk_ref, mv_ref, gq_ref, wo_ref, gf_ref, wu_ref, wd_ref,
                 out_ref, *, main_width, d_ff):
    nb, tm, d = h_ref.shape
    x = h_ref[...].reshape(nb * tm, d)
    attn = _dot(om_ref[...].reshape(nb * tm, main_width), wo_ref[:main_width, :])
    gq2 = gq_ref[...]
    n_pairs = pm_ref.shape[-1] // LANES
    o_mem = []
    for p in range(n_pairs):
        sl = slice(p * LANES, (p + 1) * LANES)
        per_batch = []
        for bi in range(nb):
            qn = _pair_rms(pm_ref[bi, :, sl], gq2) * (HEAD_DIM ** -0.5)
            even = _even_lanes(qn.shape)
            k2 = mk_ref[bi, :, sl]
            v2 = mv_ref[bi, :, sl]
            halves = []
            for own in (even, jnp.logical_not(even)):
                s = _dot_nt(jnp.where(own, qn, 0.0).astype(BF16), k2)
                e = jnp.exp(s - jnp.max(s, axis=-1, keepdims=True))
                l = jnp.sum(e, axis=-1, keepdims=True)
                halves.append(_dot(e.astype(BF16), v2) * (1.0 / l))
            per_batch.append(jnp.where(even, halves[0], halves[1]))
        o2 = per_batch[0] if nb == 1 else jnp.concatenate(per_batch, axis=0)
        o_mem.append(o2.astype(BF16))
    attn = attn + _dot(jnp.concatenate(o_mem, axis=1), wo_ref[main_width:, :])
    h1 = x + attn

    xn = _rms(h1, gf_ref[...]).astype(BF16)
    acc = jnp.zeros_like(h1)
    for f in range(d_ff // FF_TILE):
        gate = _dot(xn, wu_ref[:, f * FF_TILE:(f + 1) * FF_TILE])
        up = _dot(xn, wu_ref[:, d_ff + f * FF_TILE:d_ff + (f + 1) * FF_TILE])
        act = (_silu(gate) * up).astype(BF16)
        acc = acc + _dot(act, wd_ref[f * FF_TILE:(f + 1) * FF_TILE, :])
    out_ref[...] = (h1 + acc).reshape(nb, tm, d)


def _post(h, o_main, pm, mem_k, mem_v, gq2, w_out, g_ffn, w_up, w_down, layer):
    b, t, d = h.shape
    tm = min(ROW_TILE, t)
    nb = b if b * t <= ROW_TILE else 1
    main_width = o_main.shape[-1]
    mem_width = pm.shape[-1]
    n_mem = mem_k.shape[-2]
    d_ff = w_down.shape[1]
    row = lambda wd: pl.BlockSpec((nb, tm, wd), lambda i, r: (i, r, 0))
    if mem_k.ndim == 4:
        mem = pl.BlockSpec((None, nb, n_mem, mem_width), lambda i, r: (layer, i, 0, 0))
    else:
        mem = pl.BlockSpec((nb, n_mem, mem_width), lambda i, r: (i, 0, 0))
    return pl.pallas_call(
        functools.partial(_post_kernel, main_width=main_width, d_ff=d_ff),
        grid=(b // nb, t // tm),
        in_specs=[row(d), row(main_width), row(mem_width), mem, mem, _resident((1, LANES)),
                  _resident_layer(w_out, layer), _resident((1, d)), _resident_layer(w_up, layer),
                  _resident_layer(w_down, layer)],
        out_specs=row(d),
        out_shape=jax.ShapeDtypeStruct((b, t, d), F32),
        compiler_params=_cparams("parallel", "parallel"), name="post",
    )(h, o_main, pm, mem_k, mem_v, gq2, w_out, g_ffn, w_up, w_down)


def _kv_proj_kernel(*refs, has_f, n_f):
    if has_f:
        (x_ref, g_ref, wk_ref, wv_ref, gk_ref, wf_ref, bf_ref,
         k_ref, v_ref, lf_ref, lfp_ref) = refs
    else:
        x_ref, g_ref, wk_ref, wv_ref, gk_ref, k_ref, v_ref, kb_ref, vb_ref = refs
    xn = _rms(x_ref[...], g_ref[...]).astype(BF16)
    gk2 = gk_ref[...]
    k_raw = _dot(xn, wk_ref[...])
    for p in range(k_ref.shape[-1] // LANES):
        sl = slice(p * LANES, (p + 1) * LANES)
        k2 = _pair_rms(k_raw[:, sl], gk2)
        k_ref[:, sl] = k2
        if not has_f:
            kb_ref[:, sl] = k2.astype(BF16)
    v = _dot(xn, wv_ref[...])
    v_ref[...] = v
    if has_f:
        y = _dot(xn, wf_ref[...]) + bf_ref[...]
        lf = jnp.minimum(y, 0.0) - jnp.log1p(jnp.exp(-jnp.abs(y)))
        lane = lax.broadcasted_iota(jnp.int32, lf.shape, 1)
        lf = jnp.where(lane < n_f, lf, 0.0)
        lfp_ref[...] = lf
        lf_ref[...] = lf[:, :n_f]
    else:
        vb_ref[...] = v.astype(BF16)


def _kv_proj(x, g, wk, wv, gk2, wf=None, bf=None, n_f=0):
    n, d = x.shape
    tm = min(ROW_TILE, n)
    wk_w = wk.shape[1]
    has_f = wf is not None
    row = lambda wd: pl.BlockSpec((tm, wd), lambda i: (i, 0))
    in_specs = [row(d), _resident((1, d)), _resident(wk.shape), _resident(wv.shape),
                _resident((1, LANES))]
    args = [x, g, wk, wv, gk2]
    if has_f:
        in_specs += [_resident(wf.shape), _resident((1, LANES))]
        args += [wf, bf]
        out_specs = [row(wk_w), row(wk_w), row(n_f), row(LANES)]
        out_shape = [jax.ShapeDtypeStruct((n, wk_w), F32), jax.ShapeDtypeStruct((n, wk_w), F32),
                     jax.ShapeDtypeStruct((n, n_f), F32), jax.ShapeDtypeStruct((n, LANES), F32)]
    else:
        out_specs = [row(wk_w)] * 4
        out_shape = [jax.ShapeDtypeStruct((n, wk_w), F32), jax.ShapeDtypeStruct((n, wk_w), F32),
                     jax.ShapeDtypeStruct((n, wk_w), BF16), jax.ShapeDtypeStruct((n, wk_w), BF16)]
    return pl.pallas_call(
        functools.partial(_kv_proj_kernel, has_f=has_f, n_f=n_f),
        grid=(n // tm,),
        in_specs=in_specs, out_specs=out_specs, out_shape=out_shape,
        compiler_params=_cparams("parallel"), name="kv_proj",
    )(*args)


def _bias_lane_base(h):
    return h * LANES + (HEAD_DIM if h % 2 == 0 else 0)


def _pack_consts(n_heads):
    p = np.zeros((N_SPLIT, LANES, LANES), np.float32)
    for s in range(N_SPLIT):
        for h in range(n_heads):
            p[s, h, s * n_heads + h] = 1.0
    ones_row = np.zeros((1, LANES), np.float32)
    ones_row[0, ONES_LANE] = 1.0
    s_k = np.zeros((LANES, n_heads * LANES), np.float32)
    s_q = np.zeros((LANES, n_heads * LANES), np.float32)
    for h in range(n_heads):
        base = _bias_lane_base(h)
        for s in range(N_SPLIT):
            s_q[s * n_heads + h, base + s] = 1.0
            s_q[ONES_LANE, base + N_SPLIT + s] = 1.0
            s_k[ONES_LANE, base + s] = 1.0
            s_k[s * n_heads + h, base + N_SPLIT + s] = -1.0
    return p, ones_row, s_q, s_k


def _own_half(shape, h):
    even = _even_lanes(shape)
    return even if h % 2 == 0 else jnp.logical_not(even)


def _kv_prep_kernel(*refs, n_heads, n_main):
    n_src = 3 if n_main is None else 6
    tril_ref, p_ref, ones_ref, sk_ref, kaug_ref, vt_ref, c3_ref, carry_ref = refs[n_src:]
    tl = kaug_ref.shape[0]
    j = pl.program_id(1)

    @pl.when(j == 0)
    def _():
        carry_ref[...] = jnp.zeros_like(carry_ref)

    def tile(k_get, v_get, lf):
        tril = tril_ref[...]
        sub = tril.shape[0]
        parts = _split_bf16(lf, N_SPLIT)
        total = carry_ref[...]
        sums = []
        for r in range(tl // sub):
            cs_r = total
            for part in parts:
                cs_r = cs_r + _dot(tril, part[r * sub:(r + 1) * sub])
            total = cs_r[sub - 1:sub, :]
            sums.append(cs_r)
        cs = jnp.concatenate(sums, axis=0)
        carry_ref[...] = total

        c3 = ones_ref[...]
        for s, part in enumerate(_split_bf16(cs * LOG2E, N_SPLIT)):
            c3 = c3 + _dot(part, p_ref[s])
        c3 = c3.astype(BF16)
        c3_ref[...] = c3
        kbias = _dot(c3, sk_ref[...])
        ones_block = (lax.broadcasted_iota(jnp.int32, (FOX_VROWS - HEAD_DIM, tl), 0) == 0).astype(F32)
        for p in range(n_heads // 2):
            pair = slice(p * LANES, (p + 1) * LANES)
            k2 = k_get(pair)
            vt2 = v_get(pair).T
            for e in range(2):
                h = 2 * p + e
                slab = slice(h * LANES, (h + 1) * LANES)
                kaug_ref[:, slab] = jnp.where(_own_half(k2.shape, h), k2, kbias[:, slab]).astype(BF16)
                vt_ref[h] = jnp.concatenate([vt2[e * HEAD_DIM:(e + 1) * HEAD_DIM], ones_block], 0).astype(BF16)

    def from_refs(k_ref, v_ref, lf_ref):
        def grow(x):
            if x.shape[0] == tl:
                return x
            return jnp.concatenate([x, jnp.zeros((tl - x.shape[0], x.shape[1]), x.dtype)], 0)
        tile(lambda pair: grow(k_ref[:, pair]), lambda pair: grow(v_ref[:, pair]), grow(lf_ref[...]))

    if n_main is None:
        from_refs(*refs[:3])
    else:
        pl.when(j < n_main)(functools.partial(from_refs, *refs[:3]))
        pl.when(j >= n_main)(functools.partial(from_refs, *refs[3:6]))


def _kv_prep(k, v, lf_pad, n_heads, tail=None):
    b, l, w = k.shape
    tl = min(FOX_TK, l)
    n_main = l // tl
    n_tiles = n_main + (0 if tail is None else 1)
    p_np, ones_np, _, sk_np = _pack_consts(n_heads)
    sub = min(PREP_SUB, tl)
    tril = jnp.asarray(np.tril(np.ones((sub, sub), np.float32)), BF16)
    row = lambda wd: pl.BlockSpec((None, tl, wd), lambda i, j: (i, j, 0))
    main = lambda wd: pl.BlockSpec((None, tl, wd), lambda i, j: (i, jnp.minimum(j, n_main - 1), 0))
    srcs = [k, v, lf_pad]
    src_specs = [main(w), main(w), main(LANES)]
    if tail is not None:
        assert l % tl == 0 and tail[0].shape[1] <= tl
        srcs += list(tail)
        src_specs += [pl.BlockSpec((None,) + a.shape[1:], lambda i, j: (i, 0, 0)) for a in tail]
    return pl.pallas_call(
        functools.partial(_kv_prep_kernel, n_heads=n_heads, n_main=None if tail is None else n_main),
        grid=(b, n_tiles),
        in_specs=src_specs + [_resident((sub, sub)), _resident(p_np.shape),
                              _resident((1, LANES)), _resident(sk_np.shape)],
        out_specs=[row(n_heads * LANES),
                   pl.BlockSpec((None, n_heads, None, FOX_VROWS, tl), lambda i, j: (i, 0, j, 0, 0)),
                   row(LANES)],
        out_shape=[jax.ShapeDtypeStruct((b, n_tiles * tl, n_heads * LANES), BF16),
                   jax.ShapeDtypeStruct((b, n_heads, n_tiles, FOX_VROWS, tl), BF16),
                   jax.ShapeDtypeStruct((b, n_tiles * tl, LANES), BF16)],
        scratch_shapes=[pltpu.VMEM((1, LANES), F32)],
        compiler_params=_cparams("parallel", "arbitrary"), name="kv_prep",
    )(*srcs, tril, jnp.asarray(p_np, BF16), jnp.asarray(ones_np, F32), jnp.asarray(sk_np, BF16))


def _proj_b_kernel(x_ref, g_ref, wq_ref, wm_ref, gq_ref, c3_ref, sq_ref, qaug_ref, pm_ref,
                   *, n_heads):
    xn = _rms(x_ref[...], g_ref[...]).astype(BF16)
    pm_ref[...] = _dot(xn, wm_ref[...])
    qbias = _dot(c3_ref[...], sq_ref[...])
    gq2 = gq_ref[...]
    q_raw = _dot(xn, wq_ref[...])
    for p in range(n_heads // 2):
        pair = slice(p * LANES, (p + 1) * LANES)
        qn = _pair_rms(q_raw[:, pair], gq2) * (LOG2E * HEAD_DIM ** -0.5)
        for h in (2 * p, 2 * p + 1):
            slab = slice(h * LANES, (h + 1) * LANES)
            qaug_ref[:, slab] = jnp.where(_own_half(qn.shape, h), qn, qbias[:, slab]).astype(BF16)


def _proj_b(x, g, wq, wm, layer, gq2, c3q, n_heads):
    n, d = x.shape
    tm = min(ROW_TILE, n)
    _, _, sq_np, _ = _pack_consts(n_heads)
    row = lambda wd: pl.BlockSpec((tm, wd), lambda i: (i, 0))
    return pl.pallas_call(
        functools.partial(_proj_b_kernel, n_heads=n_heads),
        grid=(n // tm,),
        in_specs=[row(d), _resident((1, d)), _resident_layer(wq, layer), _resident_layer(wm, layer),
                  _resident((1, LANES)), row(LANES), _resident(sq_np.shape)],
        out_specs=[row(n_heads * LANES), row(wm.shape[2])],
        out_shape=[jax.ShapeDtypeStruct((n, n_heads * LANES), BF16),
                   jax.ShapeDtypeStruct((n, wm.shape[2]), F32)],
        compiler_params=_cparams("parallel"), name="proj_b",
    )(x, g, wq, wm, gq2, c3q, jnp.asarray(sq_np, BF16))


def _fox_kernel(q_ref, k_ref, vt_ref, o_ref, acc_ref, s_ref, *, t0, tq, tk, n_q):
    cw = min(FOX_COLS, tq)
    kc = min(FOX_KEYS, tk)
    acc_ref[...] = jnp.zeros_like(acc_ref)

    def attend(qi):
        first_tile = t0 // tk + qi * (tq // tk)
        blocks = [(e, c) for e in range(2) for c in range(tq // cw)]
        items = [(j, e, c, tk) for j in range(first_tile) for e, c in blocks]
        for d in range(max(1, tq // tk)):
            for e, c in blocks:
                seen = min(max((c + 1) * cw - d * tk, 0), tk) if cw == FOX_COLS else tk
                if seen > 0:
                    items.append((first_tile + d, e, c, seen))

        def score_chunk(pos, r, m_run):
            j, e, c, _ = items[pos]
            first_key = j * tk + r * kc
            s = _dot_nt(k_ref[first_key:first_key + kc, e * LANES:(e + 1) * LANES],
                        q_ref[c * cw:(c + 1) * cw, e * LANES:(e + 1) * LANES])
            first_query = t0 + qi * tq + c * cw
            if first_key + kc - 1 > first_query:
                key = first_key + lax.broadcasted_iota(jnp.int32, s.shape, 0)
                qry = first_query + lax.broadcasted_iota(jnp.int32, s.shape, 1)
                s = jnp.where(key <= qry, s, NEG)
            s_ref[pos % 2, r * kc:(r + 1) * kc, :] = s
            return jnp.maximum(m_run, jnp.max(s, axis=0, keepdims=True))

        ms = {blk: jnp.full((1, cw), NEG, F32) for blk in blocks}
        m_next = ms[items[0][1:3]]
        for r in range(items[0][3] // kc):
            m_next = score_chunk(0, r, m_next)
        for pos, (j, e, c, seen) in enumerate(items):
            m_new = m_next
            n_nxt = 0
            if pos + 1 < len(items):
                m_next = ms[items[pos + 1][1:3]] if items[pos + 1][1:3] != (e, c) else m_new
                n_nxt = items[pos + 1][3] // kc
            pv = None
            for r in range(max(seen // kc, n_nxt)):
                if r < n_nxt:
                    m_next = score_chunk(pos + 1, r, m_next)
                if r < seen // kc:
                    p = jnp.exp2(s_ref[pos % 2, r * kc:(r + 1) * kc, :] - m_new).astype(BF16)
                    pv_r = _dot(vt_ref[e, j][:, r * kc:(r + 1) * kc], p)
                    pv = pv_r if pv is None else pv + pv_r
            cols = slice(c * cw, (c + 1) * cw)
            acc_ref[e, :, cols] = jnp.exp2(ms[(e, c)] - m_new) * acc_ref[e, :, cols] + pv
            ms[(e, c)] = m_new

    if n_q == 1:
        attend(0)
    else:
        for qi in range(n_q):
            pl.when(pl.program_id(2) == qi)(functools.partial(attend, qi))

    halves = [acc_ref[e, :HEAD_DIM, :] * (1.0 / acc_ref[e, HEAD_DIM:HEAD_DIM + 1, :]) for e in range(2)]
    o_ref[...] = jnp.concatenate(halves, axis=0).T.astype(o_ref.dtype)


def _fox(q_aug, k_aug, vt, t0):
    b, t, wq = q_aug.shape
    l = k_aug.shape[1]
    n_pairs = wq // (2 * LANES)
    tq = min(FOX_TQ, t)
    tk = vt.shape[-1]
    assert t0 % tk == 0 and (tq % tk == 0 or t == tq), (t0, tq, tk, t)
    return pl.pallas_call(
        functools.partial(_fox_kernel, t0=t0, tq=tq, tk=tk, n_q=t // tq),
        grid=(b, n_pairs, t // tq),
        in_specs=[pl.BlockSpec((None, tq, 2 * LANES), lambda bi, p, i: (bi, i, p)),
                  pl.BlockSpec((None, l, 2 * LANES), lambda bi, p, i: (bi, 0, p)),
                  pl.BlockSpec((None, 2, l // tk, FOX_VROWS, tk), lambda bi, p, i: (bi, p, 0, 0, 0))],
        out_specs=pl.BlockSpec((None, tq, LANES), lambda bi, p, i: (bi, i, p)),
        out_shape=jax.ShapeDtypeStruct((b, t, n_pairs * LANES), BF16),
        scratch_shapes=[pltpu.VMEM((2, FOX_VROWS, tq), F32), pltpu.VMEM((2, tk, min(FOX_COLS, tq)), F32)],
        compiler_params=_cparams("parallel", "parallel", "arbitrary"), name="fox",
    )(q_aug, k_aug, vt)


def _tile2(g):
    return jnp.concatenate([g, g]).reshape(1, LANES).astype(F32)


def _trunk(x, mem_k, mem_v, hg_states, past, prm):
    b, t, d = x.shape
    depth = prm["norm_mix"].shape[0]
    n_a = prm["w_in_a"].shape[0]
    mem_width = mem_k[0].shape[-1]
    main_width = prm["w_in_b"].shape[2] - mem_width
    n_fox = main_width // HEAD_DIM
    t_pad = -(-t // CHUNK) * CHUNK
    h = x
    new_states = []
    new_kv = None
    for l in range(depth):
        g_mix = prm["norm_mix"][l].reshape(1, d)
        if l < n_a:
            widths = (main_width,) * 4 + (mem_width,)
            pq, pf, pi, pg, pm = _proj_split(h.reshape(b * t, d), g_mix, prm["w_in_a"], l, widths)
            chunked = [jnp.pad(a.reshape(b, t, main_width), ((0, 0), (0, t_pad - t), (0, 0)))
                       for a in (pq, pf, pi, pg)]
            o_main, s_new = _hgrn(*chunked, prm["lb_logits"], prm["hg_gnorm"][l], hg_states[l],
                                  layer=l, t_valid=t)
            o_main = o_main[:, :t]
            new_states.append(s_new)
        else:
            j = l - n_a
            q_aug, pm = _proj_b(h.reshape(b * t, d), g_mix, prm["w_q_b"], prm["w_m_b"], j,
                                _tile2(prm["fox_gq"][j]), c3_q.reshape(b * t, LANES), n_fox)
            o_main = _fox(q_aug.reshape(b, t, n_fox * LANES), k_aug, v_t, t0)
        stacked = not isinstance(mem_k, (list, tuple))
        h = _post(h, o_main, pm.reshape(b, t, -1), mem_k if stacked else mem_k[l],
                  mem_v if stacked else mem_v[l], _tile2(prm["mem_gq"][l]), prm["w_out"],
                  prm["norm_ffn"][l].reshape(1, d), prm["w_ffn_up"], prm["w_ffn_down"], l)
        if l == n_a - 1:
            k_new, v_new, lf_new, lf_pad = _kv_proj(
                h.reshape(b * t, d), prm["norm_kv"].reshape(1, d), prm["w_k"], prm["w_v"],
                _tile2(prm["fox_gk"]), prm["w_f"], prm["b_f"], n_fox)
            k_new = k_new.reshape(b, t, main_width)
            v_new = v_new.reshape(b, t, main_width)
            lf_pad = lf_pad.reshape(b, t, LANES)
            new_kv = (k_new.reshape(b, t, n_fox, HEAD_DIM), v_new.reshape(b, t, n_fox, HEAD_DIM),
                      lf_new.reshape(b, t, n_fox))
            if past is None:
                t0 = 0
                k_aug, v_t, c3 = _kv_prep(k_new, v_new, lf_pad, n_fox)
            else:
                t0 = past[0].shape[1]
                lf_past = jnp.pad(past[2].astype(F32), ((0, 0), (0, 0), (0, LANES - n_fox)))
                k_aug, v_t, c3 = _kv_prep(past[0].reshape(b, t0, main_width), past[1].reshape(b, t0, main_width),
                                          lf_past, n_fox, tail=(k_new, v_new, lf_pad))
            c3_q = c3[:, t0:t0 + t]
    return h, new_states, new_kv


def kernel(x_prompt, x_sample, mem_prompt, state_hgrn_0, state_hgrn_1, cache_fox_k, cache_fox_v, cache_fox_logf, cache_mem_k, cache_mem_v, norm_mix, w_in_a, lb_logits, hg_gnorm, w_in_b, fox_gq, norm_kv, w_kv, b_f, fox_gk, norm_mem, w_mem_kv, mem_gq, mem_gk, w_out, norm_ffn, w_ffn_up, w_ffn_down):
    depth, d = norm_mix.shape
    mem_width = cache_mem_k.shape[-1] * cache_mem_k.shape[-2]
    main_width = w_in_b.shape[2] - mem_width
    n_fox = b_f.shape[0]
    bsz, n_mem, _ = mem_prompt.shape

    w_f = jnp.pad(w_kv[:, 2 * main_width:], ((0, 0), (0, LANES - n_fox)))
    prm = {
        "norm_mix": norm_mix, "w_in_a": w_in_a.astype(BF16), "lb_logits": lb_logits.astype(F32),
        "hg_gnorm": hg_gnorm, "w_in_b": w_in_b,
        "w_q_b": w_in_b[:, :, :main_width].astype(BF16), "w_m_b": w_in_b[:, :, main_width:].astype(BF16),
        "fox_gq": fox_gq, "norm_kv": norm_kv,
        "w_k": w_kv[:, :main_width].astype(BF16), "w_v": w_kv[:, main_width:2 * main_width].astype(BF16),
        "w_f": w_f.astype(BF16), "b_f": jnp.pad(b_f, (0, LANES - n_fox)).reshape(1, LANES).astype(F32),
        "fox_gk": fox_gk, "mem_gq": mem_gq, "w_out": w_out.astype(BF16), "norm_ffn": norm_ffn,
        "w_ffn_up": w_ffn_up.astype(BF16), "w_ffn_down": w_ffn_down.astype(BF16),
    }

    mem_rows = mem_prompt.reshape(bsz * n_mem, d)
    mk, mv, mkb, mvb = [], [], [], []
    for l in range(depth):
        wkv = w_mem_kv[l].astype(BF16)
        k_l, v_l, kb_l, vb_l = _kv_proj(mem_rows, norm_mem[l].reshape(1, d), wkv[:, :mem_width],
                                        wkv[:, mem_width:], _tile2(mem_gk[l]))
        mk.append(k_l)
        mv.append(v_l)
        mkb.append(kb_l.reshape(bsz, n_mem, mem_width))
        mvb.append(vb_l.reshape(bsz, n_mem, mem_width))
    mem_shape = (depth, bsz, n_mem) + cache_mem_k.shape[-2:]
    p_mem_k = jnp.stack(mk).reshape(mem_shape)
    p_mem_v = jnp.stack(mv).reshape(mem_shape)
    s_zero = jnp.zeros((bsz,) + state_hgrn_0.shape[1:], F32)
    y_prompt, p_states, p_kv = _trunk(x_prompt, mkb, mvb, [s_zero] * w_in_a.shape[0], None, prm)

    dec_b = x_sample.shape[0]
    cmk = cache_mem_k.reshape(depth, dec_b, n_mem, mem_width).astype(BF16)
    cmv = cache_mem_v.reshape(depth, dec_b, n_mem, mem_width).astype(BF16)
    y_sample, s_states, s_kv = _trunk(x_sample, cmk, cmv, [state_hgrn_0, state_hgrn_1],
                                      (cache_fox_k, cache_fox_v, cache_fox_logf), prm)
    return (y_prompt, y_sample, p_states[0], p_states[1], p_kv[0], p_kv[1], p_kv[2], p_mem_k, p_mem_v,
            s_states[0], s_states[1], s_kv[0], s_kv[1], s_kv[2])
```

```python
import functools

import numpy as np
import jax
import jax.numpy as jnp
from jax import lax
from jax.experimental import pallas as pl
from jax.experimental.pallas import tpu as pltpu

F32 = jnp.float32
BF16 = jnp.bfloat16

EPS = 1e-6
K_MAX = 0.999999
NEG = -1e30
LOG2E = 1.4426950408889634

LANES = 128
HEAD_DIM = 64
CHUNK = 64
HG_DK = 128
FF_TILE = 256
ROW_TILE = 512
PROJ_ROW_TILE = 1024
FOX_TQ = 2048
FOX_TK = 1024
FOX_COLS = 512
FOX_KEYS = 256
FOX_VROWS = HEAD_DIM + 16
PREP_SUB = 256
N_SPLIT = 3
ONES_LANE = 36
VMEM_LIMIT_BYTES = 56 * 1024 * 1024


def _cparams(*sem):
    return pltpu.CompilerParams(dimension_semantics=sem, vmem_limit_bytes=VMEM_LIMIT_BYTES)


def _resident(shape):
    nd = len(shape)
    return pl.BlockSpec(shape, lambda *_: (0,) * nd, pipeline_mode=pl.Buffered(1))


def _resident_layer(stacked, layer):
    nd = stacked.ndim - 1
    return pl.BlockSpec((None,) + stacked.shape[1:], lambda *_: (layer,) + (0,) * nd,
                        pipeline_mode=pl.Buffered(1))


def _dot(a, b):
    return jnp.dot(a, b, preferred_element_type=F32)


def _dot_nt(a, b):
    return lax.dot_general(a, b, (((1,), (1,)), ((), ())), preferred_element_type=F32)


def _dot_tn(a, b):
    return lax.dot_general(a, b, (((0,), (0,)), ((), ())), preferred_element_type=F32)


def _split_bf16(x, n):
    parts = []
    r = x
    for _ in range(n):
        p = r.astype(BF16)
        parts.append(p)
        r = r - p.astype(F32)
    return parts


def _rms(x, g):
    ms = jnp.mean(x * x, axis=-1, keepdims=True)
    return x * lax.rsqrt(ms + EPS) * g


def _sigmoid(x):
    return 0.5 + 0.5 * jnp.tanh(0.5 * x)


def _silu(x):
    hx = 0.5 * x
    return hx + hx * jnp.tanh(hx)


def _even_lanes(shape):
    return lax.broadcasted_iota(jnp.int32, shape, len(shape) - 1) < HEAD_DIM


def _pair_rms(x2, g2):
    even = _even_lanes(x2.shape)
    sq = x2 * x2
    se = jnp.sum(jnp.where(even, sq, 0.0), axis=-1, keepdims=True)
    so = jnp.sum(jnp.where(even, 0.0, sq), axis=-1, keepdims=True)
    ms = jnp.where(even, se, so) * (1.0 / HEAD_DIM)
    return x2 * lax.rsqrt(ms + EPS) * g2


def _proj_split_kernel(x_ref, g_ref, w_ref, *out_refs, widths):
    xn = _rms(x_ref[...], g_ref[...]).astype(BF16)
    off = 0
    for o_ref, wd in zip(out_refs, widths):
        o_ref[...] = _dot(xn, w_ref[:, off:off + wd]).astype(o_ref.dtype)
        off += wd


def _proj_split(x, g, w, layer, widths):
    n, d = x.shape
    tm = min(PROJ_ROW_TILE, n)
    return pl.pallas_call(
        functools.partial(_proj_split_kernel, widths=widths),
        grid=(n // tm,),
        in_specs=[pl.BlockSpec((tm, d), lambda i: (i, 0)), _resident((1, d)), _resident_layer(w, layer)],
        out_specs=[pl.BlockSpec((tm, wd), lambda i: (i, 0)) for wd in widths],
        out_shape=[jax.ShapeDtypeStruct((n, wd), F32) for wd in widths],
        compiler_params=_cparams("parallel"), name="proj_a",
    )(x, g, w)


HG_MXU_LEVELS = (2, 4)
HG_ROW_LEVELS = (8, 16, 32)
HG_BLOCK = 1024


def _hgrn_consts():
    c = CHUNK
    t = np.arange(c)
    m = []
    for h in HG_MXU_LEVELS:
        blk = t // (2 * h)
        second = (t // h) % 2 == 1
        boundary = blk * 2 * h + h
        mh = np.zeros((c, c), np.float32)
        for r in range(c):
            if second[r]:
                mh[r, boundary[r]:r + 1] = 1.0
            else:
                mh[r, r + 1:boundary[r]] = 1.0
        m.append(mh)
    m.append(np.tril(np.ones((c, c), np.float32)))
    masks = []
    for h in (1,) + HG_MXU_LEVELS + HG_ROW_LEVELS:
        blk = t // (2 * h)
        second = (t // h) % 2 == 1
        masks.append((blk[:, None] == blk[None, :]) & second[:, None] & ~second[None, :])
    masks.append(np.eye(c, dtype=bool))
    return np.concatenate(m, 0), np.stack(masks).astype(np.float32)


def _row_level_exponent(b, h):
    pieces = []
    for r0 in range(0, CHUNK, 2 * h):
        rho = b[r0 + h - 1:r0 + h, :]
        pieces.append(rho - b[r0:r0 + h, :])
        pieces.append(b[r0 + h:r0 + 2 * h, :] - rho)
    return jnp.concatenate(pieces, axis=0)


def _hgrn_kernel(pq_ref, pf_ref, pi_ref, pg_ref, lbl_ref, gn_ref, s0_ref, m_ref, lm_ref,
                 o_ref, sfin_ref, st_ref, *, n_heads, t_valid, layer):
    blk_idx = pl.program_id(1)
    tb = pq_ref.shape[0]
    n_mxu = len(HG_MXU_LEVELS)

    @pl.when(blk_idx == 0)
    def _():
        for h in range(n_heads):
            st_ref[h] = s0_ref[h].T

    rows = [lbl_ref[i:i + 1, :] for i in range(lbl_ref.shape[0])]
    mx = functools.reduce(jnp.maximum, rows)
    es = [jnp.exp(r - mx) for r in rows]
    tot = functools.reduce(lambda a, b: a + b, es)
    ps = [e / tot for e in es]
    cum = ps[0]
    for i in range(1, layer + 1):
        cum = cum + ps[i]
    lb = cum - ps[0]

    m_all = m_ref[...]
    gn = gn_ref[...]
    heads = [slice(h * HG_DK, (h + 1) * HG_DK) for h in range(n_heads)]
    odd_row = lax.broadcasted_iota(jnp.int32, (CHUNK, pq_ref.shape[1]), 0) % 2 == 1
    n_lv = lm_ref.shape[0] - 1
    level_masks = [lm_ref[i] > 0.5 for i in range(n_lv)]
    on_diag = lm_ref[n_lv] > 0.5

    per_chunk = []
    for c in range(tb // CHUNK):
        rs = slice(c * CHUNK, (c + 1) * CHUNK)
        q = _silu(pq_ref[rs, :])
        kk = jnp.minimum((1.0 - lb) * _sigmoid(-pf_ref[rs, :]), K_MAX)
        if t_valid is not None:
            row = blk_idx * tb + c * CHUNK + lax.broadcasted_iota(jnp.int32, kk.shape, 0)
            kk = jnp.where(row < t_valid, kk, 0.0)
        f = 1.0 - kk
        g = jnp.log2(f)
        v_b = pi_ref[rs, :].astype(BF16)

        g_hi, g_lo = _split_bf16(g, 2)
        d_all = _dot(m_all, g_hi) + _dot(m_all, g_lo)
        b = d_all[n_mxu * CHUNK:]
        e_levels = [jnp.where(odd_row, f, 1.0)]
        e_levels += [jnp.exp2(d_all[i * CHUNK:(i + 1) * CHUNK]) for i in range(n_mxu)]
        e_levels += [jnp.exp2(_row_level_exponent(b, h)) for h in HG_ROW_LEVELS]

        q_b = q.astype(BF16)
        k_b = kk.astype(BF16)
        a = [jnp.where(on_diag, _dot_nt(q_b[:, sl], k_b[:, sl]), 0.0) for sl in heads]
        for in_level, el in zip(level_masks, e_levels):
            el_b = el.astype(BF16)
            q_l = q_b * el_b
            k_l = k_b * el_b
            a = [jnp.where(in_level, _dot_nt(q_l[:, sl], k_l[:, sl]), a_h) for a_h, sl in zip(a, heads)]

        b_last = b[CHUNK - 1:CHUNK, :]
        per_chunk.append(dict(
            a=[a_h.astype(BF16) for a_h in a], v=v_b, q_e=(q * jnp.exp2(b)).astype(BF16),
            k_e=(kk * jnp.exp2(b_last - b)).astype(BF16), decay=jnp.exp2(b_last)))

    for c, pc in enumerate(per_chunk):
        rs = slice(c * CHUNK, (c + 1) * CHUNK)
        for h, sl in enumerate(heads):
            st = st_ref[h]
            o = _dot_nt(pc["q_e"][:, sl], st.astype(BF16)) + _dot(pc["a"][h], pc["v"][:, sl])
            st_ref[h] = pc["decay"][:, sl] * st + _dot_tn(pc["v"][:, sl], pc["k_e"][:, sl])
            o_ref[rs, sl] = (_rms(o, gn) * _silu(pg_ref[rs, sl])).astype(o_ref.dtype)

    @pl.when(blk_idx == pl.num_programs(1) - 1)
    def _():
        for h in range(n_heads):
            sfin_ref[h] = st_ref[h].T


def _hgrn(pq, pf, pi, pg, lb_logits, gnorm, s0, layer, t_valid):
    b, t, w = pq.shape
    n_heads = w // HG_DK
    tb = HG_BLOCK if t % HG_BLOCK == 0 else CHUNK
    m_np, masks_np = _hgrn_consts()
    m_all = jnp.asarray(m_np, BF16)
    masks = jnp.asarray(masks_np, F32)
    tok = pl.BlockSpec((None, tb, w), lambda i, c: (i, c, 0))
    st_spec = pl.BlockSpec((None, n_heads, HG_DK, HG_DK), lambda i, c: (i, 0, 0, 0))
    return pl.pallas_call(
        functools.partial(_hgrn_kernel, n_heads=n_heads, layer=layer,
                          t_valid=None if t_valid == t else t_valid),
        grid=(b, t // tb),
        in_specs=[tok, tok, tok, tok, _resident(lb_logits.shape), _resident((1, HG_DK)), st_spec,
                  _resident(m_all.shape), _resident(masks.shape)],
        out_specs=[tok, st_spec],
        out_shape=[jax.ShapeDtypeStruct((b, t, w), BF16),
                   jax.ShapeDtypeStruct((b, n_heads, HG_DK, HG_DK), F32)],
        scratch_shapes=[pltpu.VMEM((n_heads, HG_DK, HG_DK), F32)],
        compiler_params=_cparams("parallel", "arbitrary"), name="hgrn",
    )(pq, pf, pi, pg, lb_logits, gnorm.reshape(1, HG_DK), s0, m_all, masks)


def _post_kernel(h_ref, om_ref, pm_ref, mk_ref, mv_ref, gq_ref, wo_ref, gf_ref, wu_ref, wd_ref,
                 out_ref, *, main_width, d_ff):
    nb, tm, d = h_ref.shape
    x = h_ref[...].reshape(nb * tm, d)
    attn = _dot(om_ref[...].reshape(nb * tm, main_width), wo_ref[:main_width, :])
    gq2 = gq_ref[...]
    n_pairs = pm_ref.shape[-1] // LANES
    o_mem = []
    for p in range(n_pairs):
        sl = slice(p * LANES, (p + 1) * LANES)
        per_batch = []
        for bi in range(nb):
            qn = _pair_rms(pm_ref[bi, :, sl], gq2) * (HEAD_DIM ** -0.5)
            even = _even_lanes(qn.shape)
            k2 = mk_ref[bi, :, sl]
            v2 = mv_ref[bi, :, sl]
            halves = []
            for own in (even, jnp.logical_not(even)):
                s = _dot_nt(jnp.where(own, qn, 0.0).astype(BF16), k2)
                e = jnp.exp(s - jnp.max(s, axis=-1, keepdims=True))
                l = jnp.sum(e, axis=-1, keepdims=True)
                halves.append(_dot(e.astype(BF16), v2) * (1.0 / l))
            per_batch.append(jnp.where(even, halves[0], halves[1]))
        o2 = per_batch[0] if nb == 1 else jnp.concatenate(per_batch, axis=0)
        o_mem.append(o2.astype(BF16))
    attn = attn + _dot(jnp.concatenate(o_mem, axis=1), wo_ref[main_width:, :])
    h1 = x + attn

    xn = _rms(h1, gf_ref[...]).astype(BF16)
    acc = jnp.zeros_like(h1)
    for f in range(d_ff // FF_TILE):
        gate = _dot(xn, wu_ref[:, f * FF_TILE:(f + 1) * FF_TILE])
        up = _dot(xn, wu_ref[:, d_ff + f * FF_TILE:d_ff + (f + 1) * FF_TILE])
        act = (_silu(gate) * up).astype(BF16)
        acc = acc + _dot(act, wd_ref[f * FF_TILE:(f + 1) * FF_TILE, :])
    out_ref[...] = (h1 + acc).reshape(nb, tm, d)


def _post(h, o_main, pm, mem_k, mem_v, gq2, w_out, g_ffn, w_up, w_down, layer):
    b, t, d = h.shape
    tm = min(ROW_TILE, t)
    nb = b if b * t <= ROW_TILE else 1
    main_width = o_main.shape[-1]
    mem_width = pm.shape[-1]
    n_mem = mem_k.shape[-2]
    d_ff = w_down.shape[1]
    row = lambda wd: pl.BlockSpec((nb, tm, wd), lambda i, r: (i, r, 0))
    if mem_k.ndim == 4:
        mem = pl.BlockSpec((None, nb, n_mem, mem_width), lambda i, r: (layer, i, 0, 0))
    else:
        mem = pl.BlockSpec((nb, n_mem, mem_width), lambda i, r: (i, 0, 0))
    return pl.pallas_call(
        functools.partial(_post_kernel, main_width=main_width, d_ff=d_ff),
        grid=(b // nb, t // tm),
        in_specs=[row(d), row(main_width), row(mem_width), mem, mem, _resident((1, LANES)),
                  _resident_layer(w_out, layer), _resident((1, d)), _resident_layer(w_up, layer),
                  _resident_layer(w_down, layer)],
        out_specs=row(d),
        out_shape=jax.ShapeDtypeStruct((b, t, d), F32),
        compiler_params=_cparams("parallel", "parallel"), name="post",
    )(h, o_main, pm, mem_k, mem_v, gq2, w_out, g_ffn, w_up, w_down)


def _kv_proj_kernel(*refs, has_f, n_f):
    if has_f:
        (x_ref, g_ref, wk_ref, wv_ref, gk_ref, wf_ref, bf_ref,
         k_ref, v_ref, lf_ref, lfp_ref) = refs
    else:
        x_ref, g_ref, wk_ref, wv_ref, gk_ref, k_ref, v_ref, kb_ref, vb_ref = refs
    xn = _rms(x_ref[...], g_ref[...]).astype(BF16)
    gk2 = gk_ref[...]
    k_raw = _dot(xn, wk_ref[...])
    for p in range(k_ref.shape[-1] // LANES):
        sl = slice(p * LANES, (p + 1) * LANES)
        k2 = _pair_rms(k_raw[:, sl], gk2)
        k_ref[:, sl] = k2
        if not has_f:
            kb_ref[:, sl] = k2.astype(BF16)
    v = _dot(xn, wv_ref[...])
    v_ref[...] = v
    if has_f:
        y = _dot(xn, wf_ref[...]) + bf_ref[...]
        lf = jnp.minimum(y, 0.0) - jnp.log1p(jnp.exp(-jnp.abs(y)))
        lane = lax.broadcasted_iota(jnp.int32, lf.shape, 1)
        lf = jnp.where(lane < n_f, lf, 0.0)
        lfp_ref[...] = lf
        lf_ref[...] = lf[:, :n_f]
    else:
        vb_ref[...] = v.astype(BF16)


def _kv_proj(x, g, wk, wv, gk2, wf=None, bf=None, n_f=0):
    n, d = x.shape
    tm = min(PROJ_ROW_TILE, n)
    wk_w = wk.shape[1]
    has_f = wf is not None
    row = lambda wd: pl.BlockSpec((tm, wd), lambda i: (i, 0))
    in_specs = [row(d), _resident((1, d)), _resident(wk.shape), _resident(wv.shape),
                _resident((1, LANES))]
    args = [x, g, wk, wv, gk2]
    if has_f:
        in_specs += [_resident(wf.shape), _resident((1, LANES))]
        args += [wf, bf]
        out_specs = [row(wk_w), row(wk_w), row(n_f), row(LANES)]
        out_shape = [jax.ShapeDtypeStruct((n, wk_w), F32), jax.ShapeDtypeStruct((n, wk_w), F32),
                     jax.ShapeDtypeStruct((n, n_f), F32), jax.ShapeDtypeStruct((n, LANES), F32)]
    else:
        out_specs = [row(wk_w)] * 4
        out_shape = [jax.ShapeDtypeStruct((n, wk_w), F32), jax.ShapeDtypeStruct((n, wk_w), F32),
                     jax.ShapeDtypeStruct((n, wk_w), BF16), jax.ShapeDtypeStruct((n, wk_w), BF16)]
    return pl.pallas_call(
        functools.partial(_kv_proj_kernel, has_f=has_f, n_f=n_f),
        grid=(n // tm,),
        in_specs=in_specs, out_specs=out_specs, out_shape=out_shape,
        compiler_params=_cparams("parallel"), name="kv_proj",
    )(*args)


def _bias_lane_base(h):
    return h * LANES + (HEAD_DIM if h % 2 == 0 else 0)


def _pack_consts(n_heads):
    p = np.zeros((N_SPLIT, LANES, LANES), np.float32)
    for s in range(N_SPLIT):
        for h in range(n_heads):
            p[s, h, s * n_heads + h] = 1.0
    ones_row = np.zeros((1, LANES), np.float32)
    ones_row[0, ONES_LANE] = 1.0
    s_k = np.zeros((LANES, n_heads * LANES), np.float32)
    s_q = np.zeros((LANES, n_heads * LANES), np.float32)
    for h in range(n_heads):
        base = _bias_lane_base(h)
        for s in range(N_SPLIT):
            s_q[s * n_heads + h, base + s] = 1.0
            s_q[ONES_LANE, base + N_SPLIT + s] = 1.0
            s_k[ONES_LANE, base + s] = 1.0
            s_k[s * n_heads + h, base + N_SPLIT + s] = -1.0
    return p, ones_row, s_q, s_k


def _own_half(shape, h):
    even = _even_lanes(shape)
    return even if h % 2 == 0 else jnp.logical_not(even)


def _kv_prep_kernel(*refs, n_heads, n_main):
    n_src = 3 if n_main is None else 6
    tril_ref, p_ref, ones_ref, sk_ref, kaug_ref, vt_ref, c3_ref, carry_ref = refs[n_src:]
    tl = kaug_ref.shape[0]
    j = pl.program_id(1)

    @pl.when(j == 0)
    def _():
        carry_ref[...] = jnp.zeros_like(carry_ref)

    def tile(k_get, v_get, lf):
        tril = tril_ref[...]
        sub = tril.shape[0]
        parts = _split_bf16(lf, N_SPLIT)
        total = carry_ref[...]
        sums = []
        for r in range(tl // sub):
            cs_r = total
            for part in parts:
                cs_r = cs_r + _dot(tril, part[r * sub:(r + 1) * sub])
            total = cs_r[sub - 1:sub, :]
            sums.append(cs_r)
        cs = jnp.concatenate(sums, axis=0)
        carry_ref[...] = total

        c3 = ones_ref[...]
        for s, part in enumerate(_split_bf16(cs * LOG2E, N_SPLIT)):
            c3 = c3 + _dot(part, p_ref[s])
        c3 = c3.astype(BF16)
        c3_ref[...] = c3
        kbias = _dot(c3, sk_ref[...])
        ones_block = (lax.broadcasted_iota(jnp.int32, (FOX_VROWS - HEAD_DIM, tl), 0) == 0).astype(F32)
        for p in range(n_heads // 2):
            pair = slice(p * LANES, (p + 1) * LANES)
            k2 = k_get(pair)
            vt2 = v_get(pair).T
            for e in range(2):
                h = 2 * p + e
                slab = slice(h * LANES, (h + 1) * LANES)
                kaug_ref[:, slab] = jnp.where(_own_half(k2.shape, h), k2, kbias[:, slab]).astype(BF16)
                vt_ref[h] = jnp.concatenate([vt2[e * HEAD_DIM:(e + 1) * HEAD_DIM], ones_block], 0).astype(BF16)

    def from_refs(k_ref, v_ref, lf_ref):
        def grow(x):
            if x.shape[0] == tl:
                return x
            return jnp.concatenate([x, jnp.zeros((tl - x.shape[0], x.shape[1]), x.dtype)], 0)
        tile(lambda pair: grow(k_ref[:, pair]), lambda pair: grow(v_ref[:, pair]), grow(lf_ref[...]))

    if n_main is None:
        from_refs(*refs[:3])
    else:
        pl.when(j < n_main)(functools.partial(from_refs, *refs[:3]))
        pl.when(j >= n_main)(functools.partial(from_refs, *refs[3:6]))


def _kv_prep(k, v, lf_pad, n_heads, tail=None):
    b, l, w = k.shape
    tl = min(FOX_TK, l)
    n_main = l // tl
    n_tiles = n_main + (0 if tail is None else 1)
    p_np, ones_np, _, sk_np = _pack_consts(n_heads)
    sub = min(PREP_SUB, tl)
    tril = jnp.asarray(np.tril(np.ones((sub, sub), np.float32)), BF16)
    row = lambda wd: pl.BlockSpec((None, tl, wd), lambda i, j: (i, j, 0))
    main = lambda wd: pl.BlockSpec((None, tl, wd), lambda i, j: (i, jnp.minimum(j, n_main - 1), 0))
    srcs = [k, v, lf_pad]
    src_specs = [main(w), main(w), main(LANES)]
    if tail is not None:
        assert l % tl == 0 and tail[0].shape[1] <= tl
        srcs += list(tail)
        src_specs += [pl.BlockSpec((None,) + a.shape[1:], lambda i, j: (i, 0, 0)) for a in tail]
    return pl.pallas_call(
        functools.partial(_kv_prep_kernel, n_heads=n_heads, n_main=None if tail is None else n_main),
        grid=(b, n_tiles),
        in_specs=src_specs + [_resident((sub, sub)), _resident(p_np.shape),
                              _resident((1, LANES)), _resident(sk_np.shape)],
        out_specs=[row(n_heads * LANES),
                   pl.BlockSpec((None, n_heads, None, FOX_VROWS, tl), lambda i, j: (i, 0, j, 0, 0)),
                   row(LANES)],
        out_shape=[jax.ShapeDtypeStruct((b, n_tiles * tl, n_heads * LANES), BF16),
                   jax.ShapeDtypeStruct((b, n_heads, n_tiles, FOX_VROWS, tl), BF16),
                   jax.ShapeDtypeStruct((b, n_tiles * tl, LANES), BF16)],
        scratch_shapes=[pltpu.VMEM((1, LANES), F32)],
        compiler_params=_cparams("parallel", "arbitrary"), name="kv_prep",
    )(*srcs, tril, jnp.asarray(p_np, BF16), jnp.asarray(ones_np, F32), jnp.asarray(sk_np, BF16))


def _proj_b_kernel(x_ref, g_ref, wq_ref, wm_ref, gq_ref, c3_ref, sq_ref, qaug_ref, pm_ref,
                   *, n_heads):
    xn = _rms(x_ref[...], g_ref[...]).astype(BF16)
    pm_ref[...] = _dot(xn, wm_ref[...])
    qbias = _dot(c3_ref[...], sq_ref[...])
    gq2 = gq_ref[...]
    q_raw = _dot(xn, wq_ref[...])
    for p in range(n_heads // 2):
        pair = slice(p * LANES, (p + 1) * LANES)
        qn = _pair_rms(q_raw[:, pair], gq2) * (LOG2E * HEAD_DIM ** -0.5)
        for h in (2 * p, 2 * p + 1):
            slab = slice(h * LANES, (h + 1) * LANES)
            qaug_ref[:, slab] = jnp.where(_own_half(qn.shape, h), qn, qbias[:, slab]).astype(BF16)


def _proj_b(x, g, wq, wm, layer, gq2, c3q, n_heads):
    n, d = x.shape
    tm = min(PROJ_ROW_TILE, n)
    _, _, sq_np, _ = _pack_consts(n_heads)
    row = lambda wd: pl.BlockSpec((tm, wd), lambda i: (i, 0))
    return pl.pallas_call(
        functools.partial(_proj_b_kernel, n_heads=n_heads),
        grid=(n // tm,),
        in_specs=[row(d), _resident((1, d)), _resident_layer(wq, layer), _resident_layer(wm, layer),
                  _resident((1, LANES)), row(LANES), _resident(sq_np.shape)],
        out_specs=[row(n_heads * LANES), row(wm.shape[2])],
        out_shape=[jax.ShapeDtypeStruct((n, n_heads * LANES), BF16),
                   jax.ShapeDtypeStruct((n, wm.shape[2]), F32)],
        compiler_params=_cparams("parallel"), name="proj_b",
    )(x, g, wq, wm, gq2, c3q, jnp.asarray(sq_np, BF16))


def _fox_kernel(q_ref, k_ref, vt_ref, o_ref, acc_ref, s_ref, *, t0, tq, tk, n_q):
    cw = min(FOX_COLS, tq)
    kc = min(FOX_KEYS, tk)
    acc_ref[...] = jnp.zeros_like(acc_ref)

    def attend(qi):
        first_tile = t0 // tk + qi * (tq // tk)
        blocks = [(e, c) for e in range(2) for c in range(tq // cw)]
        items = [(j, e, c, tk) for j in range(first_tile) for e, c in blocks]
        for d in range(max(1, tq // tk)):
            for e, c in blocks:
                seen = min(max((c + 1) * cw - d * tk, 0), tk) if cw == FOX_COLS else tk
                if seen > 0:
                    items.append((first_tile + d, e, c, seen))

        def score_chunk(pos, r, m_run):
            j, e, c, _ = items[pos]
            first_key = j * tk + r * kc
            s = _dot_nt(k_ref[first_key:first_key + kc, e * LANES:(e + 1) * LANES],
                        q_ref[c * cw:(c + 1) * cw, e * LANES:(e + 1) * LANES])
            first_query = t0 + qi * tq + c * cw
            if first_key + kc - 1 > first_query:
                key = first_key + lax.broadcasted_iota(jnp.int32, s.shape, 0)
                qry = first_query + lax.broadcasted_iota(jnp.int32, s.shape, 1)
                s = jnp.where(key <= qry, s, NEG)
            s_ref[pos % 2, r * kc:(r + 1) * kc, :] = s
            return jnp.maximum(m_run, jnp.max(s, axis=0, keepdims=True))

        ms = {blk: jnp.full((1, cw), NEG, F32) for blk in blocks}
        m_next = ms[items[0][1:3]]
        for r in range(items[0][3] // kc):
            m_next = score_chunk(0, r, m_next)
        for pos, (j, e, c, seen) in enumerate(items):
            m_new = m_next
            n_nxt = 0
            if pos + 1 < len(items):
                m_next = ms[items[pos + 1][1:3]] if items[pos + 1][1:3] != (e, c) else m_new
                n_nxt = items[pos + 1][3] // kc
            pv = None
            for r in range(max(seen // kc, n_nxt)):
                if r < n_nxt:
                    m_next = score_chunk(pos + 1, r, m_next)
                if r < seen // kc:
                    p = jnp.exp2(s_ref[pos % 2, r * kc:(r + 1) * kc, :] - m_new).astype(BF16)
                    pv_r = _dot(vt_ref[e, j][:, r * kc:(r + 1) * kc], p)
                    pv = pv_r if pv is None else pv + pv_r
            cols = slice(c * cw, (c + 1) * cw)
            acc_ref[e, :, cols] = jnp.exp2(ms[(e, c)] - m_new) * acc_ref[e, :, cols] + pv
            ms[(e, c)] = m_new

    if n_q == 1:
        attend(0)
    else:
        for qi in range(n_q):
            pl.when(pl.program_id(2) == qi)(functools.partial(attend, qi))

    halves = [acc_ref[e, :HEAD_DIM, :] * (1.0 / acc_ref[e, HEAD_DIM:HEAD_DIM + 1, :]) for e in range(2)]
    o_ref[...] = jnp.concatenate(halves, axis=0).T.astype(o_ref.dtype)


def _fox(q_aug, k_aug, vt, t0):
    b, t, wq = q_aug.shape
    l = k_aug.shape[1]
    n_pairs = wq // (2 * LANES)
    tq = min(FOX_TQ, t)
    tk = vt.shape[-1]
    assert t0 % tk == 0 and (tq % tk == 0 or t == tq), (t0, tq, tk, t)
    return pl.pallas_call(
        functools.partial(_fox_kernel, t0=t0, tq=tq, tk=tk, n_q=t // tq),
        grid=(b, n_pairs, t // tq),
        in_specs=[pl.BlockSpec((None, tq, 2 * LANES), lambda bi, p, i: (bi, i, p)),
                  pl.BlockSpec((None, l, 2 * LANES), lambda bi, p, i: (bi, 0, p)),
                  pl.BlockSpec((None, 2, l // tk, FOX_VROWS, tk), lambda bi, p, i: (bi, p, 0, 0, 0))],
        out_specs=pl.BlockSpec((None, tq, LANES), lambda bi, p, i: (bi, i, p)),
        out_shape=jax.ShapeDtypeStruct((b, t, n_pairs * LANES), BF16),
        scratch_shapes=[pltpu.VMEM((2, FOX_VROWS, tq), F32), pltpu.VMEM((2, tk, min(FOX_COLS, tq)), F32)],
        compiler_params=_cparams("parallel", "parallel", "arbitrary"), name="fox",
    )(q_aug, k_aug, vt)


def _tile2(g):
    return jnp.concatenate([g, g]).reshape(1, LANES).astype(F32)


def _trunk(x, mem_k, mem_v, hg_states, past, prm):
    b, t, d = x.shape
    depth = prm["norm_mix"].shape[0]
    n_a = prm["w_in_a"].shape[0]
    mem_width = mem_k[0].shape[-1]
    main_width = prm["w_in_b"].shape[2] - mem_width
    n_fox = main_width // HEAD_DIM
    t_pad = -(-t // CHUNK) * CHUNK
    h = x
    new_states = []
    new_kv = None
    for l in range(depth):
        g_mix = prm["norm_mix"][l].reshape(1, d)
        if l < n_a:
            widths = (main_width,) * 4 + (mem_width,)
            pq, pf, pi, pg, pm = _proj_split(h.reshape(b * t, d), g_mix, prm["w_in_a"], l, widths)
            chunked = [jnp.pad(a.reshape(b, t, main_width), ((0, 0), (0, t_pad - t), (0, 0)))
                       for a in (pq, pf, pi, pg)]
            o_main, s_new = _hgrn(*chunked, prm["lb_logits"], prm["hg_gnorm"][l], hg_states[l],
                                  layer=l, t_valid=t)
            o_main = o_main[:, :t]
            new_states.append(s_new)
        else:
            j = l - n_a
            q_aug, pm = _proj_b(h.reshape(b * t, d), g_mix, prm["w_q_b"], prm["w_m_b"], j,
                                _tile2(prm["fox_gq"][j]), c3_q.reshape(b * t, LANES), n_fox)
            o_main = _fox(q_aug.reshape(b, t, n_fox * LANES), k_aug, v_t, t0)
        stacked = not isinstance(mem_k, (list, tuple))
        h = _post(h, o_main, pm.reshape(b, t, -1), mem_k if stacked else mem_k[l],
                  mem_v if stacked else mem_v[l], _tile2(prm["mem_gq"][l]), prm["w_out"],
                  prm["norm_ffn"][l].reshape(1, d), prm["w_ffn_up"], prm["w_ffn_down"], l)
        if l == n_a - 1:
            k_new, v_new, lf_new, lf_pad = _kv_proj(
                h.reshape(b * t, d), prm["norm_kv"].reshape(1, d), prm["w_k"], prm["w_v"],
                _tile2(prm["fox_gk"]), prm["w_f"], prm["b_f"], n_fox)
            k_new = k_new.reshape(b, t, main_width)
            v_new = v_new.reshape(b, t, main_width)
            lf_pad = lf_pad.reshape(b, t, LANES)
            new_kv = (k_new.reshape(b, t, n_fox, HEAD_DIM), v_new.reshape(b, t, n_fox, HEAD_DIM),
                      lf_new.reshape(b, t, n_fox))
            if past is None:
                t0 = 0
                k_aug, v_t, c3 = _kv_prep(k_new, v_new, lf_pad, n_fox)
            else:
                t0 = past[0].shape[1]
                lf_past = jnp.pad(past[2].astype(F32), ((0, 0), (0, 0), (0, LANES - n_fox)))
                k_aug, v_t, c3 = _kv_prep(past[0].reshape(b, t0, main_width), past[1].reshape(b, t0, main_width),
                                          lf_past, n_fox, tail=(k_new, v_new, lf_pad))
            c3_q = c3[:, t0:t0 + t]
    return h, new_states, new_kv


def kernel(x_prompt, x_sample, mem_prompt, state_hgrn_0, state_hgrn_1, cache_fox_k, cache_fox_v, cache_fox_logf, cache_mem_k, cache_mem_v, norm_mix, w_in_a, lb_logits, hg_gnorm, w_in_b, fox_gq, norm_kv, w_kv, b_f, fox_gk, norm_mem, w_mem_kv, mem_gq, mem_gk, w_out, norm_ffn, w_ffn_up, w_ffn_down):
    depth, d = norm_mix.shape
    mem_width = cache_mem_k.shape[-1] * cache_mem_k.shape[-2]
    main_width = w_in_b.shape[2] - mem_width
    n_fox = b_f.shape[0]
    bsz, n_mem, _ = mem_prompt.shape

    w_f = jnp.pad(w_kv[:, 2 * main_width:], ((0, 0), (0, LANES - n_fox)))
    prm = {
        "norm_mix": norm_mix, "w_in_a": w_in_a.astype(BF16), "lb_logits": lb_logits.astype(F32),
        "hg_gnorm": hg_gnorm, "w_in_b": w_in_b,
        "w_q_b": w_in_b[:, :, :main_width].astype(BF16), "w_m_b": w_in_b[:, :, main_width:].astype(BF16),
        "fox_gq": fox_gq, "norm_kv": norm_kv,
        "w_k": w_kv[:, :main_width].astype(BF16), "w_v": w_kv[:, main_width:2 * main_width].astype(BF16),
        "w_f": w_f.astype(BF16), "b_f": jnp.pad(b_f, (0, LANES - n_fox)).reshape(1, LANES).astype(F32),
        "fox_gk": fox_gk, "mem_gq": mem_gq, "w_out": w_out.astype(BF16), "norm_ffn": norm_ffn,
        "w_ffn_up": w_ffn_up.astype(BF16), "w_ffn_down": w_ffn_down.astype(BF16),
    }

    mem_rows = mem_prompt.reshape(bsz * n_mem, d)
    mk, mv, mkb, mvb = [], [], [], []
    for l in range(depth):
        wkv = w_mem_kv[l].astype(BF16)
        k_l, v_l, kb_l, vb_l = _kv_proj(mem_rows, norm_mem[l].reshape(1, d), wkv[:, :mem_width],
                                        wkv[:, mem_width:], _tile2(mem_gk[l]))
        mk.append(k_l)
        mv.append(v_l)
        mkb.append(kb_l.reshape(bsz, n_mem, mem_width))
        mvb.append(vb_l.reshape(bsz, n_mem, mem_width))
    mem_shape = (depth, bsz, n_mem) + cache_mem_k.shape[-2:]
    p_mem_k = jnp.stack(mk).reshape(mem_shape)
    p_mem_v = jnp.stack(mv).reshape(mem_shape)
    s_zero = jnp.zeros((bsz,) + state_hgrn_0.shape[1:], F32)
    y_prompt, p_states, p_kv = _trunk(x_prompt, mkb, mvb, [s_zero] * w_in_a.shape[0], None, prm)

    dec_b = x_sample.shape[0]
    cmk = cache_mem_k.reshape(depth, dec_b, n_mem, mem_width).astype(BF16)
    cmv = cache_mem_v.reshape(depth, dec_b, n_mem, mem_width).astype(BF16)
    y_sample, s_states, s_kv = _trunk(x_sample, cmk, cmv, [state_hgrn_0, state_hgrn_1],
                                      (cache_fox_k, cache_fox_v, cache_fox_logf), prm)
    return (y_prompt, y_sample, p_states[0], p_states[1], p_kv[0], p_kv[1], p_kv[2], p_mem_k, p_mem_v,
            s_states[0], s_states[1], s_kv[0], s_kv[1], s_kv[2])
```

```python
import functools

import numpy as np
import jax
import jax.numpy as jnp
from jax import lax
from jax.experimental import pallas as pl
from jax.experimental.pallas import tpu as pltpu

F32 = jnp.float32
BF16 = jnp.bfloat16

EPS = 1e-6
K_MAX = 0.999999
NEG = -1e30
LOG2E = 1.4426950408889634

LANES = 128
HEAD_DIM = 64
CHUNK = 64
HG_DK = 128
FF_TILE = 256
ROW_TILE = 512
PROJ_ROW_TILE = 1024
FOX_TQ = 2048
FOX_TK = 1024
FOX_COLS = 512
FOX_KEYS = 512
FOX_VROWS = HEAD_DIM + 16
PREP_SUB = 256
N_SPLIT = 3
ONES_LANE = 36
VMEM_LIMIT_BYTES = 56 * 1024 * 1024


def _cparams(*sem):
    return pltpu.CompilerParams(dimension_semantics=sem, vmem_limit_bytes=VMEM_LIMIT_BYTES)


def _resident(shape):
    nd = len(shape)
    return pl.BlockSpec(shape, lambda *_: (0,) * nd, pipeline_mode=pl.Buffered(1))


def _resident_layer(stacked, layer):
    nd = stacked.ndim - 1
    return pl.BlockSpec((None,) + stacked.shape[1:], lambda *_: (layer,) + (0,) * nd,
                        pipeline_mode=pl.Buffered(1))


def _dot(a, b):
    return jnp.dot(a, b, preferred_element_type=F32)


def _dot_nt(a, b):
    return lax.dot_general(a, b, (((1,), (1,)), ((), ())), preferred_element_type=F32)


def _dot_tn(a, b):
    return lax.dot_general(a, b, (((0,), (0,)), ((), ())), preferred_element_type=F32)


def _split_bf16(x, n):
    parts = []
    r = x
    for _ in range(n):
        p = r.astype(BF16)
        parts.append(p)
        r = r - p.astype(F32)
    return parts


def _rms(x, g):
    ms = jnp.mean(x * x, axis=-1, keepdims=True)
    return x * lax.rsqrt(ms + EPS) * g


def _sigmoid(x):
    return 0.5 + 0.5 * jnp.tanh(0.5 * x)


def _silu(x):
    hx = 0.5 * x
    return hx + hx * jnp.tanh(hx)


def _even_lanes(shape):
    return lax.broadcasted_iota(jnp.int32, shape, len(shape) - 1) < HEAD_DIM


def _pair_rms(x2, g2):
    even = _even_lanes(x2.shape)
    sq = x2 * x2
    se = jnp.sum(jnp.where(even, sq, 0.0), axis=-1, keepdims=True)
    so = jnp.sum(jnp.where(even, 0.0, sq), axis=-1, keepdims=True)
    ms = jnp.where(even, se, so) * (1.0 / HEAD_DIM)
    return x2 * lax.rsqrt(ms + EPS) * g2


def _proj_split_kernel(x_ref, g_ref, w_ref, *out_refs, widths):
    xn = _rms(x_ref[...], g_ref[...]).astype(BF16)
    off = 0
    for o_ref, wd in zip(out_refs, widths):
        o_ref[...] = _dot(xn, w_ref[:, off:off + wd]).astype(o_ref.dtype)
        off += wd


def _proj_split(x, g, w, layer, widths):
    n, d = x.shape
    tm = min(PROJ_ROW_TILE, n)
    return pl.pallas_call(
        functools.partial(_proj_split_kernel, widths=widths),
        grid=(n // tm,),
        in_specs=[pl.BlockSpec((tm, d), lambda i: (i, 0)), _resident((1, d)), _resident_layer(w, layer)],
        out_specs=[pl.BlockSpec((tm, wd), lambda i: (i, 0)) for wd in widths],
        out_shape=[jax.ShapeDtypeStruct((n, wd), F32) for wd in widths],
        compiler_params=_cparams("parallel"), name="proj_a",
    )(x, g, w)


HG_MXU_LEVELS = (2, 4)
HG_ROW_LEVELS = (8, 16, 32)
HG_BLOCK = 1024


def _hgrn_consts():
    c = CHUNK
    t = np.arange(c)
    m = []
    for h in HG_MXU_LEVELS:
        blk = t // (2 * h)
        second = (t // h) % 2 == 1
        boundary = blk * 2 * h + h
        mh = np.zeros((c, c), np.float32)
        for r in range(c):
            if second[r]:
                mh[r, boundary[r]:r + 1] = 1.0
            else:
                mh[r, r + 1:boundary[r]] = 1.0
        m.append(mh)
    m.append(np.tril(np.ones((c, c), np.float32)))
    masks = []
    for h in (1,) + HG_MXU_LEVELS + HG_ROW_LEVELS:
        blk = t // (2 * h)
        second = (t // h) % 2 == 1
        masks.append((blk[:, None] == blk[None, :]) & second[:, None] & ~second[None, :])
    masks.append(np.eye(c, dtype=bool))
    return np.concatenate(m, 0), np.stack(masks).astype(np.float32)


def _row_level_exponent(b, h):
    pieces = []
    for r0 in range(0, CHUNK, 2 * h):
        rho = b[r0 + h - 1:r0 + h, :]
        pieces.append(rho - b[r0:r0 + h, :])
        pieces.append(b[r0 + h:r0 + 2 * h, :] - rho)
    return jnp.concatenate(pieces, axis=0)


def _hgrn_kernel(pq_ref, pf_ref, pi_ref, pg_ref, lbl_ref, gn_ref, s0_ref, m_ref, lm_ref,
                 o_ref, sfin_ref, st_ref, *, n_heads, t_valid, layer):
    blk_idx = pl.program_id(1)
    tb = pq_ref.shape[0]
    n_mxu = len(HG_MXU_LEVELS)

    @pl.when(blk_idx == 0)
    def _():
        for h in range(n_heads):
            st_ref[h] = s0_ref[h].T

    rows = [lbl_ref[i:i + 1, :] for i in range(lbl_ref.shape[0])]
    mx = functools.reduce(jnp.maximum, rows)
    es = [jnp.exp(r - mx) for r in rows]
    tot = functools.reduce(lambda a, b: a + b, es)
    ps = [e / tot for e in es]
    cum = ps[0]
    for i in range(1, layer + 1):
        cum = cum + ps[i]
    lb = cum - ps[0]

    m_all = m_ref[...]
    gn = gn_ref[...]
    heads = [slice(h * HG_DK, (h + 1) * HG_DK) for h in range(n_heads)]
    odd_row = lax.broadcasted_iota(jnp.int32, (CHUNK, pq_ref.shape[1]), 0) % 2 == 1
    n_lv = lm_ref.shape[0] - 1
    level_masks = [lm_ref[i] > 0.5 for i in range(n_lv)]
    on_diag = lm_ref[n_lv] > 0.5

    per_chunk = []
    for c in range(tb // CHUNK):
        rs = slice(c * CHUNK, (c + 1) * CHUNK)
        q = _silu(pq_ref[rs, :])
        kk = jnp.minimum((1.0 - lb) * _sigmoid(-pf_ref[rs, :]), K_MAX)
        if t_valid is not None:
            row = blk_idx * tb + c * CHUNK + lax.broadcasted_iota(jnp.int32, kk.shape, 0)
            kk = jnp.where(row < t_valid, kk, 0.0)
        f = 1.0 - kk
        g = jnp.log2(f)
        v_b = pi_ref[rs, :].astype(BF16)

        g_hi, g_lo = _split_bf16(g, 2)
        d_all = _dot(m_all, g_hi) + _dot(m_all, g_lo)
        b = d_all[n_mxu * CHUNK:]
        e_levels = [jnp.where(odd_row, f, 1.0)]
        e_levels += [jnp.exp2(d_all[i * CHUNK:(i + 1) * CHUNK]) for i in range(n_mxu)]
        e_levels += [jnp.exp2(_row_level_exponent(b, h)) for h in HG_ROW_LEVELS]

        q_b = q.astype(BF16)
        k_b = kk.astype(BF16)
        a = [jnp.where(on_diag, _dot_nt(q_b[:, sl], k_b[:, sl]), 0.0) for sl in heads]
        for in_level, el in zip(level_masks, e_levels):
            el_b = el.astype(BF16)
            q_l = q_b * el_b
            k_l = k_b * el_b
            a = [jnp.where(in_level, _dot_nt(q_l[:, sl], k_l[:, sl]), a_h) for a_h, sl in zip(a, heads)]

        b_last = b[CHUNK - 1:CHUNK, :]
        per_chunk.append(dict(
            a=[a_h.astype(BF16) for a_h in a], v=v_b, q_e=(q * jnp.exp2(b)).astype(BF16),
            k_e=(kk * jnp.exp2(b_last - b)).astype(BF16), decay=jnp.exp2(b_last)))

    for c, pc in enumerate(per_chunk):
        rs = slice(c * CHUNK, (c + 1) * CHUNK)
        for h, sl in enumerate(heads):
            st = st_ref[h]
            o = _dot_nt(pc["q_e"][:, sl], st.astype(BF16)) + _dot(pc["a"][h], pc["v"][:, sl])
            st_ref[h] = pc["decay"][:, sl] * st + _dot_tn(pc["v"][:, sl], pc["k_e"][:, sl])
            o_ref[rs, sl] = (_rms(o, gn) * _silu(pg_ref[rs, sl])).astype(o_ref.dtype)

    @pl.when(blk_idx == pl.num_programs(1) - 1)
    def _():
        for h in range(n_heads):
            sfin_ref[h] = st_ref[h].T


def _hgrn(pq, pf, pi, pg, lb_logits, gnorm, s0, layer, t_valid):
    b, t, w = pq.shape
    n_heads = w // HG_DK
    tb = HG_BLOCK if t % HG_BLOCK == 0 else CHUNK
    m_np, masks_np = _hgrn_consts()
    m_all = jnp.asarray(m_np, BF16)
    masks = jnp.asarray(masks_np, F32)
    tok = pl.BlockSpec((None, tb, w), lambda i, c: (i, c, 0))
    st_spec = pl.BlockSpec((None, n_heads, HG_DK, HG_DK), lambda i, c: (i, 0, 0, 0))
    return pl.pallas_call(
        functools.partial(_hgrn_kernel, n_heads=n_heads, layer=layer,
                          t_valid=None if t_valid == t else t_valid),
        grid=(b, t // tb),
        in_specs=[tok, tok, tok, tok, _resident(lb_logits.shape), _resident((1, HG_DK)), st_spec,
                  _resident(m_all.shape), _resident(masks.shape)],
        out_specs=[tok, st_spec],
        out_shape=[jax.ShapeDtypeStruct((b, t, w), BF16),
                   jax.ShapeDtypeStruct((b, n_heads, HG_DK, HG_DK), F32)],
        scratch_shapes=[pltpu.VMEM((n_heads, HG_DK, HG_DK), F32)],
        compiler_params=_cparams("parallel", "arbitrary"), name="hgrn",
    )(pq, pf, pi, pg, lb_logits, gnorm.reshape(1, HG_DK), s0, m_all, masks)


def _post_kernel(h_ref, om_ref, pm_ref, mk_ref, mv_ref, gq_ref, wo_ref, gf_ref, wu_ref, wd_ref,
                 out_ref, *, main_width, d_ff):
    nb, tm, d = h_ref.shape
    x = h_ref[...].reshape(nb * tm, d)
    attn = _dot(om_ref[...].reshape(nb * tm, main_width), wo_ref[:main_width, :])
    gq2 = gq_ref[...]
    n_pairs = pm_ref.shape[-1] // LANES
    o_mem = []
    for p in range(n_pairs):
        sl = slice(p * LANES, (p + 1) * LANES)
        per_batch = []
        for bi in range(nb):
            qn = _pair_rms(pm_ref[bi, :, sl], gq2) * (HEAD_DIM ** -0.5)
            even = _even_lanes(qn.shape)
            k2 = mk_ref[bi, :, sl]
            v2 = mv_ref[bi, :, sl]
            halves = []
            for own in (even, jnp.logical_not(even)):
                s = _dot_nt(jnp.where(own, qn, 0.0).astype(BF16), k2)
                e = jnp.exp(s - jnp.max(s, axis=-1, keepdims=True))
                l = jnp.sum(e, axis=-1, keepdims=True)
                halves.append(_dot(e.astype(BF16), v2) * (1.0 / l))
            per_batch.append(jnp.where(even, halves[0], halves[1]))
        o2 = per_batch[0] if nb == 1 else jnp.concatenate(per_batch, axis=0)
        o_mem.append(o2.astype(BF16))
    attn = attn + _dot(jnp.concatenate(o_mem, axis=1), wo_ref[main_width:, :])
    h1 = x + attn

    xn = _rms(h1, gf_ref[...]).astype(BF16)
    acc = jnp.zeros_like(h1)
    for f in range(d_ff // FF_TILE):
        gate = _dot(xn, wu_ref[:, f * FF_TILE:(f + 1) * FF_TILE])
        up = _dot(xn, wu_ref[:, d_ff + f * FF_TILE:d_ff + (f + 1) * FF_TILE])
        act = (_silu(gate) * up).astype(BF16)
        acc = acc + _dot(act, wd_ref[f * FF_TILE:(f + 1) * FF_TILE, :])
    out_ref[...] = (h1 + acc).reshape(nb, tm, d)


def _post(h, o_main, pm, mem_k, mem_v, gq2, w_out, g_ffn, w_up, w_down, layer):
    b, t, d = h.shape
    tm = min(ROW_TILE, t)
    nb = b if b * t <= ROW_TILE else 1
    main_width = o_main.shape[-1]
    mem_width = pm.shape[-1]
    n_mem = mem_k.shape[-2]
    d_ff = w_down.shape[1]
    row = lambda wd: pl.BlockSpec((nb, tm, wd), lambda i, r: (i, r, 0))
    if mem_k.ndim == 4:
        mem = pl.BlockSpec((None, nb, n_mem, mem_width), lambda i, r: (layer, i, 0, 0))
    else:
        mem = pl.BlockSpec((nb, n_mem, mem_width), lambda i, r: (i, 0, 0))
    return pl.pallas_call(
        functools.partial(_post_kernel, main_width=main_width, d_ff=d_ff),
        grid=(b // nb, t // tm),
        in_specs=[row(d), row(main_width), row(mem_width), mem, mem, _resident((1, LANES)),
                  _resident_layer(w_out, layer), _resident((1, d)), _resident_layer(w_up, layer),
                  _resident_layer(w_down, layer)],
        out_specs=row(d),
        out_shape=jax.ShapeDtypeStruct((b, t, d), F32),
        compiler_params=_cparams("parallel", "parallel"), name="post",
    )(h, o_main, pm, mem_k, mem_v, gq2, w_out, g_ffn, w_up, w_down)


def _kv_proj_kernel(*refs, has_f, n_f):
    if has_f:
        (x_ref, g_ref, wk_ref, wv_ref, gk_ref, wf_ref, bf_ref,
         k_ref, v_ref, lf_ref, lfp_ref) = refs
    else:
        x_ref, g_ref, wk_ref, wv_ref, gk_ref, k_ref, v_ref, kb_ref, vb_ref = refs
    xn = _rms(x_ref[...], g_ref[...]).astype(BF16)
    gk2 = gk_ref[...]
    k_raw = _dot(xn, wk_ref[...])
    for p in range(k_ref.shape[-1] // LANES):
        sl = slice(p * LANES, (p + 1) * LANES)
        k2 = _pair_rms(k_raw[:, sl], gk2)
        k_ref[:, sl] = k2
        if not has_f:
            kb_ref[:, sl] = k2.astype(BF16)
    v = _dot(xn, wv_ref[...])
    v_ref[...] = v
    if has_f:
        y = _dot(xn, wf_ref[...]) + bf_ref[...]
        lf = jnp.minimum(y, 0.0) - jnp.log1p(jnp.exp(-jnp.abs(y)))
        lane = lax.broadcasted_iota(jnp.int32, lf.shape, 1)
        lf = jnp.where(lane < n_f, lf, 0.0)
        lfp_ref[...] = lf
        lf_ref[...] = lf[:, :n_f]
    else:
        vb_ref[...] = v.astype(BF16)


def _kv_proj(x, g, wk, wv, gk2, wf=None, bf=None, n_f=0):
    n, d = x.shape
    tm = min(PROJ_ROW_TILE, n)
    wk_w = wk.shape[1]
    has_f = wf is not None
    row = lambda wd: pl.BlockSpec((tm, wd), lambda i: (i, 0))
    in_specs = [row(d), _resident((1, d)), _resident(wk.shape), _resident(wv.shape),
                _resident((1, LANES))]
    args = [x, g, wk, wv, gk2]
    if has_f:
        in_specs += [_resident(wf.shape), _resident((1, LANES))]
        args += [wf, bf]
        out_specs = [row(wk_w), row(wk_w), row(n_f), row(LANES)]
        out_shape = [jax.ShapeDtypeStruct((n, wk_w), F32), jax.ShapeDtypeStruct((n, wk_w), F32),
                     jax.ShapeDtypeStruct((n, n_f), F32), jax.ShapeDtypeStruct((n, LANES), F32)]
    else:
        out_specs = [row(wk_w)] * 4
        out_shape = [jax.ShapeDtypeStruct((n, wk_w), F32), jax.ShapeDtypeStruct((n, wk_w), F32),
                     jax.ShapeDtypeStruct((n, wk_w), BF16), jax.ShapeDtypeStruct((n, wk_w), BF16)]
    return pl.pallas_call(
        functools.partial(_kv_proj_kernel, has_f=has_f, n_f=n_f),
        grid=(n // tm,),
        in_specs=in_specs, out_specs=out_specs, out_shape=out_shape,
        compiler_params=_cparams("parallel"), name="kv_proj",
    )(*args)


def _bias_lane_base(h):
    return h * LANES + (HEAD_DIM if h % 2 == 0 else 0)


def _pack_consts(n_heads):
    p = np.zeros((N_SPLIT, LANES, LANES), np.float32)
    for s in range(N_SPLIT):
        for h in range(n_heads):
            p[s, h, s * n_heads + h] = 1.0
    ones_row = np.zeros((1, LANES), np.float32)
    ones_row[0, ONES_LANE] = 1.0
    s_k = np.zeros((LANES, n_heads * LANES), np.float32)
    s_q = np.zeros((LANES, n_heads * LANES), np.float32)
    for h in range(n_heads):
        base = _bias_lane_base(h)
        for s in range(N_SPLIT):
            s_q[s * n_heads + h, base + s] = 1.0
            s_q[ONES_LANE, base + N_SPLIT + s] = 1.0
            s_k[ONES_LANE, base + s] = 1.0
            s_k[s * n_heads + h, base + N_SPLIT + s] = -1.0
    return p, ones_row, s_q, s_k


def _own_half(shape, h):
    even = _even_lanes(shape)
    return even if h % 2 == 0 else jnp.logical_not(even)


def _kv_prep_kernel(*refs, n_heads, n_main):
    n_src = 3 if n_main is None else 6
    tril_ref, p_ref, ones_ref, sk_ref, kaug_ref, vt_ref, c3_ref, carry_ref = refs[n_src:]
    tl = kaug_ref.shape[0]
    j = pl.program_id(1)

    @pl.when(j == 0)
    def _():
        carry_ref[...] = jnp.zeros_like(carry_ref)

    def tile(k_get, v_get, lf):
        tril = tril_ref[...]
        sub = tril.shape[0]
        parts = _split_bf16(lf, N_SPLIT)
        total = carry_ref[...]
        sums = []
        for r in range(tl // sub):
            cs_r = total
            for part in parts:
                cs_r = cs_r + _dot(tril, part[r * sub:(r + 1) * sub])
            total = cs_r[sub - 1:sub, :]
            sums.append(cs_r)
        cs = jnp.concatenate(sums, axis=0)
        carry_ref[...] = total

        c3 = ones_ref[...]
        for s, part in enumerate(_split_bf16(cs * LOG2E, N_SPLIT)):
            c3 = c3 + _dot(part, p_ref[s])
        c3 = c3.astype(BF16)
        c3_ref[...] = c3
        kbias = _dot(c3, sk_ref[...])
        ones_block = (lax.broadcasted_iota(jnp.int32, (FOX_VROWS - HEAD_DIM, tl), 0) == 0).astype(F32)
        for p in range(n_heads // 2):
            pair = slice(p * LANES, (p + 1) * LANES)
            k2 = k_get(pair)
            vt2 = v_get(pair).T
            for e in range(2):
                h = 2 * p + e
                slab = slice(h * LANES, (h + 1) * LANES)
                kaug_ref[:, slab] = jnp.where(_own_half(k2.shape, h), k2, kbias[:, slab]).astype(BF16)
                vt_ref[h] = jnp.concatenate([vt2[e * HEAD_DIM:(e + 1) * HEAD_DIM], ones_block], 0).astype(BF16)

    def from_refs(k_ref, v_ref, lf_ref):
        def grow(x):
            if x.shape[0] == tl:
                return x
            return jnp.concatenate([x, jnp.zeros((tl - x.shape[0], x.shape[1]), x.dtype)], 0)
        tile(lambda pair: grow(k_ref[:, pair]), lambda pair: grow(v_ref[:, pair]), grow(lf_ref[...]))

    if n_main is None:
        from_refs(*refs[:3])
    else:
        pl.when(j < n_main)(functools.partial(from_refs, *refs[:3]))
        pl.when(j >= n_main)(functools.partial(from_refs, *refs[3:6]))


def _kv_prep(k, v, lf_pad, n_heads, tail=None):
    b, l, w = k.shape
    tl = min(FOX_TK, l)
    n_main = l // tl
    n_tiles = n_main + (0 if tail is None else 1)
    p_np, ones_np, _, sk_np = _pack_consts(n_heads)
    sub = min(PREP_SUB, tl)
    tril = jnp.asarray(np.tril(np.ones((sub, sub), np.float32)), BF16)
    row = lambda wd: pl.BlockSpec((None, tl, wd), lambda i, j: (i, j, 0))
    main = lambda wd: pl.BlockSpec((None, tl, wd), lambda i, j: (i, jnp.minimum(j, n_main - 1), 0))
    srcs = [k, v, lf_pad]
    src_specs = [main(w), main(w), main(LANES)]
    if tail is not None:
        assert l % tl == 0 and tail[0].shape[1] <= tl
        srcs += list(tail)
        src_specs += [pl.BlockSpec((None,) + a.shape[1:], lambda i, j: (i, 0, 0)) for a in tail]
    return pl.pallas_call(
        functools.partial(_kv_prep_kernel, n_heads=n_heads, n_main=None if tail is None else n_main),
        grid=(b, n_tiles),
        in_specs=src_specs + [_resident((sub, sub)), _resident(p_np.shape),
                              _resident((1, LANES)), _resident(sk_np.shape)],
        out_specs=[row(n_heads * LANES),
                   pl.BlockSpec((None, n_heads, None, FOX_VROWS, tl), lambda i, j: (i, 0, j, 0, 0)),
                   row(LANES)],
        out_shape=[jax.ShapeDtypeStruct((b, n_tiles * tl, n_heads * LANES), BF16),
                   jax.ShapeDtypeStruct((b, n_heads, n_tiles, FOX_VROWS, tl), BF16),
                   jax.ShapeDtypeStruct((b, n_tiles * tl, LANES), BF16)],
        scratch_shapes=[pltpu.VMEM((1, LANES), F32)],
        compiler_params=_cparams("parallel", "arbitrary"), name="kv_prep",
    )(*srcs, tril, jnp.asarray(p_np, BF16), jnp.asarray(ones_np, F32), jnp.asarray(sk_np, BF16))


def _proj_b_kernel(x_ref, g_ref, wq_ref, wm_ref, gq_ref, c3_ref, sq_ref, qaug_ref, pm_ref,
                   *, n_heads):
    xn = _rms(x_ref[...], g_ref[...]).astype(BF16)
    pm_ref[...] = _dot(xn, wm_ref[...])
    qbias = _dot(c3_ref[...], sq_ref[...])
    gq2 = gq_ref[...]
    q_raw = _dot(xn, wq_ref[...])
    for p in range(n_heads // 2):
        pair = slice(p * LANES, (p + 1) * LANES)
        qn = _pair_rms(q_raw[:, pair], gq2) * (LOG2E * HEAD_DIM ** -0.5)
        for h in (2 * p, 2 * p + 1):
            slab = slice(h * LANES, (h + 1) * LANES)
            qaug_ref[:, slab] = jnp.where(_own_half(qn.shape, h), qn, qbias[:, slab]).astype(BF16)


def _proj_b(x, g, wq, wm, layer, gq2, c3q, n_heads):
    n, d = x.shape
    tm = min(PROJ_ROW_TILE, n)
    _, _, sq_np, _ = _pack_consts(n_heads)
    row = lambda wd: pl.BlockSpec((tm, wd), lambda i: (i, 0))
    return pl.pallas_call(
        functools.partial(_proj_b_kernel, n_heads=n_heads),
        grid=(n // tm,),
        in_specs=[row(d), _resident((1, d)), _resident_layer(wq, layer), _resident_layer(wm, layer),
                  _resident((1, LANES)), row(LANES), _resident(sq_np.shape)],
        out_specs=[row(n_heads * LANES), row(wm.shape[2])],
        out_shape=[jax.ShapeDtypeStruct((n, n_heads * LANES), BF16),
                   jax.ShapeDtypeStruct((n, wm.shape[2]), F32)],
        compiler_params=_cparams("parallel"), name="proj_b",
    )(x, g, wq, wm, gq2, c3q, jnp.asarray(sq_np, BF16))


def _fox_kernel(q_ref, k_ref, vt_ref, o_ref, acc_ref, s_ref, *, t0, tq, tk, n_q):
    cw = min(FOX_COLS, tq)
    kc = min(FOX_KEYS, tk)
    acc_ref[...] = jnp.zeros_like(acc_ref)

    def attend(qi):
        first_tile = t0 // tk + qi * (tq // tk)
        blocks = [(e, c) for e in range(2) for c in range(tq // cw)]
        items = [(j, e, c, tk) for j in range(first_tile) for e, c in blocks]
        for d in range(max(1, tq // tk)):
            for e, c in blocks:
                seen = min(max((c + 1) * cw - d * tk, 0), tk) if cw == FOX_COLS else tk
                if seen > 0:
                    items.append((first_tile + d, e, c, seen))

        def score_chunk(pos, r, m_run):
            j, e, c, _ = items[pos]
            first_key = j * tk + r * kc
            s = _dot_nt(k_ref[first_key:first_key + kc, e * LANES:(e + 1) * LANES],
                        q_ref[c * cw:(c + 1) * cw, e * LANES:(e + 1) * LANES])
            first_query = t0 + qi * tq + c * cw
            if first_key + kc - 1 > first_query:
                key = first_key + lax.broadcasted_iota(jnp.int32, s.shape, 0)
                qry = first_query + lax.broadcasted_iota(jnp.int32, s.shape, 1)
                s = jnp.where(key <= qry, s, NEG)
            s_ref[pos % 2, r * kc:(r + 1) * kc, :] = s
            return jnp.maximum(m_run, jnp.max(s, axis=0, keepdims=True))

        ms = {blk: jnp.full((1, cw), NEG, F32) for blk in blocks}
        m_next = ms[items[0][1:3]]
        for r in range(items[0][3] // kc):
            m_next = score_chunk(0, r, m_next)
        for pos, (j, e, c, seen) in enumerate(items):
            m_new = m_next
            n_nxt = 0
            if pos + 1 < len(items):
                m_next = ms[items[pos + 1][1:3]] if items[pos + 1][1:3] != (e, c) else m_new
                n_nxt = items[pos + 1][3] // kc
            pv = None
            for r in range(max(seen // kc, n_nxt)):
                if r < n_nxt:
                    m_next = score_chunk(pos + 1, r, m_next)
                if r < seen // kc:
                    p = jnp.exp2(s_ref[pos % 2, r * kc:(r + 1) * kc, :] - m_new).astype(BF16)
                    pv_r = _dot(vt_ref[e, j][:, r * kc:(r + 1) * kc], p)
                    pv = pv_r if pv is None else pv + pv_r
            cols = slice(c * cw, (c + 1) * cw)
            acc_ref[e, :, cols] = jnp.exp2(ms[(e, c)] - m_new) * acc_ref[e, :, cols] + pv
            ms[(e, c)] = m_new

    if n_q == 1:
        attend(0)
    else:
        for qi in range(n_q):
            pl.when(pl.program_id(2) == qi)(functools.partial(attend, qi))

    halves = [acc_ref[e, :HEAD_DIM, :] * (1.0 / acc_ref[e, HEAD_DIM:HEAD_DIM + 1, :]) for e in range(2)]
    o_ref[...] = jnp.concatenate(halves, axis=0).T.astype(o_ref.dtype)


def _fox(q_aug, k_aug, vt, t0):
    b, t, wq = q_aug.shape
    l = k_aug.shape[1]
    n_pairs = wq // (2 * LANES)
    tq = min(FOX_TQ, t)
    tk = vt.shape[-1]
    assert t0 % tk == 0 and (tq % tk == 0 or t == tq), (t0, tq, tk, t)
    return pl.pallas_call(
        functools.partial(_fox_kernel, t0=t0, tq=tq, tk=tk, n_q=t // tq),
        grid=(b, n_pairs, t // tq),
        in_specs=[pl.BlockSpec((None, tq, 2 * LANES), lambda bi, p, i: (bi, i, p)),
                  pl.BlockSpec((None, l, 2 * LANES), lambda bi, p, i: (bi, 0, p)),
                  pl.BlockSpec((None, 2, l // tk, FOX_VROWS, tk), lambda bi, p, i: (bi, p, 0, 0, 0))],
        out_specs=pl.BlockSpec((None, tq, LANES), lambda bi, p, i: (bi, i, p)),
        out_shape=jax.ShapeDtypeStruct((b, t, n_pairs * LANES), BF16),
        scratch_shapes=[pltpu.VMEM((2, FOX_VROWS, tq), F32), pltpu.VMEM((2, tk, min(FOX_COLS, tq)), F32)],
        compiler_params=_cparams("parallel", "parallel", "arbitrary"), name="fox",
    )(q_aug, k_aug, vt)


def _tile2(g):
    return jnp.concatenate([g, g]).reshape(1, LANES).astype(F32)


def _trunk(x, mem_k, mem_v, hg_states, past, prm):
    b, t, d = x.shape
    depth = prm["norm_mix"].shape[0]
    n_a = prm["w_in_a"].shape[0]
    mem_width = mem_k[0].shape[-1]
    main_width = prm["w_in_b"].shape[2] - mem_width
    n_fox = main_width // HEAD_DIM
    t_pad = -(-t // CHUNK) * CHUNK
    h = x
    new_states = []
    new_kv = None
    for l in range(depth):
        g_mix = prm["norm_mix"][l].reshape(1, d)
        if l < n_a:
            widths = (main_width,) * 4 + (mem_width,)
            pq, pf, pi, pg, pm = _proj_split(h.reshape(b * t, d), g_mix, prm["w_in_a"], l, widths)
            chunked = [jnp.pad(a.reshape(b, t, main_width), ((0, 0), (0, t_pad - t), (0, 0)))
                       for a in (pq, pf, pi, pg)]
            o_main, s_new = _hgrn(*chunked, prm["lb_logits"], prm["hg_gnorm"][l], hg_states[l],
                                  layer=l, t_valid=t)
            o_main = o_main[:, :t]
            new_states.append(s_new)
        else:
            j = l - n_a
            q_aug, pm = _proj_b(h.reshape(b * t, d), g_mix, prm["w_q_b"], prm["w_m_b"], j,
                                _tile2(prm["fox_gq"][j]), c3_q.reshape(b * t, LANES), n_fox)
            o_main = _fox(q_aug.reshape(b, t, n_fox * LANES), k_aug, v_t, t0)
        stacked = not isinstance(mem_k, (list, tuple))
        h = _post(h, o_main, pm.reshape(b, t, -1), mem_k if stacked else mem_k[l],
                  mem_v if stacked else mem_v[l], _tile2(prm["mem_gq"][l]), prm["w_out"],
                  prm["norm_ffn"][l].reshape(1, d), prm["w_ffn_up"], prm["w_ffn_down"], l)
        if l == n_a - 1:
            k_new, v_new, lf_new, lf_pad = _kv_proj(
                h.reshape(b * t, d), prm["norm_kv"].reshape(1, d), prm["w_k"], prm["w_v"],
                _tile2(prm["fox_gk"]), prm["w_f"], prm["b_f"], n_fox)
            k_new = k_new.reshape(b, t, main_width)
            v_new = v_new.reshape(b, t, main_width)
            lf_pad = lf_pad.reshape(b, t, LANES)
            new_kv = (k_new.reshape(b, t, n_fox, HEAD_DIM), v_new.reshape(b, t, n_fox, HEAD_DIM),
                      lf_new.reshape(b, t, n_fox))
            if past is None:
                t0 = 0
                k_aug, v_t, c3 = _kv_prep(k_new, v_new, lf_pad, n_fox)
            else:
                t0 = past[0].shape[1]
                lf_past = jnp.pad(past[2].astype(F32), ((0, 0), (0, 0), (0, LANES - n_fox)))
                k_aug, v_t, c3 = _kv_prep(past[0].reshape(b, t0, main_width), past[1].reshape(b, t0, main_width),
                                          lf_past, n_fox, tail=(k_new, v_new, lf_pad))
            c3_q = c3[:, t0:t0 + t]
    return h, new_states, new_kv


def kernel(x_prompt, x_sample, mem_prompt, state_hgrn_0, state_hgrn_1, cache_fox_k, cache_fox_v, cache_fox_logf, cache_mem_k, cache_mem_v, norm_mix, w_in_a, lb_logits, hg_gnorm, w_in_b, fox_gq, norm_kv, w_kv, b_f, fox_gk, norm_mem, w_mem_kv, mem_gq, mem_gk, w_out, norm_ffn, w_ffn_up, w_ffn_down):
    depth, d = norm_mix.shape
    mem_width = cache_mem_k.shape[-1] * cache_mem_k.shape[-2]
    main_width = w_in_b.shape[2] - mem_width
    n_fox = b_f.shape[0]
    bsz, n_mem, _ = mem_prompt.shape

    w_f = jnp.pad(w_kv[:, 2 * main_width:], ((0, 0), (0, LANES - n_fox)))
    prm = {
        "norm_mix": norm_mix, "w_in_a": w_in_a.astype(BF16), "lb_logits": lb_logits.astype(F32),
        "hg_gnorm": hg_gnorm, "w_in_b": w_in_b,
        "w_q_b": w_in_b[:, :, :main_width].astype(BF16), "w_m_b": w_in_b[:, :, main_width:].astype(BF16),
        "fox_gq": fox_gq, "norm_kv": norm_kv,
        "w_k": w_kv[:, :main_width].astype(BF16), "w_v": w_kv[:, main_width:2 * main_width].astype(BF16),
        "w_f": w_f.astype(BF16), "b_f": jnp.pad(b_f, (0, LANES - n_fox)).reshape(1, LANES).astype(F32),
        "fox_gk": fox_gk, "mem_gq": mem_gq, "w_out": w_out.astype(BF16), "norm_ffn": norm_ffn,
        "w_ffn_up": w_ffn_up.astype(BF16), "w_ffn_down": w_ffn_down.astype(BF16),
    }

    mem_rows = mem_prompt.reshape(bsz * n_mem, d)
    mk, mv, mkb, mvb = [], [], [], []
    for l in range(depth):
        wkv = w_mem_kv[l].astype(BF16)
        k_l, v_l, kb_l, vb_l = _kv_proj(mem_rows, norm_mem[l].reshape(1, d), wkv[:, :mem_width],
                                        wkv[:, mem_width:], _tile2(mem_gk[l]))
        mk.append(k_l)
        mv.append(v_l)
        mkb.append(kb_l.reshape(bsz, n_mem, mem_width))
        mvb.append(vb_l.reshape(bsz, n_mem, mem_width))
    mem_shape = (depth, bsz, n_mem) + cache_mem_k.shape[-2:]
    p_mem_k = jnp.stack(mk).reshape(mem_shape)
    p_mem_v = jnp.stack(mv).reshape(mem_shape)
    s_zero = jnp.zeros((bsz,) + state_hgrn_0.shape[1:], F32)
    y_prompt, p_states, p_kv = _trunk(x_prompt, mkb, mvb, [s_zero] * w_in_a.shape[0], None, prm)

    dec_b = x_sample.shape[0]
    cmk = cache_mem_k.reshape(depth, dec_b, n_mem, mem_width).astype(BF16)
    cmv = cache_mem_v.reshape(depth, dec_b, n_mem, mem_width).astype(BF16)
    y_sample, s_states, s_kv = _trunk(x_sample, cmk, cmv, [state_hgrn_0, state_hgrn_1],
                                      (cache_fox_k, cache_fox_v, cache_fox_logf), prm)
    return (y_prompt, y_sample, p_states[0], p_states[1], p_kv[0], p_kv[1], p_kv[2], p_mem_k, p_mem_v,
            s_states[0], s_states[1], s_kv[0], s_kv[1], s_kv[2])
```
